```python
import jax, jax.numpy as jnp
from jax import lax
import numpy as np

D_MODEL = 1024
BATCH = 16
SEQ = 2048
DEPTH = 1
DEC_BATCH = 2
DEC_SEQ = 8192
PAST_LEN = 128

HEAD_DIM = 64
N_Q_HEADS = 8
N_KV_HEADS = 2
GQA_GROUP = N_Q_HEADS // N_KV_HEADS
ATTN_WIDTH = N_Q_HEADS * HEAD_DIM
KV_WIDTH = N_KV_HEADS * HEAD_DIM
WINDOW = 128
BLOCK = 128
ROPE_THETA = 500000.0
ROT_DIM = HEAD_DIM // 4
N_FOURIER_GROUPS = 4
FOURIER_GROUP_WIDTH = 128
FOURIER_WIDTH = N_FOURIER_GROUPS * FOURIER_GROUP_WIDTH
N_BRANCHES = 2
IN_WIDTH = ATTN_WIDTH + 2 * KV_WIDTH + FOURIER_WIDTH + N_BRANCHES * D_MODEL
SPLITS = [ATTN_WIDTH, ATTN_WIDTH + KV_WIDTH, ATTN_WIDTH + 2 * KV_WIDTH,
          ATTN_WIDTH + 2 * KV_WIDTH + FOURIER_WIDTH]
D_FF = 4 * D_MODEL
N_META = 16
META_PAD = BLOCK - N_META
RMS_EPS = 1e-6
NEG_INF = -1e30

kernel_name = "hybrid_gated_swa_fnet_encoder"


def rms_norm(x, g):
    xf = x.astype(jnp.float32)
    y = xf * lax.rsqrt(jnp.mean(xf * xf, axis=-1, keepdims=True) + RMS_EPS)
    return (y * g.astype(jnp.float32)).astype(x.dtype)


def rope_partial(x, pos):
    half = ROT_DIM // 2
    inv_freq = ROPE_THETA ** (-jnp.arange(half, dtype=jnp.float32) / half)
    ang = pos[:, None] * inv_freq[None, :]
    cos = jnp.cos(ang)[None, :, None, :].astype(x.dtype)
    sin = jnp.sin(ang)[None, :, None, :].astype(x.dtype)
    x1, x2, rest = x[..., :half], x[..., half:ROT_DIM], x[..., ROT_DIM:]
    return jnp.concatenate([x1 * cos - x2 * sin, x2 * cos + x1 * sin, rest], axis=-1)


def _band(a, nb, axis):
    return jnp.concatenate([lax.slice_in_dim(a, i, i + nb, axis=axis) for i in range(3)], axis=axis + 1)


def window_attention(q, k, v, sink):
    B, L = q.shape[0], q.shape[1]
    T = META_PAD + L
    nb = T // BLOCK
    scale = HEAD_DIM ** -0.5
    k_meta, v_meta = k[:, :N_META], v[:, :N_META]
    qb = jnp.pad(q, ((0, 0), (META_PAD, 0), (0, 0), (0, 0))).reshape(
        B, nb, BLOCK, N_KV_HEADS, GQA_GROUP, HEAD_DIM)
    kv_pad = ((0, 0), (META_PAD + BLOCK, BLOCK), (0, 0), (0, 0))
    kb = _band(jnp.pad(k, kv_pad).reshape(B, nb + 2, BLOCK, N_KV_HEADS, HEAD_DIM), nb, 1)
    vb = _band(jnp.pad(v, kv_pad).reshape(B, nb + 2, BLOCK, N_KV_HEADS, HEAD_DIM), nb, 1)
    pos_q = (jnp.arange(T) - META_PAD).reshape(nb, BLOCK)
    pos_k = _band((jnp.arange(T + 2 * BLOCK) - META_PAD - BLOCK).reshape(nb + 2, BLOCK), nb, 0)
    key_real = (pos_k >= N_META) & (pos_k < L)
    mask = (jnp.abs(pos_q[:, :, None] - pos_k[:, None, :]) <= WINDOW) & key_real[:, None, :]
    s_band = jnp.einsum('bnqhgd,bnkhd->bnhgqk', qb, kb, preferred_element_type=jnp.float32) * scale
    s_band = jnp.where(mask[None, :, None, None], s_band, NEG_INF)
    s_meta = jnp.einsum('bnqhgd,bmhd->bnhgqm', qb, k_meta, preferred_element_type=jnp.float32) * scale
    sink_l = jnp.broadcast_to(sink.astype(jnp.float32).reshape(1, 1, N_KV_HEADS, GQA_GROUP, 1, 1),
                              s_meta.shape[:-1] + (1,))
    p = jax.nn.softmax(jnp.concatenate([sink_l, s_meta, s_band], axis=-1), axis=-1)
    p_meta = p[..., 1:1 + N_META].astype(v.dtype)
    p_band = p[..., 1 + N_META:].astype(v.dtype)
    o = (jnp.einsum('bnhgqm,bmhd->bnqhgd', p_meta, v_meta)
         + jnp.einsum('bnhgqk,bnkhd->bnqhgd', p_band, vb))
    return o.reshape(B, T, ATTN_WIDTH)[:, META_PAD:]


def fourier_mix(u):
    B, L = u.shape[0], u.shape[1]
    ug = u.astype(jnp.float32).reshape(B, L, N_FOURIER_GROUPS, FOURIER_GROUP_WIDTH)
    f = jnp.fft.fft2(ug, axes=(1, 3), norm="ortho").real
    return f.reshape(B, L, FOURIER_WIDTH).astype(u.dtype)


def mixer_block(h, pos, norm_g, w_in, b_gate, sink, w_attn_out, w_fourier, w_out):
    B, L = h.shape[0], h.shape[1]
    z = rms_norm(h, norm_g) @ w_in
    q, k, v, u, g = jnp.split(z, SPLITS, axis=-1)
    q = rope_partial(q.reshape(B, L, N_Q_HEADS, HEAD_DIM), pos)
    k = rope_partial(k.reshape(B, L, N_KV_HEADS, HEAD_DIM), pos)
    v = v.reshape(B, L, N_KV_HEADS, HEAD_DIM)
    a = window_attention(q, k, v, sink) @ w_attn_out
    f = fourier_mix(u) @ w_fourier
    gates = jax.nn.sigmoid((g + b_gate).astype(jnp.float32)).astype(h.dtype)
    merged = gates[..., :D_MODEL] * a + gates[..., D_MODEL:] * f
    return merged @ w_out


def sq_relu_mlp(h, norm_g, w_up, w_down):
    t = rms_norm(h, norm_g) @ w_up
    return jnp.square(jax.nn.relu(t)) @ w_down


def trunk(x, meta_tokens, norm_mix_g, w_in, b_gate, attn_sink, w_attn_out, w_fourier, w_out,
          norm_mlp_g, w_mlp_up, w_mlp_down, norm_final_g):
    B = x.shape[0]
    meta = jnp.broadcast_to(meta_tokens[None].astype(x.dtype), (B, N_META, D_MODEL))
    h = jnp.concatenate([meta, x], axis=1)
    pos = jnp.arange(h.shape[1], dtype=jnp.float32)
    for l in range(DEPTH):
        h = h + mixer_block(h, pos, norm_mix_g[l], w_in[l], b_gate[l], attn_sink[l],
                            w_attn_out[l], w_fourier[l], w_out[l])
        h = h + sq_relu_mlp(h, norm_mlp_g[l], w_mlp_up[l], w_mlp_down[l])
    h = rms_norm(h, norm_final_g)
    return h[:, N_META:]


def setup_inputs(seed: int = 0) -> dict:
    key = jax.random.key(seed)
    ks = jax.random.split(key, 16)
    nrm = lambda k, shape, s: jax.random.normal(k, shape, jnp.float32) * s
    return {
        "x_prompt": nrm(ks[0], (BATCH, SEQ, D_MODEL), 1.0),
        "x_sample": nrm(ks[1], (DEC_BATCH, DEC_SEQ, D_MODEL), 1.0),
        "meta_tokens": nrm(ks[2], (N_META, D_MODEL), 1.0),
        "norm_mix_g": 1.0 + nrm(ks[3], (DEPTH, D_MODEL), 0.02),
        "w_in": nrm(ks[4], (DEPTH, D_MODEL, IN_WIDTH), D_MODEL ** -0.5),
        "b_gate": nrm(ks[5], (DEPTH, N_BRANCHES * D_MODEL), 0.1),
        "attn_sink": nrm(ks[6], (DEPTH, N_Q_HEADS), 0.5),
        "w_attn_out": nrm(ks[7], (DEPTH, ATTN_WIDTH, D_MODEL), ATTN_WIDTH ** -0.5),
        "w_fourier": nrm(ks[8], (DEPTH, FOURIER_WIDTH, D_MODEL), FOURIER_WIDTH ** -0.5),
        "w_out": nrm(ks[9], (DEPTH, D_MODEL, D_MODEL), D_MODEL ** -0.5),
        "norm_mlp_g": 1.0 + nrm(ks[10], (DEPTH, D_MODEL), 0.02),
        "w_mlp_up": nrm(ks[11], (DEPTH, D_MODEL, D_FF), D_MODEL ** -0.5),
        "w_mlp_down": nrm(ks[12], (DEPTH, D_FF, D_MODEL), D_FF ** -0.5),
        "norm_final_g": 1.0 + nrm(ks[13], (D_MODEL,), 0.02),
    }


def reference(x_prompt, x_sample, meta_tokens, norm_mix_g, w_in, b_gate, attn_sink, w_attn_out,
              w_fourier, w_out, norm_mlp_g, w_mlp_up, w_mlp_down, norm_final_g):
    y_prompt = trunk(x_prompt, meta_tokens, norm_mix_g, w_in, b_gate, attn_sink, w_attn_out,
                     w_fourier, w_out, norm_mlp_g, w_mlp_up, w_mlp_down, norm_final_g)
    y_sample = trunk(x_sample, meta_tokens, norm_mix_g, w_in, b_gate, attn_sink, w_attn_out,
                     w_fourier, w_out, norm_mlp_g, w_mlp_up, w_mlp_down, norm_final_g)
    return (y_prompt, y_sample)
```

```python
import functools

import numpy as np
import jax
import jax.numpy as jnp
from jax import lax
from jax.experimental import pallas as pl
from jax.experimental.pallas import tpu as pltpu

D_MODEL = 1024
HEAD_DIM = 64
N_Q_HEADS = 8
N_KV_HEADS = 2
ATTN_WIDTH = N_Q_HEADS * HEAD_DIM
KV_WIDTH = N_KV_HEADS * HEAD_DIM
WINDOW = 128
ROPE_THETA = 500000.0
ROT_DIM = HEAD_DIM // 4
N_FOURIER_GROUPS = 4
FOURIER_GROUP_WIDTH = 128
FOURIER_WIDTH = N_FOURIER_GROUPS * FOURIER_GROUP_WIDTH
N_BRANCHES = 2
IN_WIDTH = ATTN_WIDTH + 2 * KV_WIDTH + FOURIER_WIDTH + N_BRANCHES * D_MODEL
D_FF = 4 * D_MODEL
N_META = 16
RMS_EPS = 1e-6
NEG_INF = -1e30

LANES = 128
SUBLANES = 8
VMEM_LIMIT_BYTES = 56 * 1024 * 1024

_Q0, _K0, _V0, _U0, _G0 = 0, ATTN_WIDTH, ATTN_WIDTH + KV_WIDTH, ATTN_WIDTH + 2 * KV_WIDTH, \
    ATTN_WIDTH + 2 * KV_WIDTH + FOURIER_WIDTH

BF16 = jnp.bfloat16
F32 = jnp.float32


def _dot(a, b):
    return jnp.dot(a, b, preferred_element_type=F32)


def _dot_nt(a, b):
    return lax.dot_general(a, b, (((1,), (1,)), ((), ())), preferred_element_type=F32)


def _rms(x, g):
    return x * lax.rsqrt(jnp.mean(x * x, axis=-1, keepdims=True) + RMS_EPS) * g


def _proj_kernel(x_ref, cos_ref, sa_ref, sb_ref, g_ref, w_ref, b_ref,
                 q_ref, kk_ref, vv_ref, u_ref, gate_ref):
    n = _rms(x_ref[...], g_ref[...]).astype(BF16)
    cos_t, sin_a, sin_b = cos_ref[...], sa_ref[...], sb_ref[...]

    def rope(z):
        return (z * cos_t + pltpu.roll(z, ROT_DIM // 2, 1) * sin_a
                + pltpu.roll(z, LANES - ROT_DIM // 2, 1) * sin_b)

    zq = _dot(n, w_ref[:, _Q0:_K0])
    for c in range(ATTN_WIDTH // LANES):
        q_ref[:, c * LANES:(c + 1) * LANES] = (
            rope(zq[:, c * LANES:(c + 1) * LANES]) * (HEAD_DIM ** -0.5)).astype(BF16)
    zkv = _dot(n, w_ref[:, _K0:_U0])
    zk = rope(zkv[:, :KV_WIDTH])
    zv = zkv[:, KV_WIDTH:]
    kk_ref[:, :KV_WIDTH] = zk.astype(BF16)
    kk_ref[:, KV_WIDTH:] = pltpu.roll(zk, HEAD_DIM, 1).astype(BF16)
    vv_ref[:, :KV_WIDTH] = zv.astype(BF16)
    vv_ref[:, KV_WIDTH:] = pltpu.roll(zv, HEAD_DIM, 1).astype(BF16)
    u_ref[...] = _dot(n, w_ref[:, _U0:_G0]).astype(BF16)
    for c in range(N_BRANCHES):
        lo = _G0 + c * D_MODEL
        g = _dot(n, w_ref[:, lo:lo + D_MODEL]) + b_ref[:, c * D_MODEL:(c + 1) * D_MODEL]
        gate_ref[:, c * D_MODEL:(c + 1) * D_MODEL] = jax.nn.sigmoid(g).astype(BF16)


def _proj(x, tables, g, w_in, b_gate, *, tm):
    B, S, _ = x.shape
    cos_t, sin_a, sin_b = tables
    tok = lambda w: pl.BlockSpec((None, tm, w), lambda b, i: (b, i, 0))
    tab = pl.BlockSpec((tm, LANES), lambda b, i: (i, 0))
    const = lambda shape: pl.BlockSpec(shape, lambda b, i: (0,) * len(shape))
    out_w = (ATTN_WIDTH, 2 * KV_WIDTH, 2 * KV_WIDTH, FOURIER_WIDTH, N_BRANCHES * D_MODEL)
    return pl.pallas_call(
        _proj_kernel,
        grid=(B, S // tm),
        in_specs=[tok(D_MODEL), tab, tab, tab, const((1, D_MODEL)),
                  const((D_MODEL, IN_WIDTH)), const((1, N_BRANCHES * D_MODEL))],
        out_specs=[tok(w) for w in out_w],
        out_shape=[jax.ShapeDtypeStruct((B, S, w), BF16) for w in out_w],
        compiler_params=pltpu.CompilerParams(
            dimension_semantics=("parallel", "parallel"), vmem_limit_bytes=VMEM_LIMIT_BYTES),
        name="proj",
    )(x, cos_t, sin_a, sin_b, g, w_in, b_gate)


def _attn_kernel(sink_ref, q_ref, kc_ref, kp_ref, kn_ref, vc_ref, vp_ref, vn_ref, km_ref, vm_ref,
                 o_ref, *, tq, seq):
    i = pl.program_id(1)
    nblk = tq // WINDOW
    kall = jnp.concatenate([kp_ref[...], kc_ref[...], kn_ref[...]], axis=0)
    vall = jnp.concatenate([vp_ref[...], vc_ref[...], vn_ref[...]], axis=0)
    kmeta, vmeta = km_ref[...], vm_ref[...]
    nband = 3 * WINDOW
    ncol = nband + LANES
    row = lax.broadcasted_iota(jnp.int32, (WINDOW, ncol), 0)
    col = lax.broadcasted_iota(jnp.int32, (WINDOW, ncol), 1)
    in_band = (jnp.abs(col - WINDOW - row) <= WINDOW) & (col < nband)
    is_meta = (col >= nband) & (col < nband + N_META)
    lane = lax.broadcasted_iota(jnp.int32, (WINDOW, LANES), 1)
    low = lane < HEAD_DIM

    def softmax_pv(s, sink, vband, vmet):
        m = jnp.maximum(jnp.max(s, axis=-1, keepdims=True), sink)
        p = jnp.exp(s - m)
        denom = jnp.sum(p, axis=-1, keepdims=True) + jnp.exp(sink - m)
        pb = p.astype(BF16)
        o = _dot(pb[:, :nband], vband) + _dot(pb[:, nband:], vmet)
        return o * (1.0 / denom)

    for j in range(nblk):
        kidx = i * tq + (j - 1) * WINDOW + col
        ok = (in_band & (kidx >= 0) & (kidx < seq)) | is_meta
        bias = jnp.where(ok, 0.0, NEG_INF).astype(F32)
        kband = kall[j * WINDOW:j * WINDOW + nband]
        vband = vall[j * WINDOW:j * WINDOW + nband]
        for p in range(N_Q_HEADS // 2):
            h = (2 * p) // (N_Q_HEADS // N_KV_HEADS)
            qp = q_ref[j * WINDOW:(j + 1) * WINDOW, p * LANES:(p + 1) * LANES].astype(F32)
            q_lo = jnp.where(low, qp, 0.0).astype(BF16)
            q_hi = jnp.where(low, 0.0, qp).astype(BF16)
            ce = slice(0, KV_WIDTH) if h == 0 else slice(KV_WIDTH, 2 * KV_WIDTH)
            co = slice(KV_WIDTH, 2 * KV_WIDTH) if h == 0 else slice(0, KV_WIDTH)
            s_e = jnp.concatenate([_dot_nt(q_lo, kband[:, ce]), _dot_nt(q_lo, kmeta[:, ce])], axis=1) + bias
            s_o = jnp.concatenate([_dot_nt(q_hi, kband[:, co]), _dot_nt(q_hi, kmeta[:, co])], axis=1) + bias
            o_e = softmax_pv(s_e, sink_ref[2 * p], vband[:, ce], vmeta[:, ce])
            o_o = softmax_pv(s_o, sink_ref[2 * p + 1], vband[:, co], vmeta[:, co])
            o_ref[j * WINDOW:(j + 1) * WINDOW, p * LANES:(p + 1) * LANES] = (
                jnp.where(low, o_e, o_o).astype(BF16))


def _attn(q, kk, vv, kk_meta, vv_meta, sink, *, tq):
    B, S, _ = q.shape
    r = tq // WINDOW
    last = S // WINDOW - 1
    cur = lambda w: pl.BlockSpec((None, tq, w), lambda b, i: (b, i, 0))
    prev = pl.BlockSpec((None, WINDOW, 2 * KV_WIDTH), lambda b, i: (b, jnp.maximum(i * r - 1, 0), 0))
    nxt = pl.BlockSpec((None, WINDOW, 2 * KV_WIDTH), lambda b, i: (b, jnp.minimum((i + 1) * r, last), 0))
    meta = pl.BlockSpec((LANES, 2 * KV_WIDTH), lambda b, i: (0, 0))
    return pl.pallas_call(
        functools.partial(_attn_kernel, tq=tq, seq=S),
        grid=(B, S // tq),
        in_specs=[pl.BlockSpec(memory_space=pltpu.SMEM), cur(ATTN_WIDTH),
                  cur(2 * KV_WIDTH), prev, nxt, cur(2 * KV_WIDTH), prev, nxt, meta, meta],
        out_specs=cur(ATTN_WIDTH),
        out_shape=jax.ShapeDtypeStruct((B, S, ATTN_WIDTH), BF16),
        compiler_params=pltpu.CompilerParams(
            dimension_semantics=("parallel", "parallel"), vmem_limit_bytes=VMEM_LIMIT_BYTES),
        name="attn",
    )(sink, q, kk, kk, kk, vv, vv, vv, kk_meta, vv_meta)


def _factor(L):
    n1 = 16
    while L % (2 * n1) == 0:
        n1 *= 2
    rest = L // n1
    for f in (3, 5, 7, 9):
        while rest % f == 0 and n1 * f <= rest // f * 4:
            n1, rest = n1 * f, rest // f
    return n1, rest


def _round_up(x, m):
    return (x + m - 1) // m * m


def _fmix_consts(S):
    L = S + N_META
    N1, N2 = _factor(L)
    N2p = _round_up(N2, 2 * SUBLANES)
    P = _round_up(2 * N1, SUBLANES)
    if (P // SUBLANES) % 2 == 0:
        P += SUBLANES
    k1 = np.arange(N1, dtype=np.int64)
    ang1 = 2.0 * np.pi * ((k1[:, None] * k1[None, :]) % N1) / N1
    c1, s1 = np.cos(ang1) / np.sqrt(N1), np.sin(ang1) / np.sqrt(N1)
    f1 = np.block([[c1, s1], [-s1, c1]]).astype(np.float32)
    k = k1[:, None, None] + N1 * np.arange(N2p, dtype=np.int64)[None, :, None]
    n2 = np.arange(N2p, dtype=np.int64)[None, None, :]
    ang3 = 2.0 * np.pi * (((n2 + N_META) * k) % L) / L
    valid = ((np.arange(N2p) < N2)[None, :, None] & (np.arange(N2p) < N2)[None, None, :])
    c3 = np.where(valid, np.cos(ang3), 0.0) / np.sqrt(N2)
    s3 = np.where(valid, np.sin(ang3), 0.0) / np.sqrt(N2)
    m3 = np.concatenate([c3, s3], axis=2).astype(np.float32)
    c = np.arange(FOURIER_GROUP_WIDTH, dtype=np.int64)
    angc = 2.0 * np.pi * ((c[:, None] * c[None, :]) % FOURIER_GROUP_WIDTH) / FOURIER_GROUP_WIDTH
    cd = (np.concatenate([np.cos(angc), -np.sin(angc)], axis=1)
          / np.sqrt(FOURIER_GROUP_WIDTH)).astype(np.float32)
    return dict(L=L, N1=N1, N2=N2, N2p=N2p, P=P), f1, m3, cd


def _fmix_kernel(u_ref, um_ref, cd_ref, f1_ref, m3_ref, o_ref, xr, xi, a_scr, ys,
                 *, S, L, N1, N2, N2p, P, chunk):
    W = FOURIER_GROUP_WIDTH
    cd = cd_ref[...]

    def p0(c, carry):
        r0 = pl.multiple_of(c * chunk, chunk)
        v = _dot(u_ref[pl.ds(r0, chunk), :], cd)
        xr[pl.ds(r0, chunk), :] = v[:, :W]
        xi[pl.ds(r0, chunk), :] = v[:, W:]
        return carry

    lax.fori_loop(0, S // chunk, p0, 0)
    vm = _dot(um_ref[...], cd)
    xr[S:L, :] = vm[:, :W]
    xi[S:L, :] = vm[:, W:]
    if N2p > N2:
        a_scr[N2 * P:N2p * P, :] = jnp.zeros(((N2p - N2) * P, W), F32)

    def p1(n2, carry):
        t = jnp.concatenate([xr[pl.ds(n2, N1, stride=N2), :], xi[pl.ds(n2, N1, stride=N2), :]], axis=0)
        a = _dot(f1_ref[...], t.astype(BF16))
        a_scr[pl.ds(pl.multiple_of(n2 * P, SUBLANES), 2 * N1), :] = a
        return carry

    lax.fori_loop(0, N2, p1, 0)

    def p2(k1, carry):
        b = jnp.concatenate([a_scr[pl.ds(k1, N2p, stride=P), :],
                             a_scr[pl.ds(N1 + k1, N2p, stride=P), :]], axis=0)
        z = _dot(m3_ref[k1], b.astype(BF16))
        ys[pl.ds(k1, N2p, stride=N1), :] = z
        return carry

    lax.fori_loop(0, N1, p2, 0)

    def p3(c, carry):
        r0 = pl.multiple_of(c * chunk, chunk)
        o_ref[pl.ds(r0, chunk), :] = ys[pl.ds(pl.multiple_of(N_META + r0, SUBLANES), chunk), :].astype(BF16)
        return carry

    lax.fori_loop(0, S // chunk, p3, 0)


def _fmix(u, u_meta, *, chunk=512):
    B, S, _ = u.shape
    dims, f1, m3, cd = _fmix_consts(S)
    L, N1, N2p, P = dims["L"], dims["N1"], dims["N2p"], dims["P"]
    W = FOURIER_GROUP_WIDTH
    const = lambda shape: pl.BlockSpec(shape, lambda b, g: (0,) * len(shape))
    return pl.pallas_call(
        functools.partial(_fmix_kernel, S=S, chunk=chunk, **dims),
        grid=(B, N_FOURIER_GROUPS),
        in_specs=[pl.BlockSpec((None, S, W), lambda b, g: (b, 0, g)),
                  pl.BlockSpec((N_META, W), lambda b, g: (0, g)),
                  const(cd.shape), const(f1.shape), const(m3.shape)],
        out_specs=pl.BlockSpec((None, S, W), lambda b, g: (b, 0, g)),
        out_shape=jax.ShapeDtypeStruct((B, S, FOURIER_WIDTH), BF16),
        scratch_shapes=[pltpu.VMEM((L, W), F32), pltpu.VMEM((L, W), F32),
                        pltpu.VMEM((N2p * P, W), F32), pltpu.VMEM((N1 * N2p, W), F32)],
        compiler_params=pltpu.CompilerParams(
            dimension_semantics=("parallel", "parallel"), vmem_limit_bytes=VMEM_LIMIT_BYTES),
        name="fmix",
    )(u, u_meta, jnp.asarray(cd).astype(BF16), jnp.asarray(f1).astype(BF16), jnp.asarray(m3).astype(BF16))


def _post_kernel(x_ref, a_ref, f_ref, gate_ref, wao_ref, wf_ref, wout_ref, g2_ref, wup_ref, wdown_ref,
                 gfin_ref, y_ref, *, ff_chunk):
    a = _dot(a_ref[...], wao_ref[...])
    f = _dot(f_ref[...], wf_ref[...])
    merged = (gate_ref[:, :D_MODEL].astype(F32) * a + gate_ref[:, D_MODEL:].astype(F32) * f).astype(BF16)
    h = x_ref[...] + _dot(merged, wout_ref[...])
    n = _rms(h, g2_ref[...]).astype(BF16)
    for c in range(D_FF // ff_chunk):
        t = _dot(n, wup_ref[:, c * ff_chunk:(c + 1) * ff_chunk])
        r = jnp.square(jnp.maximum(t, 0.0)).astype(BF16)
        h = h + _dot(r, wdown_ref[c * ff_chunk:(c + 1) * ff_chunk, :])
    y_ref[...] = _rms(h, gfin_ref[...])


def _post(x, a, f, gates, w_ao, w_f, w_out, g2, w_up, w_down, g_fin, *, tm, ff_chunk=1024):
    B, S, _ = x.shape
    tok = lambda w: pl.BlockSpec((None, tm, w), lambda b, i: (b, i, 0))
    const = lambda shape: pl.BlockSpec(shape, lambda b, i: (0,) * len(shape), pipeline_mode=pl.Buffered(1))
    return pl.pallas_call(
        functools.partial(_post_kernel, ff_chunk=ff_chunk),
        grid=(B, S // tm),
        in_specs=[tok(D_MODEL), tok(ATTN_WIDTH), tok(FOURIER_WIDTH), tok(N_BRANCHES * D_MODEL),
                  const(w_ao.shape), const(w_f.shape), const(w_out.shape), const((1, D_MODEL)),
                  const(w_up.shape), const(w_down.shape), const((1, D_MODEL))],
        out_specs=tok(D_MODEL),
        out_shape=jax.ShapeDtypeStruct((B, S, D_MODEL), F32),
        compiler_params=pltpu.CompilerParams(
            dimension_semantics=("parallel", "parallel"), vmem_limit_bytes=VMEM_LIMIT_BYTES),
        name="post",
    )(x, a, f, gates, w_ao, w_f, w_out, g2, w_up, w_down, g_fin)


def _rope_tables(n_pos):
    half = ROT_DIM // 2
    inv_freq = ROPE_THETA ** (-jnp.arange(half, dtype=F32) / half)
    pos = jnp.arange(n_pos, dtype=F32)
    ang = pos[:, None] * inv_freq[None, :]
    cos, sin = jnp.cos(ang), jnp.sin(ang)
    d = np.arange(LANES) % HEAD_DIM
    idx = jnp.asarray(d % half)
    cos_l, sin_l = cos[:, idx], sin[:, idx]
    first = jnp.asarray(d < half)[None, :]
    second = jnp.asarray((d >= half) & (d < ROT_DIM))[None, :]
    cos_t = jnp.where(first | second, cos_l, 1.0)
    sin_a = jnp.where(second, sin_l, 0.0)
    sin_b = jnp.where(first, -sin_l, 0.0)
    return cos_t, sin_a, sin_b


def _trunk(x, meta_parts, tables, wts, *, tm, tq):
    kk_m, vv_m, u_m = meta_parts
    S = x.shape[1]
    tabs = tuple(t[N_META:N_META + S] for t in tables)
    q, kk, vv, u, gates = _proj(x, tabs, wts["norm_mix_g"], wts["w_in"], wts["b_gate"], tm=tm)
    a = _attn(q, kk, vv, kk_m, vv_m, wts["attn_sink"], tq=tq)
    f = _fmix(u, u_m)
    return _post(x, a, f, gates, wts["w_attn_out"], wts["w_fourier"], wts["w_out"], wts["norm_mlp_g"],
                 wts["w_mlp_up"], wts["w_mlp_down"], wts["norm_final_g"], tm=tm)


def kernel(x_prompt, x_sample, meta_tokens, norm_mix_g, w_in, b_gate, attn_sink, w_attn_out, w_fourier, w_out,
           norm_mlp_g, w_mlp_up, w_mlp_down, norm_final_g):
    assert w_in.shape[0] == 1, "single-layer trunk: meta-token outputs are never consumed"
    wts = dict(
        norm_mix_g=norm_mix_g[0][None, :], w_in=w_in[0].astype(BF16), b_gate=b_gate[0][None, :],
        attn_sink=attn_sink[0], w_attn_out=w_attn_out[0].astype(BF16), w_fourier=w_fourier[0].astype(BF16),
        w_out=w_out[0].astype(BF16), norm_mlp_g=norm_mlp_g[0][None, :], w_mlp_up=w_mlp_up[0].astype(BF16),
        w_mlp_down=w_mlp_down[0].astype(BF16), norm_final_g=norm_final_g[None, :])
    n_pos = N_META + max(x_prompt.shape[1], x_sample.shape[1])
    tables = _rope_tables(n_pos)
    _, kk_m, vv_m, u_m, _ = _proj(meta_tokens[None], tuple(t[:N_META] for t in tables),
                                  wts["norm_mix_g"], wts["w_in"], wts["b_gate"], tm=N_META)
    pad = ((0, LANES - N_META), (0, 0))
    meta_parts = (jnp.pad(kk_m[0], pad), jnp.pad(vv_m[0], pad), u_m[0])
    y_prompt = _trunk(x_prompt, meta_parts, tables, wts, tm=512, tq=256)
    y_sample = _trunk(x_sample, meta_parts, tables, wts, tm=512, tq=256)
    return (y_prompt, y_sample)
```

```python
import functools

import numpy as np
import jax
import jax.numpy as jnp
from jax import lax
from jax.experimental import pallas as pl
from jax.experimental.pallas import tpu as pltpu

D_MODEL = 1024
HEAD_DIM = 64
N_Q_HEADS = 8
N_KV_HEADS = 2
ATTN_WIDTH = N_Q_HEADS * HEAD_DIM
KV_WIDTH = N_KV_HEADS * HEAD_DIM
WINDOW = 128
ROPE_THETA = 500000.0
ROT_DIM = HEAD_DIM // 4
N_FOURIER_GROUPS = 4
FOURIER_GROUP_WIDTH = 128
FOURIER_WIDTH = N_FOURIER_GROUPS * FOURIER_GROUP_WIDTH
N_BRANCHES = 2
IN_WIDTH = ATTN_WIDTH + 2 * KV_WIDTH + FOURIER_WIDTH + N_BRANCHES * D_MODEL
D_FF = 4 * D_MODEL
N_META = 16
RMS_EPS = 1e-6
NEG_INF = -1e30

LANES = 128
SUBLANES = 8
VMEM_LIMIT_BYTES = 56 * 1024 * 1024

_Q0, _K0, _V0, _U0, _G0 = 0, ATTN_WIDTH, ATTN_WIDTH + KV_WIDTH, ATTN_WIDTH + 2 * KV_WIDTH, \
    ATTN_WIDTH + 2 * KV_WIDTH + FOURIER_WIDTH

BF16 = jnp.bfloat16
F32 = jnp.float32


def _dot(a, b):
    return jnp.dot(a, b, preferred_element_type=F32)


def _dot_nt(a, b):
    return lax.dot_general(a, b, (((1,), (1,)), ((), ())), preferred_element_type=F32)


def _rms(x, g):
    return x * lax.rsqrt(jnp.mean(x * x, axis=-1, keepdims=True) + RMS_EPS) * g


def _proj_kernel(x_ref, cos_ref, sa_ref, sb_ref, g_ref, w_ref, b_ref,
                 q_ref, kk_ref, vv_ref, u_ref, gate_ref):
    n = _rms(x_ref[...], g_ref[...]).astype(BF16)
    cos_t, sin_a, sin_b = cos_ref[...], sa_ref[...], sb_ref[...]

    def rope(z):
        return (z * cos_t + pltpu.roll(z, ROT_DIM // 2, 1) * sin_a
                + pltpu.roll(z, LANES - ROT_DIM // 2, 1) * sin_b)

    zq = _dot(n, w_ref[:, _Q0:_K0])
    for c in range(ATTN_WIDTH // LANES):
        q_ref[:, c * LANES:(c + 1) * LANES] = (
            rope(zq[:, c * LANES:(c + 1) * LANES]) * (HEAD_DIM ** -0.5)).astype(BF16)
    zkv = _dot(n, w_ref[:, _K0:_U0])
    zk = rope(zkv[:, :KV_WIDTH])
    zv = zkv[:, KV_WIDTH:]
    kk_ref[:, :KV_WIDTH] = zk.astype(BF16)
    kk_ref[:, KV_WIDTH:] = pltpu.roll(zk, HEAD_DIM, 1).astype(BF16)
    vv_ref[:, :KV_WIDTH] = zv.astype(BF16)
    vv_ref[:, KV_WIDTH:] = pltpu.roll(zv, HEAD_DIM, 1).astype(BF16)
    u_ref[...] = _dot(n, w_ref[:, _U0:_G0]).astype(BF16)
    for c in range(N_BRANCHES):
        lo = _G0 + c * D_MODEL
        g = _dot(n, w_ref[:, lo:lo + D_MODEL]) + b_ref[:, c * D_MODEL:(c + 1) * D_MODEL]
        gate_ref[:, c * D_MODEL:(c + 1) * D_MODEL] = jax.nn.sigmoid(g).astype(BF16)


def _proj(x, tables, g, w_in, b_gate, *, tm):
    B, S, _ = x.shape
    cos_t, sin_a, sin_b = tables
    tok = lambda w: pl.BlockSpec((None, tm, w), lambda b, i: (b, i, 0))
    tab = pl.BlockSpec((tm, LANES), lambda b, i: (i, 0))
    const = lambda shape: pl.BlockSpec(shape, lambda b, i: (0,) * len(shape))
    out_w = (ATTN_WIDTH, 2 * KV_WIDTH, 2 * KV_WIDTH, FOURIER_WIDTH, N_BRANCHES * D_MODEL)
    return pl.pallas_call(
        _proj_kernel,
        grid=(B, S // tm),
        in_specs=[tok(D_MODEL), tab, tab, tab, const((1, D_MODEL)),
                  const((D_MODEL, IN_WIDTH)), const((1, N_BRANCHES * D_MODEL))],
        out_specs=[tok(w) for w in out_w],
        out_shape=[jax.ShapeDtypeStruct((B, S, w), BF16) for w in out_w],
        compiler_params=pltpu.CompilerParams(
            dimension_semantics=("parallel", "parallel"), vmem_limit_bytes=VMEM_LIMIT_BYTES),
        name="proj",
    )(x, cos_t, sin_a, sin_b, g, w_in, b_gate)


def _attn_kernel(sink_ref, q_ref, kc_ref, kp_ref, kn_ref, vc_ref, vp_ref, vn_ref, km_ref, vm_ref,
                 o_ref, *, tq, seq):
    i = pl.program_id(1)
    nblk = tq // WINDOW
    kall = jnp.concatenate([kp_ref[...], kc_ref[...], kn_ref[...]], axis=0)
    vall = jnp.concatenate([vp_ref[...], vc_ref[...], vn_ref[...]], axis=0)
    kmeta, vmeta = km_ref[...], vm_ref[...]
    nband = 3 * WINDOW
    ncol = nband + LANES
    row = lax.broadcasted_iota(jnp.int32, (WINDOW, ncol), 0)
    col = lax.broadcasted_iota(jnp.int32, (WINDOW, ncol), 1)
    in_band = (jnp.abs(col - WINDOW - row) <= WINDOW) & (col < nband)
    is_meta = (col >= nband) & (col < nband + N_META)
    lane = lax.broadcasted_iota(jnp.int32, (WINDOW, LANES), 1)
    low = lane < HEAD_DIM

    def softmax_pv(s, sink, vband, vmet):
        m = jnp.maximum(jnp.max(s, axis=-1, keepdims=True), sink)
        p = jnp.exp(s - m)
        denom = jnp.sum(p, axis=-1, keepdims=True) + jnp.exp(sink - m)
        pb = p.astype(BF16)
        o = _dot(pb[:, :nband], vband) + _dot(pb[:, nband:], vmet)
        return o * (1.0 / denom)

    for j in range(nblk):
        kidx = i * tq + (j - 1) * WINDOW + col
        ok = (in_band & (kidx >= 0) & (kidx < seq)) | is_meta
        bias = jnp.where(ok, 0.0, NEG_INF).astype(F32)
        kband = kall[j * WINDOW:j * WINDOW + nband]
        vband = vall[j * WINDOW:j * WINDOW + nband]
        for p in range(N_Q_HEADS // 2):
            h = (2 * p) // (N_Q_HEADS // N_KV_HEADS)
            qp = q_ref[j * WINDOW:(j + 1) * WINDOW, p * LANES:(p + 1) * LANES].astype(F32)
            q_lo = jnp.where(low, qp, 0.0).astype(BF16)
            q_hi = jnp.where(low, 0.0, qp).astype(BF16)
            ce = slice(0, KV_WIDTH) if h == 0 else slice(KV_WIDTH, 2 * KV_WIDTH)
            co = slice(KV_WIDTH, 2 * KV_WIDTH) if h == 0 else slice(0, KV_WIDTH)
            s_e = jnp.concatenate([_dot_nt(q_lo, kband[:, ce]), _dot_nt(q_lo, kmeta[:, ce])], axis=1) + bias
            s_o = jnp.concatenate([_dot_nt(q_hi, kband[:, co]), _dot_nt(q_hi, kmeta[:, co])], axis=1) + bias
            o_e = softmax_pv(s_e, sink_ref[2 * p], vband[:, ce], vmeta[:, ce])
            o_o = softmax_pv(s_o, sink_ref[2 * p + 1], vband[:, co], vmeta[:, co])
            o_ref[j * WINDOW:(j + 1) * WINDOW, p * LANES:(p + 1) * LANES] = (
                jnp.where(low, o_e, o_o).astype(BF16))


def _attn(q, kk, vv, kk_meta, vv_meta, sink, *, tq):
    B, S, _ = q.shape
    r = tq // WINDOW
    last = S // WINDOW - 1
    cur = lambda w: pl.BlockSpec((None, tq, w), lambda b, i: (b, i, 0))
    prev = pl.BlockSpec((None, WINDOW, 2 * KV_WIDTH), lambda b, i: (b, jnp.maximum(i * r - 1, 0), 0))
    nxt = pl.BlockSpec((None, WINDOW, 2 * KV_WIDTH), lambda b, i: (b, jnp.minimum((i + 1) * r, last), 0))
    meta = pl.BlockSpec((LANES, 2 * KV_WIDTH), lambda b, i: (0, 0))
    return pl.pallas_call(
        functools.partial(_attn_kernel, tq=tq, seq=S),
        grid=(B, S // tq),
        in_specs=[pl.BlockSpec(memory_space=pltpu.SMEM), cur(ATTN_WIDTH),
                  cur(2 * KV_WIDTH), prev, nxt, cur(2 * KV_WIDTH), prev, nxt, meta, meta],
        out_specs=cur(ATTN_WIDTH),
        out_shape=jax.ShapeDtypeStruct((B, S, ATTN_WIDTH), BF16),
        compiler_params=pltpu.CompilerParams(
            dimension_semantics=("parallel", "parallel"), vmem_limit_bytes=VMEM_LIMIT_BYTES),
        name="attn",
    )(sink, q, kk, kk, kk, vv, vv, vv, kk_meta, vv_meta)


def _factor(L):
    n1 = 16
    while L % (2 * n1) == 0:
        n1 *= 2
    rest = L // n1
    for f in (3, 5, 7, 9):
        while rest % f == 0 and n1 * f <= rest // f * 4:
            n1, rest = n1 * f, rest // f
    return n1, rest


def _round_up(x, m):
    return (x + m - 1) // m * m


def _odd_tiles(rows):
    p = _round_up(rows, SUBLANES)
    return p if (p // SUBLANES) % 2 else p + SUBLANES


def _fmix_consts(S):
    L = S + N_META
    N1, N2 = _factor(L)
    N2p = _round_up(N2, 2 * SUBLANES)
    dims = dict(L=L, N1=N1, N2=N2, N2p=N2p, P=_odd_tiles(2 * N1), Q=_odd_tiles(N2p),
                Lp=_round_up(L + N2p - N2, SUBLANES))
    k1 = np.arange(N1, dtype=np.int64)
    ang1 = 2.0 * np.pi * ((k1[:, None] * k1[None, :]) % N1) / N1
    c1, s1 = np.cos(ang1) / np.sqrt(N1), np.sin(ang1) / np.sqrt(N1)
    f1 = np.block([[c1, s1], [-s1, c1]]).astype(np.float32)
    k = k1[:, None, None] + N1 * np.arange(N2p, dtype=np.int64)[None, :, None]
    n2 = np.arange(N2p, dtype=np.int64)[None, None, :]
    ang3 = 2.0 * np.pi * (((n2 + N_META) * k) % L) / L
    valid = ((np.arange(N2p) < N2)[None, :, None] & (np.arange(N2p) < N2)[None, None, :])
    c3 = np.where(valid, np.cos(ang3), 0.0) / np.sqrt(N2)
    s3 = np.where(valid, np.sin(ang3), 0.0) / np.sqrt(N2)
    m3 = np.concatenate([c3, s3], axis=2).astype(np.float32)
    c = np.arange(FOURIER_GROUP_WIDTH, dtype=np.int64)
    angc = 2.0 * np.pi * ((c[:, None] * c[None, :]) % FOURIER_GROUP_WIDTH) / FOURIER_GROUP_WIDTH
    cd = (np.concatenate([np.cos(angc), -np.sin(angc)], axis=1)
          / np.sqrt(FOURIER_GROUP_WIDTH)).astype(np.float32)
    return dims, f1, m3, cd


def _fmix_kernel(u_ref, um_ref, cd_ref, f1_ref, m3_ref, o_ref, *scratch,
                 S, L, Lp, N1, N2, N2p, P, Q, G, NB, U1, U2, chunk):
    W = FOURIER_GROUP_WIDTH
    xr, xi, a_s, y_s = (scratch[i * G:(i + 1) * G] for i in range(4))
    cd = cd_ref[...]
    lanes = lambda g: slice(g * W, (g + 1) * W)

    def p0(c, carry):
        r0 = pl.multiple_of(c * chunk, chunk)
        for g in range(G):
            v = _dot(u_ref[pl.ds(r0, chunk), lanes(g)], cd)
            xr[g][pl.ds(r0, chunk), :] = v[:, :W]
            xi[g][pl.ds(r0, chunk), :] = v[:, W:]
        return carry

    lax.fori_loop(0, S // chunk, p0, 0)
    for g in range(G):
        vm = _dot(um_ref[:, lanes(g)], cd)
        xr[g][S:L, :] = vm[:, :W]
        xi[g][S:L, :] = vm[:, W:]
        xr[g][L:Lp, :] = jnp.zeros((Lp - L, W), F32)
        xi[g][L:Lp, :] = jnp.zeros((Lp - L, W), F32)

    def p1(t, carry):
        for uu in range(U1):
            n2s = [(t * U1 + uu) * NB + j for j in range(NB)]
            cols = [jnp.concatenate([xr[g][pl.ds(n2, N1, stride=N2), :],
                                     xi[g][pl.ds(n2, N1, stride=N2), :]], axis=0)
                    for n2 in n2s for g in range(G)]
            a = _dot(f1_ref[...], jnp.concatenate(cols, axis=1).astype(BF16))
            for j, n2 in enumerate(n2s):
                for g in range(G):
                    a_s[g][pl.ds(pl.multiple_of(n2 * P, SUBLANES), 2 * N1), :] = a[:, lanes(j * G + g)]
        return carry

    lax.fori_loop(0, N2p // (NB * U1), p1, 0)

    def p2(t, carry):
        for uu in range(U2):
            k1 = t * U2 + uu
            b = jnp.concatenate(
                [jnp.concatenate([a_s[g][pl.ds(k1, N2p, stride=P), :],
                                  a_s[g][pl.ds(N1 + k1, N2p, stride=P), :]], axis=0) for g in range(G)], axis=1)
            z = _dot(m3_ref[k1], b.astype(BF16))
            for g in range(G):
                y_s[g][pl.ds(pl.multiple_of(k1 * Q, SUBLANES), N2p), :] = z[:, lanes(g)]
        return carry

    lax.fori_loop(0, N1 // U2, p2, 0)

    for g in range(G):
        o_ref[0:N1 - N_META, lanes(g)] = y_s[g][pl.ds(0, N1, stride=Q), :][N_META:].astype(BF16)

    def p3(k2, carry):
        r0 = pl.multiple_of(k2 * N1 - N_META, 2 * SUBLANES)
        for g in range(G):
            o_ref[pl.ds(r0, N1), lanes(g)] = y_s[g][pl.ds(k2, N1, stride=Q), :].astype(BF16)
        return carry

    lax.fori_loop(1, N2, p3, 0)


def _fmix(u, u_meta, *, groups, u1, u2, chunk=512):
    B, S, _ = u.shape
    dims, f1, m3, cd = _fmix_consts(S)
    Lp, N1, N2p, P, Q = dims["Lp"], dims["N1"], dims["N2p"], dims["P"], dims["Q"]
    W, G = FOURIER_GROUP_WIDTH, groups
    nb = (2 * LANES * 2) // (G * W)
    const = lambda shape: pl.BlockSpec(shape, lambda b, g: (0,) * len(shape))
    scratch = ([pltpu.VMEM((Lp, W), F32)] * (2 * G) + [pltpu.VMEM((N2p * P, W), F32)] * G
               + [pltpu.VMEM((N1 * Q, W), F32)] * G)
    return pl.pallas_call(
        functools.partial(_fmix_kernel, S=S, chunk=chunk, G=G, NB=nb, U1=u1, U2=u2, **dims),
        grid=(B, N_FOURIER_GROUPS // G),
        in_specs=[pl.BlockSpec((None, S, G * W), lambda b, g: (b, 0, g)),
                  pl.BlockSpec((N_META, G * W), lambda b, g: (0, g)),
                  const(cd.shape), const(f1.shape), const(m3.shape)],
        out_specs=pl.BlockSpec((None, S, G * W), lambda b, g: (b, 0, g)),
        out_shape=jax.ShapeDtypeStruct((B, S, FOURIER_WIDTH), BF16),
        scratch_shapes=scratch,
        compiler_params=pltpu.CompilerParams(
            dimension_semantics=("parallel", "parallel"), vmem_limit_bytes=VMEM_LIMIT_BYTES),
        name="fmix",
    )(u, u_meta, jnp.asarray(cd).astype(BF16), jnp.asarray(f1).astype(BF16), jnp.asarray(m3).astype(BF16))


def _post_kernel(x_ref, a_ref, f_ref, gate_ref, wao_ref, wf_ref, wout_ref, g2_ref, wup_ref, wdown_ref,
                 gfin_ref, y_ref, *, ff_chunk):
    a = _dot(a_ref[...], wao_ref[...])
    f = _dot(f_ref[...], wf_ref[...])
    merged = (gate_ref[:, :D_MODEL].astype(F32) * a + gate_ref[:, D_MODEL:].astype(F32) * f).astype(BF16)
    h = x_ref[...] + _dot(merged, wout_ref[...])
    n = _rms(h, g2_ref[...]).astype(BF16)
    for c in range(D_FF // ff_chunk):
        t = _dot(n, wup_ref[:, c * ff_chunk:(c + 1) * ff_chunk])
        r = jnp.square(jnp.maximum(t, 0.0)).astype(BF16)
        h = h + _dot(r, wdown_ref[c * ff_chunk:(c + 1) * ff_chunk, :])
    y_ref[...] = _rms(h, gfin_ref[...])


def _post(x, a, f, gates, w_ao, w_f, w_out, g2, w_up, w_down, g_fin, *, tm, ff_chunk=1024):
    B, S, _ = x.shape
    tok = lambda w: pl.BlockSpec((None, tm, w), lambda b, i: (b, i, 0))
    const = lambda shape: pl.BlockSpec(shape, lambda b, i: (0,) * len(shape), pipeline_mode=pl.Buffered(1))
    return pl.pallas_call(
        functools.partial(_post_kernel, ff_chunk=ff_chunk),
        grid=(B, S // tm),
        in_specs=[tok(D_MODEL), tok(ATTN_WIDTH), tok(FOURIER_WIDTH), tok(N_BRANCHES * D_MODEL),
                  const(w_ao.shape), const(w_f.shape), const(w_out.shape), const((1, D_MODEL)),
                  const(w_up.shape), const(w_down.shape), const((1, D_MODEL))],
        out_specs=tok(D_MODEL),
        out_shape=jax.ShapeDtypeStruct((B, S, D_MODEL), F32),
        compiler_params=pltpu.CompilerParams(
            dimension_semantics=("parallel", "parallel"), vmem_limit_bytes=VMEM_LIMIT_BYTES),
        name="post",
    )(x, a, f, gates, w_ao, w_f, w_out, g2, w_up, w_down, g_fin)


def _rope_tables(n_pos):
    half = ROT_DIM // 2
    inv_freq = ROPE_THETA ** (-jnp.arange(half, dtype=F32) / half)
    pos = jnp.arange(n_pos, dtype=F32)
    ang = pos[:, None] * inv_freq[None, :]
    cos, sin = jnp.cos(ang), jnp.sin(ang)
    d = np.arange(LANES) % HEAD_DIM
    idx = jnp.asarray(d % half)
    cos_l, sin_l = cos[:, idx], sin[:, idx]
    first = jnp.asarray(d < half)[None, :]
    second = jnp.asarray((d >= half) & (d < ROT_DIM))[None, :]
    cos_t = jnp.where(first | second, cos_l, 1.0)
    sin_a = jnp.where(second, sin_l, 0.0)
    sin_b = jnp.where(first, -sin_l, 0.0)
    return cos_t, sin_a, sin_b


def _trunk(x, meta_parts, tables, wts, *, tm, tq, fmix):
    kk_m, vv_m, u_m = meta_parts
    S = x.shape[1]
    tabs = tuple(t[N_META:N_META + S] for t in tables)
    q, kk, vv, u, gates = _proj(x, tabs, wts["norm_mix_g"], wts["w_in"], wts["b_gate"], tm=tm)
    a = _attn(q, kk, vv, kk_m, vv_m, wts["attn_sink"], tq=tq)
    f = _fmix(u, u_m, **fmix)
    return _post(x, a, f, gates, wts["w_attn_out"], wts["w_fourier"], wts["w_out"], wts["norm_mlp_g"],
                 wts["w_mlp_up"], wts["w_mlp_down"], wts["norm_final_g"], tm=tm)


def kernel(x_prompt, x_sample, meta_tokens, norm_mix_g, w_in, b_gate, attn_sink, w_attn_out, w_fourier, w_out,
           norm_mlp_g, w_mlp_up, w_mlp_down, norm_final_g):
    assert w_in.shape[0] == 1, "single-layer trunk: meta-token outputs are never consumed"
    wts = dict(
        norm_mix_g=norm_mix_g[0][None, :], w_in=w_in[0].astype(BF16), b_gate=b_gate[0][None, :],
        attn_sink=attn_sink[0], w_attn_out=w_attn_out[0].astype(BF16), w_fourier=w_fourier[0].astype(BF16),
        w_out=w_out[0].astype(BF16), norm_mlp_g=norm_mlp_g[0][None, :], w_mlp_up=w_mlp_up[0].astype(BF16),
        w_mlp_down=w_mlp_down[0].astype(BF16), norm_final_g=norm_final_g[None, :])
    n_pos = N_META + max(x_prompt.shape[1], x_sample.shape[1])
    tables = _rope_tables(n_pos)
    _, kk_m, vv_m, u_m, _ = _proj(meta_tokens[None], tuple(t[:N_META] for t in tables),
                                  wts["norm_mix_g"], wts["w_in"], wts["b_gate"], tm=N_META)
    pad = ((0, LANES - N_META), (0, 0))
    meta_parts = (jnp.pad(kk_m[0], pad), jnp.pad(vv_m[0], pad), u_m[0])
    y_prompt = _trunk(x_prompt, meta_parts, tables, wts, tm=512, tq=256, fmix=dict(groups=4, u1=4, u2=12))
    y_sample = _trunk(x_sample, meta_parts, tables, wts, tm=512, tq=256, fmix=dict(groups=1, u1=4, u2=8))
    return (y_prompt, y_sample)
```

```python
import functools

import numpy as np
import jax
import jax.numpy as jnp
from jax import lax
from jax.experimental import pallas as pl
from jax.experimental.pallas import tpu as pltpu

D_MODEL = 1024
HEAD_DIM = 64
N_Q_HEADS = 8
N_KV_HEADS = 2
ATTN_WIDTH = N_Q_HEADS * HEAD_DIM
KV_WIDTH = N_KV_HEADS * HEAD_DIM
WINDOW = 128
ROPE_THETA = 500000.0
ROT_DIM = HEAD_DIM // 4
N_FOURIER_GROUPS = 4
FOURIER_GROUP_WIDTH = 128
FOURIER_WIDTH = N_FOURIER_GROUPS * FOURIER_GROUP_WIDTH
N_BRANCHES = 2
IN_WIDTH = ATTN_WIDTH + 2 * KV_WIDTH + FOURIER_WIDTH + N_BRANCHES * D_MODEL
D_FF = 4 * D_MODEL
N_META = 16
RMS_EPS = 1e-6
NEG_INF = -1e30
LOG2E = 1.4426950408889634

LANES = 128
SUBLANES = 8
VMEM_LIMIT_BYTES = 56 * 1024 * 1024

_Q0, _K0, _V0, _U0, _G0 = 0, ATTN_WIDTH, ATTN_WIDTH + KV_WIDTH, ATTN_WIDTH + 2 * KV_WIDTH, \
    ATTN_WIDTH + 2 * KV_WIDTH + FOURIER_WIDTH

BF16 = jnp.bfloat16
F32 = jnp.float32


def _dot(a, b):
    return jnp.dot(a, b, preferred_element_type=F32)


def _dot_nt(a, b):
    return lax.dot_general(a, b, (((1,), (1,)), ((), ())), preferred_element_type=F32)


def _rms(x, g):
    return x * lax.rsqrt(jnp.mean(x * x, axis=-1, keepdims=True) + RMS_EPS) * g


def _proj_kernel(x_ref, cos_ref, sa_ref, sb_ref, g_ref, w_ref, b_ref,
                 q_ref, k4_ref, v_ref, u_ref, gate_ref, *, transpose_v):
    n = _rms(x_ref[...], g_ref[...]).astype(BF16)
    cos_t, sin_a, sin_b = cos_ref[...], sa_ref[...], sb_ref[...]

    def rope(z):
        return (z * cos_t + pltpu.roll(z, ROT_DIM // 2, 1) * sin_a
                + pltpu.roll(z, LANES - ROT_DIM // 2, 1) * sin_b)

    zq = _dot(n, w_ref[:, _Q0:_K0])
    for c in range(ATTN_WIDTH // LANES):
        q_ref[:, c * LANES:(c + 1) * LANES] = (
            rope(zq[:, c * LANES:(c + 1) * LANES]) * (LOG2E * HEAD_DIM ** -0.5)).astype(BF16)
    zkv = _dot(n, w_ref[:, _K0:_U0])
    zk = rope(zkv[:, :KV_WIDTH])
    zv = zkv[:, KV_WIDTH:]
    zks = pltpu.roll(zk, HEAD_DIM, 1)
    low = lax.broadcasted_iota(jnp.int32, zk.shape, 1) < HEAD_DIM
    for s, (keep_low, src) in enumerate(((True, zk), (False, zks), (True, zks), (False, zk))):
        k4_ref[:, s * LANES:(s + 1) * LANES] = jnp.where(low == keep_low, src, 0.0).astype(BF16)
    v_ref[...] = (zv.T if transpose_v else zv).astype(BF16)
    u_ref[...] = _dot(n, w_ref[:, _U0:_G0]).astype(BF16)
    for c in range(N_BRANCHES):
        lo = _G0 + c * D_MODEL
        g = _dot(n, w_ref[:, lo:lo + D_MODEL]) + b_ref[:, c * D_MODEL:(c + 1) * D_MODEL]
        gate_ref[:, c * D_MODEL:(c + 1) * D_MODEL] = jax.nn.sigmoid(g).astype(BF16)


def _proj(x, tables, g, w_in, b_gate, *, tm, transpose_v):
    B, S, _ = x.shape
    cos_t, sin_a, sin_b = tables
    tok = lambda w: pl.BlockSpec((None, tm, w), lambda b, i: (b, i, 0))
    tab = pl.BlockSpec((tm, LANES), lambda b, i: (i, 0))
    const = lambda shape: pl.BlockSpec(shape, lambda b, i: (0,) * len(shape))
    out_w = (ATTN_WIDTH, 4 * LANES, KV_WIDTH, FOURIER_WIDTH, N_BRANCHES * D_MODEL)
    out_specs = [tok(w) for w in out_w]
    out_shape = [jax.ShapeDtypeStruct((B, S, w), BF16) for w in out_w]
    if transpose_v:
        out_specs[2] = pl.BlockSpec((None, KV_WIDTH, tm), lambda b, i: (b, 0, i))
        out_shape[2] = jax.ShapeDtypeStruct((B, KV_WIDTH, S), BF16)
    return pl.pallas_call(
        functools.partial(_proj_kernel, transpose_v=transpose_v),
        grid=(B, S // tm),
        in_specs=[tok(D_MODEL), tab, tab, tab, const((1, D_MODEL)),
                  const((D_MODEL, IN_WIDTH)), const((1, N_BRANCHES * D_MODEL))],
        out_specs=out_specs,
        out_shape=out_shape,
        compiler_params=pltpu.CompilerParams(
            dimension_semantics=("parallel", "parallel"), vmem_limit_bytes=VMEM_LIMIT_BYTES),
        name="proj",
    )(x, cos_t, sin_a, sin_b, g, w_in, b_gate)


def _attn_kernel(sink_ref, q_ref, kc_ref, kp_ref, kn_ref, vc_ref, vp_ref, vn_ref, km_ref, vm_ref,
                 o_ref, *, tq, seq):
    i = pl.program_id(1)
    nblk = tq // WINDOW
    nband = 3 * WINDOW
    nkey = nband + N_META
    kall = jnp.concatenate([kp_ref[...], kc_ref[...], kn_ref[...]], axis=0)
    vall = jnp.concatenate([vp_ref[...], vc_ref[...], vn_ref[...]], axis=1)
    kmeta, vmeta = km_ref[...], vm_ref[...]
    key = lax.broadcasted_iota(jnp.int32, (WINDOW, 2 * LANES), 0)
    qry = jnp.bitwise_and(lax.broadcasted_iota(jnp.int32, (WINDOW, 2 * LANES), 1), WINDOW - 1)
    band_prev = jnp.where(key >= qry, 0.0, NEG_INF).astype(F32)
    band_next = jnp.where(key <= qry, 0.0, NEG_INF).astype(F32)
    first_pair = lax.broadcasted_iota(jnp.int32, (1, 2 * LANES), 1) < LANES
    p_pad = jnp.zeros((LANES - N_META, 2 * LANES), BF16)
    ones_band = jnp.ones((2 * SUBLANES, nband), BF16)
    ones_meta = jnp.ones((2 * SUBLANES, LANES), BF16)
    heads_per_kv = N_Q_HEADS // N_KV_HEADS

    units = [(h, e) for h in range(N_KV_HEADS) for e in range(2)]

    def score_stage(j):
        rows = slice(j * WINDOW, (j + 1) * WINDOW)
        blk0 = i * tq + j * WINDOW
        bias_prev = band_prev + jnp.where(blk0 >= WINDOW, 0.0, NEG_INF).astype(F32)
        bias_next = band_next + jnp.where(blk0 + WINDOW < seq, 0.0, NEG_INF).astype(F32)
        scores = []
        for h, e in units:
            q2 = jnp.concatenate([q_ref[rows, (2 * h) * LANES:(2 * h + 1) * LANES],
                                  q_ref[rows, (2 * h + 1) * LANES:(2 * h + 2) * LANES]], axis=0)
            sl = slice((2 * h + e) * LANES, (2 * h + e + 1) * LANES)
            keys = jnp.concatenate([kall[j * WINDOW:j * WINDOW + nband, sl], kmeta[:, sl]], axis=0)
            s = _dot_nt(keys, q2)
            scores.append(jnp.concatenate([s[:WINDOW] + bias_prev, s[WINDOW:2 * WINDOW],
                                           s[2 * WINDOW:nband] + bias_next, s[nband:]], axis=0))
        return scores

    def softmax_stage(scores):
        probs = []
        for (h, e), s in zip(units, scores):
            sink = LOG2E * jnp.where(first_pair, sink_ref[heads_per_kv * h + e],
                                     sink_ref[heads_per_kv * h + 2 + e])
            m = jnp.maximum(jnp.max(s, axis=0, keepdims=True), sink)
            probs.append((jnp.exp2(s - m).astype(BF16), jnp.exp2(sink - m)))
        return probs

    def value_stage(j, probs):
        rows = slice(j * WINDOW, (j + 1) * WINDOW)
        halves = []
        for (h, e), (pb, p_sink) in zip(units, probs):
            vband = jnp.concatenate([vall[h * HEAD_DIM:(h + 1) * HEAD_DIM, j * WINDOW:j * WINDOW + nband],
                                     ones_band], axis=0)
            vmet = jnp.concatenate([vmeta[h * HEAD_DIM:(h + 1) * HEAD_DIM, :], ones_meta], axis=0)
            p_meta = jnp.concatenate([pb[nband:], p_pad], axis=0)
            o = _dot(vband, pb[:nband]) + _dot(vmet, p_meta)
            denom = o[HEAD_DIM:HEAD_DIM + 1] + p_sink
            halves.append(o[:HEAD_DIM] * (1.0 / denom))
        for h in range(N_KV_HEADS):
            o_t = jnp.concatenate(halves[2 * h:2 * h + 2], axis=0)
            for pp in range(2):
                o_ref[rows, (2 * h + pp) * LANES:(2 * h + pp + 1) * LANES] = (
                    o_t[:, pp * LANES:(pp + 1) * LANES].T.astype(BF16))

    scores, probs = {}, {}
    for t in range(nblk + 2):
        if t < nblk:
            scores[t] = score_stage(t)
        if 0 <= t - 1 < nblk:
            probs[t - 1] = softmax_stage(scores.pop(t - 1))
        if 0 <= t - 2 < nblk:
            value_stage(t - 2, probs.pop(t - 2))


def _attn(q, k4, vt, k4_meta, vt_meta, sink, *, tq):
    B, S, _ = q.shape
    r = tq // WINDOW
    last = S // WINDOW - 1
    prev_idx = lambda i: jnp.maximum(i * r - 1, 0)
    next_idx = lambda i: jnp.minimum((i + 1) * r, last)
    cur = lambda w: pl.BlockSpec((None, tq, w), lambda b, i: (b, i, 0))
    kw = 4 * LANES
    return pl.pallas_call(
        functools.partial(_attn_kernel, tq=tq, seq=S),
        grid=(B, S // tq),
        in_specs=[pl.BlockSpec(memory_space=pltpu.SMEM), cur(ATTN_WIDTH),
                  cur(kw),
                  pl.BlockSpec((None, WINDOW, kw), lambda b, i: (b, prev_idx(i), 0)),
                  pl.BlockSpec((None, WINDOW, kw), lambda b, i: (b, next_idx(i), 0)),
                  pl.BlockSpec((None, KV_WIDTH, tq), lambda b, i: (b, 0, i)),
                  pl.BlockSpec((None, KV_WIDTH, WINDOW), lambda b, i: (b, 0, prev_idx(i))),
                  pl.BlockSpec((None, KV_WIDTH, WINDOW), lambda b, i: (b, 0, next_idx(i))),
                  pl.BlockSpec((N_META, kw), lambda b, i: (0, 0)),
                  pl.BlockSpec((KV_WIDTH, LANES), lambda b, i: (0, 0))],
        out_specs=cur(ATTN_WIDTH),
        out_shape=jax.ShapeDtypeStruct((B, S, ATTN_WIDTH), BF16),
        compiler_params=pltpu.CompilerParams(
            dimension_semantics=("parallel", "parallel"), vmem_limit_bytes=VMEM_LIMIT_BYTES),
        name="attn",
    )(sink, q, k4, k4, k4, vt, vt, vt, k4_meta, vt_meta)


def _factor(L):
    n1 = 16
    while L % (2 * n1) == 0:
        n1 *= 2
    rest = L // n1
    for f in (3, 5, 7, 9):
        while rest % f == 0 and n1 * f <= rest // f * 4:
            n1, rest = n1 * f, rest // f
    return n1, rest


def _round_up(x, m):
    return (x + m - 1) // m * m


def _odd_tiles(rows):
    p = _round_up(rows, SUBLANES)
    return p if (p // SUBLANES) % 2 else p + SUBLANES


def _fmix_consts(S):
    L = S + N_META
    N1, N2 = _factor(L)
    N2p = _round_up(N2, 2 * SUBLANES)
    dims = dict(L=L, N1=N1, N2=N2, N2p=N2p, P=_odd_tiles(2 * N1), Q=_odd_tiles(N2p),
                Lp=_round_up(L + N2p - N2, SUBLANES))
    k1 = np.arange(N1, dtype=np.int64)
    ang1 = 2.0 * np.pi * ((k1[:, None] * k1[None, :]) % N1) / N1
    c1, s1 = np.cos(ang1) / np.sqrt(N1), np.sin(ang1) / np.sqrt(N1)
    f1 = np.block([[c1, s1], [-s1, c1]]).astype(np.float32)
    k = k1[:, None, None] + N1 * np.arange(N2p, dtype=np.int64)[None, :, None]
    n2 = np.arange(N2p, dtype=np.int64)[None, None, :]
    ang3 = 2.0 * np.pi * (((n2 + N_META) * k) % L) / L
    valid = ((np.arange(N2p) < N2)[None, :, None] & (np.arange(N2p) < N2)[None, None, :])
    c3 = np.where(valid, np.cos(ang3), 0.0) / np.sqrt(N2)
    s3 = np.where(valid, np.sin(ang3), 0.0) / np.sqrt(N2)
    m3 = np.concatenate([c3, s3], axis=2).astype(np.float32)
    c = np.arange(FOURIER_GROUP_WIDTH, dtype=np.int64)
    angc = 2.0 * np.pi * ((c[:, None] * c[None, :]) % FOURIER_GROUP_WIDTH) / FOURIER_GROUP_WIDTH
    cd = (np.concatenate([np.cos(angc), -np.sin(angc)], axis=1)
          / np.sqrt(FOURIER_GROUP_WIDTH)).astype(np.float32)
    return dims, f1, m3, cd


def _fmix_kernel(u_ref, um_ref, cd_ref, f1_ref, m3_ref, o_ref, *scratch,
                 S, L, Lp, N1, N2, N2p, P, Q, G, NB, U1, U2, chunk):
    W = FOURIER_GROUP_WIDTH
    xr, xi, a_s, y_s = (scratch[i * G:(i + 1) * G] for i in range(4))
    cd = cd_ref[...]
    lanes = lambda g: slice(g * W, (g + 1) * W)

    def p0(c, carry):
        r0 = pl.multiple_of(c * chunk, chunk)
        for g in range(G):
            v = _dot(u_ref[pl.ds(r0, chunk), lanes(g)], cd)
            xr[g][pl.ds(r0, chunk), :] = v[:, :W]
            xi[g][pl.ds(r0, chunk), :] = v[:, W:]
        return carry

    lax.fori_loop(0, S // chunk, p0, 0)
    for g in range(G):
        vm = _dot(um_ref[:, lanes(g)], cd)
        xr[g][S:L, :] = vm[:, :W]
        xi[g][S:L, :] = vm[:, W:]
        xr[g][L:Lp, :] = jnp.zeros((Lp - L, W), F32)
        xi[g][L:Lp, :] = jnp.zeros((Lp - L, W), F32)

    def p1(t, carry):
        for uu in range(U1):
            n2s = [(t * U1 + uu) * NB + j for j in range(NB)]
            cols = [jnp.concatenate([xr[g][pl.ds(n2, N1, stride=N2), :],
                                     xi[g][pl.ds(n2, N1, stride=N2), :]], axis=0)
                    for n2 in n2s for g in range(G)]
            a = _dot(f1_ref[...], jnp.concatenate(cols, axis=1).astype(BF16))
            for j, n2 in enumerate(n2s):
                for g in range(G):
                    a_s[g][pl.ds(pl.multiple_of(n2 * P, SUBLANES), 2 * N1), :] = a[:, lanes(j * G + g)]
        return carry

    lax.fori_loop(0, N2p // (NB * U1), p1, 0)

    def p2(t, carry):
        for uu in range(U2):
            k1 = t * U2 + uu
            b = jnp.concatenate(
                [jnp.concatenate([a_s[g][pl.ds(k1, N2p, stride=P), :],
                                  a_s[g][pl.ds(N1 + k1, N2p, stride=P), :]], axis=0) for g in range(G)], axis=1)
            z = _dot(m3_ref[k1], b.astype(BF16))
            for g in range(G):
                y_s[g][pl.ds(pl.multiple_of(k1 * Q, SUBLANES), N2p), :] = z[:, lanes(g)]
        return carry

    lax.fori_loop(0, N1 // U2, p2, 0)

    for g in range(G):
        o_ref[0:N1 - N_META, lanes(g)] = y_s[g][pl.ds(0, N1, stride=Q), :][N_META:].astype(BF16)

    def p3(k2, carry):
        r0 = pl.multiple_of(k2 * N1 - N_META, 2 * SUBLANES)
        for g in range(G):
            o_ref[pl.ds(r0, N1), lanes(g)] = y_s[g][pl.ds(k2, N1, stride=Q), :].astype(BF16)
        return carry

    lax.fori_loop(1, N2, p3, 0)


def _fmix(u, u_meta, *, groups, u1, u2, chunk=512):
    B, S, _ = u.shape
    dims, f1, m3, cd = _fmix_consts(S)
    Lp, N1, N2p, P, Q = dims["Lp"], dims["N1"], dims["N2p"], dims["P"], dims["Q"]
    W, G = FOURIER_GROUP_WIDTH, groups
    nb = (2 * LANES * 2) // (G * W)
    const = lambda shape: pl.BlockSpec(shape, lambda b, g: (0,) * len(shape))
    scratch = ([pltpu.VMEM((Lp, W), F32)] * (2 * G) + [pltpu.VMEM((N2p * P, W), F32)] * G
               + [pltpu.VMEM((N1 * Q, W), F32)] * G)
    return pl.pallas_call(
        functools.partial(_fmix_kernel, S=S, chunk=chunk, G=G, NB=nb, U1=u1, U2=u2, **dims),
        grid=(B, N_FOURIER_GROUPS // G),
        in_specs=[pl.BlockSpec((None, S, G * W), lambda b, g: (b, 0, g)),
                  pl.BlockSpec((N_META, G * W), lambda b, g: (0, g)),
                  const(cd.shape), const(f1.shape), const(m3.shape)],
        out_specs=pl.BlockSpec((None, S, G * W), lambda b, g: (b, 0, g)),
        out_shape=jax.ShapeDtypeStruct((B, S, FOURIER_WIDTH), BF16),
        scratch_shapes=scratch,
        compiler_params=pltpu.CompilerParams(
            dimension_semantics=("parallel", "parallel"), vmem_limit_bytes=VMEM_LIMIT_BYTES),
        name="fmix",
    )(u, u_meta, jnp.asarray(cd).astype(BF16), jnp.asarray(f1).astype(BF16), jnp.asarray(m3).astype(BF16))


def _post_kernel(x_ref, a_ref, f_ref, gate_ref, wao_ref, wf_ref, wout_ref, g2_ref, wup_ref, wdown_ref,
                 gfin_ref, y_ref, *, ff_chunk):
    a = _dot(a_ref[...], wao_ref[...])
    f = _dot(f_ref[...], wf_ref[...])
    merged = (gate_ref[:, :D_MODEL].astype(F32) * a + gate_ref[:, D_MODEL:].astype(F32) * f).astype(BF16)
    h = x_ref[...] + _dot(merged, wout_ref[...])
    n = _rms(h, g2_ref[...]).astype(BF16)
    for c in range(D_FF // ff_chunk):
        t = _dot(n, wup_ref[:, c * ff_chunk:(c + 1) * ff_chunk])
        r = jnp.square(jnp.maximum(t, 0.0)).astype(BF16)
        h = h + _dot(r, wdown_ref[c * ff_chunk:(c + 1) * ff_chunk, :])
    y_ref[...] = _rms(h, gfin_ref[...])


def _post(x, a, f, gates, w_ao, w_f, w_out, g2, w_up, w_down, g_fin, *, tm, ff_chunk=1024):
    B, S, _ = x.shape
    tok = lambda w: pl.BlockSpec((None, tm, w), lambda b, i: (b, i, 0))
    const = lambda shape: pl.BlockSpec(shape, lambda b, i: (0,) * len(shape), pipeline_mode=pl.Buffered(1))
    return pl.pallas_call(
        functools.partial(_post_kernel, ff_chunk=ff_chunk),
        grid=(B, S // tm),
        in_specs=[tok(D_MODEL), tok(ATTN_WIDTH), tok(FOURIER_WIDTH), tok(N_BRANCHES * D_MODEL),
                  const(w_ao.shape), const(w_f.shape), const(w_out.shape), const((1, D_MODEL)),
                  const(w_up.shape), const(w_down.shape), const((1, D_MODEL))],
        out_specs=tok(D_MODEL),
        out_shape=jax.ShapeDtypeStruct((B, S, D_MODEL), F32),
        compiler_params=pltpu.CompilerParams(
            dimension_semantics=("parallel", "parallel"), vmem_limit_bytes=VMEM_LIMIT_BYTES),
        name="post",
    )(x, a, f, gates, w_ao, w_f, w_out, g2, w_up, w_down, g_fin)


def _rope_tables(n_pos):
    half = ROT_DIM // 2
    inv_freq = ROPE_THETA ** (-jnp.arange(half, dtype=F32) / half)
    pos = jnp.arange(n_pos, dtype=F32)
    ang = pos[:, None] * inv_freq[None, :]
    cos, sin = jnp.cos(ang), jnp.sin(ang)
    d = np.arange(LANES) % HEAD_DIM
    idx = jnp.asarray(d % half)
    cos_l, sin_l = cos[:, idx], sin[:, idx]
    first = jnp.asarray(d < half)[None, :]
    second = jnp.asarray((d >= half) & (d < ROT_DIM))[None, :]
    cos_t = jnp.where(first | second, cos_l, 1.0)
    sin_a = jnp.where(second, sin_l, 0.0)
    sin_b = jnp.where(first, -sin_l, 0.0)
    return cos_t, sin_a, sin_b


def _trunk(x, meta_parts, tables, wts, *, tm, tq, fmix):
    k4_m, vt_m, u_m = meta_parts
    S = x.shape[1]
    tabs = tuple(t[N_META:N_META + S] for t in tables)
    q, k4, vt, u, gates = _proj(x, tabs, wts["norm_mix_g"], wts["w_in"], wts["b_gate"], tm=tm, transpose_v=True)
    a = _attn(q, k4, vt, k4_m, vt_m, wts["attn_sink"], tq=tq)
    f = _fmix(u, u_m, **fmix)
    return _post(x, a, f, gates, wts["w_attn_out"], wts["w_fourier"], wts["w_out"], wts["norm_mlp_g"],
                 wts["w_mlp_up"], wts["w_mlp_down"], wts["norm_final_g"], tm=tm)


def kernel(x_prompt, x_sample, meta_tokens, norm_mix_g, w_in, b_gate, attn_sink, w_attn_out, w_fourier, w_out,
           norm_mlp_g, w_mlp_up, w_mlp_down, norm_final_g):
    assert w_in.shape[0] == 1, "single-layer trunk: meta-token outputs are never consumed"
    wts = dict(
        norm_mix_g=norm_mix_g[0][None, :], w_in=w_in[0].astype(BF16), b_gate=b_gate[0][None, :],
        attn_sink=attn_sink[0], w_attn_out=w_attn_out[0].astype(BF16), w_fourier=w_fourier[0].astype(BF16),
        w_out=w_out[0].astype(BF16), norm_mlp_g=norm_mlp_g[0][None, :], w_mlp_up=w_mlp_up[0].astype(BF16),
        w_mlp_down=w_mlp_down[0].astype(BF16), norm_final_g=norm_final_g[None, :])
    n_pos = N_META + max(x_prompt.shape[1], x_sample.shape[1])
    tables = _rope_tables(n_pos)
    _, k4_m, v_m, u_m, _ = _proj(meta_tokens[None], tuple(t[:N_META] for t in tables),
                                 wts["norm_mix_g"], wts["w_in"], wts["b_gate"], tm=N_META, transpose_v=False)
    vt_m = jnp.pad(v_m[0].T, ((0, 0), (0, LANES - N_META)))
    meta_parts = (k4_m[0], vt_m, u_m[0])
    y_prompt = _trunk(x_prompt, meta_parts, tables, wts, tm=512, tq=512, fmix=dict(groups=4, u1=4, u2=12))
    y_sample = _trunk(x_sample, meta_parts, tables, wts, tm=512, tq=512, fmix=dict(groups=1, u1=4, u2=8))
    return (y_prompt, y_sample)
```

```python
import functools

import numpy as np
import jax
import jax.numpy as jnp
from jax import lax
from jax.experimental import pallas as pl
from jax.experimental.pallas import tpu as pltpu

D_MODEL = 1024
HEAD_DIM = 64
N_Q_HEADS = 8
N_KV_HEADS = 2
ATTN_WIDTH = N_Q_HEADS * HEAD_DIM
KV_WIDTH = N_KV_HEADS * HEAD_DIM
WINDOW = 128
ROPE_THETA = 500000.0
ROT_DIM = HEAD_DIM // 4
N_FOURIER_GROUPS = 4
FOURIER_GROUP_WIDTH = 128
FOURIER_WIDTH = N_FOURIER_GROUPS * FOURIER_GROUP_WIDTH
N_BRANCHES = 2
IN_WIDTH = ATTN_WIDTH + 2 * KV_WIDTH + FOURIER_WIDTH + N_BRANCHES * D_MODEL
D_FF = 4 * D_MODEL
N_META = 16
RMS_EPS = 1e-6
NEG_INF = -1e30
LOG2E = 1.4426950408889634

LANES = 128
SUBLANES = 8
VMEM_LIMIT_BYTES = 56 * 1024 * 1024

_Q0, _K0, _V0, _U0, _G0 = 0, ATTN_WIDTH, ATTN_WIDTH + KV_WIDTH, ATTN_WIDTH + 2 * KV_WIDTH, \
    ATTN_WIDTH + 2 * KV_WIDTH + FOURIER_WIDTH

BF16 = jnp.bfloat16
F32 = jnp.float32


def _dot(a, b):
    return jnp.dot(a, b, preferred_element_type=F32)


def _dot_nt(a, b):
    return lax.dot_general(a, b, (((1,), (1,)), ((), ())), preferred_element_type=F32)


def _rms(x, g):
    return x * lax.rsqrt(jnp.mean(x * x, axis=-1, keepdims=True) + RMS_EPS) * g


def _proj_kernel(x_ref, tab_ref, g_ref, w_ref, b_ref,
                 q_ref, k4_ref, v_ref, u_ref, gate_ref, *, transpose_v):
    n = _rms(x_ref[...], g_ref[...]).astype(BF16)
    cos_t, sin_a, sin_b = (tab_ref[:, c * LANES:(c + 1) * LANES] for c in range(3))

    def rope(z):
        return (z * cos_t + pltpu.roll(z, ROT_DIM // 2, 1) * sin_a
                + pltpu.roll(z, LANES - ROT_DIM // 2, 1) * sin_b)

    zq = _dot(n, w_ref[:, _Q0:_K0])
    for c in range(ATTN_WIDTH // LANES):
        q_ref[:, c * LANES:(c + 1) * LANES] = (
            rope(zq[:, c * LANES:(c + 1) * LANES]) * (LOG2E * HEAD_DIM ** -0.5)).astype(BF16)
    zkv = _dot(n, w_ref[:, _K0:_U0])
    zk = rope(zkv[:, :KV_WIDTH])
    zv = zkv[:, KV_WIDTH:]
    zks = pltpu.roll(zk, HEAD_DIM, 1)
    low = lax.broadcasted_iota(jnp.int32, zk.shape, 1) < HEAD_DIM
    for s, (keep_low, src) in enumerate(((True, zk), (False, zks), (True, zks), (False, zk))):
        k4_ref[:, s * LANES:(s + 1) * LANES] = jnp.where(low == keep_low, src, 0.0).astype(BF16)
    v_ref[...] = (zv.T if transpose_v else zv).astype(BF16)
    u_ref[...] = _dot(n, w_ref[:, _U0:_G0]).astype(BF16)
    for c in range(N_BRANCHES):
        lo = _G0 + c * D_MODEL
        g = _dot(n, w_ref[:, lo:lo + D_MODEL]) + b_ref[:, c * D_MODEL:(c + 1) * D_MODEL]
        gate_ref[:, c * D_MODEL:(c + 1) * D_MODEL] = jax.nn.sigmoid(g).astype(BF16)


def _proj(x, table, g, w_in, b_gate, *, tm, transpose_v):
    B, S, _ = x.shape
    tok = lambda w: pl.BlockSpec((None, tm, w), lambda b, i: (b, i, 0))
    tab = pl.BlockSpec((tm, 3 * LANES), lambda b, i: (i, 0))
    const = lambda shape: pl.BlockSpec(shape, lambda b, i: (0,) * len(shape))
    out_w = (ATTN_WIDTH, 4 * LANES, KV_WIDTH, FOURIER_WIDTH, N_BRANCHES * D_MODEL)
    out_specs = [tok(w) for w in out_w]
    out_shape = [jax.ShapeDtypeStruct((B, S, w), BF16) for w in out_w]
    if transpose_v:
        out_specs[2] = pl.BlockSpec((None, KV_WIDTH, tm), lambda b, i: (b, 0, i))
        out_shape[2] = jax.ShapeDtypeStruct((B, KV_WIDTH, S), BF16)
    return pl.pallas_call(
        functools.partial(_proj_kernel, transpose_v=transpose_v),
        grid=(B, S // tm),
        in_specs=[tok(D_MODEL), tab, const((1, D_MODEL)),
                  const((D_MODEL, IN_WIDTH)), const((1, N_BRANCHES * D_MODEL))],
        out_specs=out_specs,
        out_shape=out_shape,
        compiler_params=pltpu.CompilerParams(
            dimension_semantics=("parallel", "parallel"), vmem_limit_bytes=VMEM_LIMIT_BYTES),
        name="proj",
    )(x, table, g, w_in, b_gate)


def _attn_kernel(sink_ref, q_ref, kc_ref, kp_ref, kn_ref, vc_ref, vp_ref, vn_ref, km_ref, vm_ref,
                 o_ref, *, tq, seq):
    i = pl.program_id(1)
    nblk = tq // WINDOW
    nband = 3 * WINDOW
    nkey = nband + N_META
    kall = jnp.concatenate([kp_ref[...], kc_ref[...], kn_ref[...]], axis=0)
    vall = jnp.concatenate([vp_ref[...], vc_ref[...], vn_ref[...]], axis=1)
    kmeta, vmeta = km_ref[...], vm_ref[...]
    key = lax.broadcasted_iota(jnp.int32, (WINDOW, 2 * LANES), 0)
    qry = jnp.bitwise_and(lax.broadcasted_iota(jnp.int32, (WINDOW, 2 * LANES), 1), WINDOW - 1)
    band_prev = jnp.where(key >= qry, 0.0, NEG_INF).astype(F32)
    band_next = jnp.where(key <= qry, 0.0, NEG_INF).astype(F32)
    first_pair = lax.broadcasted_iota(jnp.int32, (1, 2 * LANES), 1) < LANES
    p_pad = jnp.zeros((LANES - N_META, 2 * LANES), BF16)
    ones_band = jnp.ones((2 * SUBLANES, nband), BF16)
    ones_meta = jnp.ones((2 * SUBLANES, LANES), BF16)
    heads_per_kv = N_Q_HEADS // N_KV_HEADS

    units = [(h, e) for h in range(N_KV_HEADS) for e in range(2)]

    def score_stage(j):
        rows = slice(j * WINDOW, (j + 1) * WINDOW)
        blk0 = i * tq + j * WINDOW
        bias_prev = band_prev + jnp.where(blk0 >= WINDOW, 0.0, NEG_INF).astype(F32)
        bias_next = band_next + jnp.where(blk0 + WINDOW < seq, 0.0, NEG_INF).astype(F32)
        scores = []
        for h, e in units:
            q2 = jnp.concatenate([q_ref[rows, (2 * h) * LANES:(2 * h + 1) * LANES],
                                  q_ref[rows, (2 * h + 1) * LANES:(2 * h + 2) * LANES]], axis=0)
            sl = slice((2 * h + e) * LANES, (2 * h + e + 1) * LANES)
            keys = jnp.concatenate([kall[j * WINDOW:j * WINDOW + nband, sl], kmeta[:, sl]], axis=0)
            s = _dot_nt(keys, q2)
            scores.append(jnp.concatenate([s[:WINDOW] + bias_prev, s[WINDOW:2 * WINDOW],
                                           s[2 * WINDOW:nband] + bias_next, s[nband:]], axis=0))
        return scores

    def softmax_stage(scores):
        probs = []
        for (h, e), s in zip(units, scores):
            sink = LOG2E * jnp.where(first_pair, sink_ref[heads_per_kv * h + e],
                                     sink_ref[heads_per_kv * h + 2 + e])
            m = jnp.maximum(jnp.max(s, axis=0, keepdims=True), sink)
            probs.append((jnp.exp2(s - m).astype(BF16), jnp.exp2(sink - m)))
        return probs

    def value_stage(j, probs):
        rows = slice(j * WINDOW, (j + 1) * WINDOW)
        halves = []
        for (h, e), (pb, p_sink) in zip(units, probs):
            vband = jnp.concatenate([vall[h * HEAD_DIM:(h + 1) * HEAD_DIM, j * WINDOW:j * WINDOW + nband],
                                     ones_band], axis=0)
            vmet = jnp.concatenate([vmeta[h * HEAD_DIM:(h + 1) * HEAD_DIM, :], ones_meta], axis=0)
            p_meta = jnp.concatenate([pb[nband:], p_pad], axis=0)
            o = _dot(vband, pb[:nband]) + _dot(vmet, p_meta)
            denom = o[HEAD_DIM:HEAD_DIM + 1] + p_sink
            halves.append(o[:HEAD_DIM] * (1.0 / denom))
        for h in range(N_KV_HEADS):
            o_t = jnp.concatenate(halves[2 * h:2 * h + 2], axis=0)
            for pp in range(2):
                o_ref[rows, (2 * h + pp) * LANES:(2 * h + pp + 1) * LANES] = (
                    o_t[:, pp * LANES:(pp + 1) * LANES].T.astype(BF16))

    scores, probs = {}, {}
    for t in range(nblk + 2):
        if t < nblk:
            scores[t] = score_stage(t)
        if 0 <= t - 1 < nblk:
            probs[t - 1] = softmax_stage(scores.pop(t - 1))
        if 0 <= t - 2 < nblk:
            value_stage(t - 2, probs.pop(t - 2))


def _attn(q, k4, vt, k4_meta, vt_meta, sink, *, tq):
    B, S, _ = q.shape
    r = tq // WINDOW
    last = S // WINDOW - 1
    prev_idx = lambda i: jnp.maximum(i * r - 1, 0)
    next_idx = lambda i: jnp.minimum((i + 1) * r, last)
    cur = lambda w: pl.BlockSpec((None, tq, w), lambda b, i: (b, i, 0))
    kw = 4 * LANES
    return pl.pallas_call(
        functools.partial(_attn_kernel, tq=tq, seq=S),
        grid=(B, S // tq),
        in_specs=[pl.BlockSpec(memory_space=pltpu.SMEM), cur(ATTN_WIDTH),
                  cur(kw),
                  pl.BlockSpec((None, WINDOW, kw), lambda b, i: (b, prev_idx(i), 0)),
                  pl.BlockSpec((None, WINDOW, kw), lambda b, i: (b, next_idx(i), 0)),
                  pl.BlockSpec((None, KV_WIDTH, tq), lambda b, i: (b, 0, i)),
                  pl.BlockSpec((None, KV_WIDTH, WINDOW), lambda b, i: (b, 0, prev_idx(i))),
                  pl.BlockSpec((None, KV_WIDTH, WINDOW), lambda b, i: (b, 0, next_idx(i))),
                  pl.BlockSpec((N_META, kw), lambda b, i: (0, 0)),
                  pl.BlockSpec((KV_WIDTH, LANES), lambda b, i: (0, 0))],
        out_specs=cur(ATTN_WIDTH),
        out_shape=jax.ShapeDtypeStruct((B, S, ATTN_WIDTH), BF16),
        compiler_params=pltpu.CompilerParams(
            dimension_semantics=("parallel", "parallel"), vmem_limit_bytes=VMEM_LIMIT_BYTES),
        name="attn",
    )(sink, q, k4, k4, k4, vt, vt, vt, k4_meta, vt_meta)


def _factor(L):
    n1 = 16
    while L % (2 * n1) == 0:
        n1 *= 2
    rest = L // n1
    for f in (3, 5, 7, 9):
        while rest % f == 0 and n1 * f <= rest // f * 4:
            n1, rest = n1 * f, rest // f
    return n1, rest


def _round_up(x, m):
    return (x + m - 1) // m * m


def _odd_tiles(rows):
    p = _round_up(rows, SUBLANES)
    return p if (p // SUBLANES) % 2 else p + SUBLANES


def _fmix_consts(S):
    L = S + N_META
    N1, N2 = _factor(L)
    N2p = _round_up(N2, 2 * SUBLANES)
    dims = dict(L=L, N1=N1, N2=N2, N2p=N2p, P=_odd_tiles(2 * N1), Q=_odd_tiles(N2p),
                Lp=_round_up(L + N2p - N2, SUBLANES))
    k1 = np.arange(N1, dtype=np.int64)
    ang1 = 2.0 * np.pi * ((k1[:, None] * k1[None, :]) % N1) / N1
    c1, s1 = np.cos(ang1) / np.sqrt(N1), np.sin(ang1) / np.sqrt(N1)
    f1 = np.block([[c1, s1], [-s1, c1]]).astype(np.float32)
    k = k1[:, None, None] + N1 * np.arange(N2p, dtype=np.int64)[None, :, None]
    n2 = np.arange(N2p, dtype=np.int64)[None, None, :]
    ang3 = 2.0 * np.pi * (((n2 + N_META) * k) % L) / L
    valid = ((np.arange(N2p) < N2)[None, :, None] & (np.arange(N2p) < N2)[None, None, :])
    c3 = np.where(valid, np.cos(ang3), 0.0) / np.sqrt(N2)
    s3 = np.where(valid, np.sin(ang3), 0.0) / np.sqrt(N2)
    m3 = np.concatenate([c3, s3], axis=2).astype(np.float32)
    c = np.arange(FOURIER_GROUP_WIDTH, dtype=np.int64)
    angc = 2.0 * np.pi * ((c[:, None] * c[None, :]) % FOURIER_GROUP_WIDTH) / FOURIER_GROUP_WIDTH
    cd = (np.concatenate([np.cos(angc), -np.sin(angc)], axis=1)
          / np.sqrt(FOURIER_GROUP_WIDTH)).astype(np.float32)
    return dims, f1, m3, cd


def _fmix_kernel(u_ref, um_ref, cd_ref, f1_ref, m3_ref, o_ref, *scratch,
                 S, L, Lp, N1, N2, N2p, P, Q, G, NB, U1, U2, chunk):
    W = FOURIER_GROUP_WIDTH
    xr, xi, a_s, y_s = (scratch[i * G:(i + 1) * G] for i in range(4))
    cd = cd_ref[...]
    lanes = lambda g: slice(g * W, (g + 1) * W)

    per_iter = 4 // G
    def p0(c, carry):
        for uu in range(per_iter):
            r0 = pl.multiple_of((c * per_iter + uu) * chunk, chunk)
            for g in range(G):
                v = _dot(u_ref[pl.ds(r0, chunk), lanes(g)], cd)
                xr[g][pl.ds(r0, chunk), :] = v[:, :W]
                xi[g][pl.ds(r0, chunk), :] = v[:, W:]
        return carry

    lax.fori_loop(0, S // (chunk * per_iter), p0, 0)
    for g in range(G):
        vm = _dot(um_ref[:, lanes(g)], cd)
        xr[g][S:L, :] = vm[:, :W]
        xi[g][S:L, :] = vm[:, W:]
        xr[g][L:Lp, :] = jnp.zeros((Lp - L, W), F32)
        xi[g][L:Lp, :] = jnp.zeros((Lp - L, W), F32)

    def p1(t, carry):
        for uu in range(U1):
            n2s = [(t * U1 + uu) * NB + j for j in range(NB)]
            cols = [jnp.concatenate([xr[g][pl.ds(n2, N1, stride=N2), :],
                                     xi[g][pl.ds(n2, N1, stride=N2), :]], axis=0)
                    for n2 in n2s for g in range(G)]
            a = _dot(f1_ref[...], jnp.concatenate(cols, axis=1).astype(BF16))
            for j, n2 in enumerate(n2s):
                for g in range(G):
                    a_s[g][pl.ds(pl.multiple_of(n2 * P, SUBLANES), 2 * N1), :] = a[:, lanes(j * G + g)]
        return carry

    lax.fori_loop(0, N2p // (NB * U1), p1, 0)

    def p2(t, carry):
        for uu in range(U2):
            k1 = t * U2 + uu
            b = jnp.concatenate(
                [jnp.concatenate([a_s[g][pl.ds(k1, N2p, stride=P), :],
                                  a_s[g][pl.ds(N1 + k1, N2p, stride=P), :]], axis=0) for g in range(G)], axis=1)
            z = _dot(m3_ref[k1], b.astype(BF16))
            for g in range(G):
                y_s[g][pl.ds(pl.multiple_of(k1 * Q, SUBLANES), N2p), :] = z[:, lanes(g)]
        return carry

    lax.fori_loop(0, N1 // U2, p2, 0)

    for g in range(G):
        o_ref[0:N1 - N_META, lanes(g)] = y_s[g][pl.ds(0, N1, stride=Q), :][N_META:].astype(BF16)

    def p3(t, carry):
        for uu in range(2):
            k2 = 2 * t + 1 + uu
            r0 = pl.multiple_of(k2 * N1 - N_META, 2 * SUBLANES)
            for g in range(G):
                o_ref[pl.ds(r0, N1), lanes(g)] = y_s[g][pl.ds(k2, N1, stride=Q), :].astype(BF16)
        return carry

    lax.fori_loop(0, (N2 - 1) // 2, p3, 0)


def _fmix(u, u_meta, *, groups, u1, u2, chunk=512):
    B, S, _ = u.shape
    dims, f1, m3, cd = _fmix_consts(S)
    Lp, N1, N2p, P, Q = dims["Lp"], dims["N1"], dims["N2p"], dims["P"], dims["Q"]
    W, G = FOURIER_GROUP_WIDTH, groups
    nb = (2 * LANES * 2) // (G * W)
    const = lambda shape: pl.BlockSpec(shape, lambda b, g: (0,) * len(shape))
    scratch = ([pltpu.VMEM((Lp, W), F32)] * (2 * G) + [pltpu.VMEM((N2p * P, W), F32)] * G
               + [pltpu.VMEM((N1 * Q, W), F32)] * G)
    return pl.pallas_call(
        functools.partial(_fmix_kernel, S=S, chunk=chunk, G=G, NB=nb, U1=u1, U2=u2, **dims),
        grid=(B, N_FOURIER_GROUPS // G),
        in_specs=[pl.BlockSpec((None, S, G * W), lambda b, g: (b, 0, g)),
                  pl.BlockSpec((N_META, G * W), lambda b, g: (0, g)),
                  const(cd.shape), const(f1.shape), const(m3.shape)],
        out_specs=pl.BlockSpec((None, S, G * W), lambda b, g: (b, 0, g)),
        out_shape=jax.ShapeDtypeStruct((B, S, FOURIER_WIDTH), BF16),
        scratch_shapes=scratch,
        compiler_params=pltpu.CompilerParams(
            dimension_semantics=("parallel", "parallel"), vmem_limit_bytes=VMEM_LIMIT_BYTES),
        name="fmix",
    )(u, u_meta, jnp.asarray(cd).astype(BF16), jnp.asarray(f1).astype(BF16), jnp.asarray(m3).astype(BF16))


def _post_kernel(x_ref, a_ref, f_ref, gate_ref, wao_ref, wf_ref, wout_ref, g2_ref, wup_ref, wdown_ref,
                 gfin_ref, y_ref, *, ff_chunk):
    a = _dot(a_ref[...], wao_ref[...])
    f = _dot(f_ref[...], wf_ref[...])
    merged = (gate_ref[:, :D_MODEL].astype(F32) * a + gate_ref[:, D_MODEL:].astype(F32) * f).astype(BF16)
    h = x_ref[...] + _dot(merged, wout_ref[...])
    n = _rms(h, g2_ref[...]).astype(BF16)
    for c in range(D_FF // ff_chunk):
        t = _dot(n, wup_ref[:, c * ff_chunk:(c + 1) * ff_chunk])
        r = jnp.square(jnp.maximum(t, 0.0)).astype(BF16)
        h = h + _dot(r, wdown_ref[c * ff_chunk:(c + 1) * ff_chunk, :])
    y_ref[...] = _rms(h, gfin_ref[...])


def _post(x, a, f, gates, w_ao, w_f, w_out, g2, w_up, w_down, g_fin, *, tm, ff_chunk=1024):
    B, S, _ = x.shape
    tok = lambda w: pl.BlockSpec((None, tm, w), lambda b, i: (b, i, 0))
    const = lambda shape: pl.BlockSpec(shape, lambda b, i: (0,) * len(shape), pipeline_mode=pl.Buffered(1))
    return pl.pallas_call(
        functools.partial(_post_kernel, ff_chunk=ff_chunk),
        grid=(B, S // tm),
        in_specs=[tok(D_MODEL), tok(ATTN_WIDTH), tok(FOURIER_WIDTH), tok(N_BRANCHES * D_MODEL),
                  const(w_ao.shape), const(w_f.shape), const(w_out.shape), const((1, D_MODEL)),
                  const(w_up.shape), const(w_down.shape), const((1, D_MODEL))],
        out_specs=tok(D_MODEL),
        out_shape=jax.ShapeDtypeStruct((B, S, D_MODEL), F32),
        compiler_params=pltpu.CompilerParams(
            dimension_semantics=("parallel", "parallel"), vmem_limit_bytes=VMEM_LIMIT_BYTES),
        name="post",
    )(x, a, f, gates, w_ao, w_f, w_out, g2, w_up, w_down, g_fin)


def _rope_table(first_pos, n_pos):
    half = ROT_DIM // 2
    inv_freq = ROPE_THETA ** (-jnp.arange(half, dtype=F32) / half)
    pos = jnp.arange(first_pos, first_pos + n_pos, dtype=jnp.int32).astype(F32)
    ang = pos[:, None] * inv_freq[None, :]
    cos, sin = jnp.cos(ang), jnp.sin(ang)
    rest = HEAD_DIM - ROT_DIM
    head = lambda a, b, fill: jnp.concatenate([a, b, jnp.full((n_pos, rest), fill, F32)], axis=1)
    zero = jnp.zeros_like(sin)
    per_head = (head(cos, cos, 1.0), head(zero, sin, 0.0), head(-sin, zero, 0.0))
    return jnp.concatenate([jnp.tile(t, (1, LANES // HEAD_DIM)) for t in per_head], axis=1)


def _trunk(x, meta_parts, table, wts, *, tm_proj, tm, tq, fmix):
    k4_m, vt_m, u_m = meta_parts
    q, k4, vt, u, gates = _proj(x, table, wts["norm_mix_g"], wts["w_in"], wts["b_gate"], tm=tm_proj,
                                transpose_v=True)
    a = _attn(q, k4, vt, k4_m, vt_m, wts["attn_sink"], tq=tq)
    f = _fmix(u, u_m, **fmix)
    return _post(x, a, f, gates, wts["w_attn_out"], wts["w_fourier"], wts["w_out"], wts["norm_mlp_g"],
                 wts["w_mlp_up"], wts["w_mlp_down"], wts["norm_final_g"], tm=tm)


def kernel(x_prompt, x_sample, meta_tokens, norm_mix_g, w_in, b_gate, attn_sink, w_attn_out, w_fourier, w_out,
           norm_mlp_g, w_mlp_up, w_mlp_down, norm_final_g):
    assert w_in.shape[0] == 1, "single-layer trunk: meta-token outputs are never consumed"
    wts = dict(
        norm_mix_g=norm_mix_g[0][None, :], w_in=w_in[0].astype(BF16), b_gate=b_gate[0][None, :],
        attn_sink=attn_sink[0], w_attn_out=w_attn_out[0].astype(BF16), w_fourier=w_fourier[0].astype(BF16),
        w_out=w_out[0].astype(BF16), norm_mlp_g=norm_mlp_g[0][None, :], w_mlp_up=w_mlp_up[0].astype(BF16),
        w_mlp_down=w_mlp_down[0].astype(BF16), norm_final_g=norm_final_g[None, :])
    table = _rope_table(N_META, max(x_prompt.shape[1], x_sample.shape[1]))
    _, k4_m, v_m, u_m, _ = _proj(meta_tokens[None], _rope_table(0, N_META),
                                 wts["norm_mix_g"], wts["w_in"], wts["b_gate"], tm=N_META, transpose_v=False)
    vt_m = jnp.pad(v_m[0].T, ((0, 0), (0, LANES - N_META)))
    meta_parts = (k4_m[0], vt_m, u_m[0])
    tiles = dict(tm_proj=1024, tm=512, tq=1024)
    y_prompt = _trunk(x_prompt, meta_parts, table, wts, fmix=dict(groups=4, u1=4, u2=12), **tiles)
    y_sample = _trunk(x_sample, meta_parts, table, wts, fmix=dict(groups=1, u1=4, u2=8), **tiles)
    return (y_prompt, y_sample)
```

```python
import functools

import numpy as np
import jax
import jax.numpy as jnp
from jax import lax
from jax.experimental import pallas as pl
from jax.experimental.pallas import tpu as pltpu

D_MODEL = 1024
HEAD_DIM = 64
N_Q_HEADS = 8
N_KV_HEADS = 2
ATTN_WIDTH = N_Q_HEADS * HEAD_DIM
KV_WIDTH = N_KV_HEADS * HEAD_DIM
WINDOW = 128
ROPE_THETA = 500000.0
ROT_DIM = HEAD_DIM // 4
N_FOURIER_GROUPS = 4
FOURIER_GROUP_WIDTH = 128
FOURIER_WIDTH = N_FOURIER_GROUPS * FOURIER_GROUP_WIDTH
N_BRANCHES = 2
IN_WIDTH = ATTN_WIDTH + 2 * KV_WIDTH + FOURIER_WIDTH + N_BRANCHES * D_MODEL
D_FF = 4 * D_MODEL
N_META = 16
RMS_EPS = 1e-6
NEG_INF = -1e30
LOG2E = 1.4426950408889634

LANES = 128
SUBLANES = 8
VMEM_LIMIT_BYTES = 56 * 1024 * 1024

_Q0, _K0, _V0, _U0, _G0 = 0, ATTN_WIDTH, ATTN_WIDTH + KV_WIDTH, ATTN_WIDTH + 2 * KV_WIDTH, \
    ATTN_WIDTH + 2 * KV_WIDTH + FOURIER_WIDTH

BF16 = jnp.bfloat16
F32 = jnp.float32


def _dot(a, b):
    return jnp.dot(a, b, preferred_element_type=F32)


def _dot_nt(a, b):
    return lax.dot_general(a, b, (((1,), (1,)), ((), ())), preferred_element_type=F32)


def _rms(x, g):
    return x * lax.rsqrt(jnp.mean(x * x, axis=-1, keepdims=True) + RMS_EPS) * g


def _proj_kernel(x_ref, tab_ref, g_ref, w_ref, b_ref,
                 q_ref, k4_ref, v_ref, u_ref, gate_ref, *, transpose_v, sub):
    tm = x_ref.shape[0]
    blocks = [slice(r, r + sub) for r in range(0, tm, sub)]
    normed = {rows.start: _rms(x_ref[rows, :], g_ref[...]).astype(BF16) for rows in blocks}
    project = lambda rows, lo, hi: _dot(normed[rows.start], w_ref[:, lo:hi])

    def rope(z, rows):
        cos_t, sin_a, sin_b = (tab_ref[rows, c * LANES:(c + 1) * LANES] for c in range(3))
        return (z * cos_t + pltpu.roll(z, ROT_DIM // 2, 1) * sin_a
                + pltpu.roll(z, LANES - ROT_DIM // 2, 1) * sin_b)

    for rows in blocks:
        zq = project(rows, _Q0, _K0)
        for c in range(ATTN_WIDTH // LANES):
            q_ref[rows, c * LANES:(c + 1) * LANES] = (
                rope(zq[:, c * LANES:(c + 1) * LANES], rows) * (LOG2E * HEAD_DIM ** -0.5)).astype(BF16)
    for rows in blocks:
        zkv = project(rows, _K0, _U0)
        zk = rope(zkv[:, :KV_WIDTH], rows)
        zv = zkv[:, KV_WIDTH:]
        zks = pltpu.roll(zk, HEAD_DIM, 1)
        low = lax.broadcasted_iota(jnp.int32, zk.shape, 1) < HEAD_DIM
        for s, (keep_low, src) in enumerate(((True, zk), (False, zks), (True, zks), (False, zk))):
            k4_ref[rows, s * LANES:(s + 1) * LANES] = jnp.where(low == keep_low, src, 0.0).astype(BF16)
        if transpose_v:
            v_ref[:, rows] = zv.T.astype(BF16)
        else:
            v_ref[rows, :] = zv.astype(BF16)
    for rows in blocks:
        u_ref[rows, :] = project(rows, _U0, _G0).astype(BF16)
    for c in range(N_BRANCHES):
        lo = _G0 + c * D_MODEL
        for rows in blocks:
            g = project(rows, lo, lo + D_MODEL) + b_ref[:, c * D_MODEL:(c + 1) * D_MODEL]
            gate_ref[rows, c * D_MODEL:(c + 1) * D_MODEL] = jax.nn.sigmoid(g).astype(BF16)


def _proj(x, table, g, w_in, b_gate, *, tm, transpose_v):
    B, S, _ = x.shape
    tok = lambda w: pl.BlockSpec((None, tm, w), lambda b, i: (b, i, 0))
    tab = pl.BlockSpec((tm, 3 * LANES), lambda b, i: (i, 0))
    const = lambda shape: pl.BlockSpec(shape, lambda b, i: (0,) * len(shape))
    out_w = (ATTN_WIDTH, 4 * LANES, KV_WIDTH, FOURIER_WIDTH, N_BRANCHES * D_MODEL)
    out_specs = [tok(w) for w in out_w]
    out_shape = [jax.ShapeDtypeStruct((B, S, w), BF16) for w in out_w]
    if transpose_v:
        out_specs[2] = pl.BlockSpec((None, KV_WIDTH, tm), lambda b, i: (b, 0, i))
        out_shape[2] = jax.ShapeDtypeStruct((B, KV_WIDTH, S), BF16)
    return pl.pallas_call(
        functools.partial(_proj_kernel, transpose_v=transpose_v, sub=min(tm, 512)),
        grid=(B, S // tm),
        in_specs=[tok(D_MODEL), tab, const((1, D_MODEL)),
                  const((D_MODEL, IN_WIDTH)), const((1, N_BRANCHES * D_MODEL))],
        out_specs=out_specs,
        out_shape=out_shape,
        compiler_params=pltpu.CompilerParams(
            dimension_semantics=("parallel", "parallel"), vmem_limit_bytes=VMEM_LIMIT_BYTES),
        name="proj",
    )(x, table, g, w_in, b_gate)


def _attn_kernel(sink_ref, q_ref, kc_ref, kp_ref, kn_ref, vc_ref, vp_ref, vn_ref, km_ref, vm_ref,
                 o_ref, *, tq, seq):
    i = pl.program_id(1)
    nblk = tq // WINDOW
    nband = 3 * WINDOW
    nkey = nband + N_META
    kall = jnp.concatenate([kp_ref[...], kc_ref[...], kn_ref[...]], axis=0)
    vall = jnp.concatenate([vp_ref[...], vc_ref[...], vn_ref[...]], axis=1)
    kmeta, vmeta = km_ref[...], vm_ref[...]
    key = lax.broadcasted_iota(jnp.int32, (WINDOW, 2 * LANES), 0)
    qry = jnp.bitwise_and(lax.broadcasted_iota(jnp.int32, (WINDOW, 2 * LANES), 1), WINDOW - 1)
    band_prev = jnp.where(key >= qry, 0.0, NEG_INF).astype(F32)
    band_next = jnp.where(key <= qry, 0.0, NEG_INF).astype(F32)
    first_pair = lax.broadcasted_iota(jnp.int32, (1, 2 * LANES), 1) < LANES
    p_pad = jnp.zeros((LANES - N_META, 2 * LANES), BF16)
    ones_band = jnp.ones((2 * SUBLANES, nband), BF16)
    ones_meta = jnp.ones((2 * SUBLANES, LANES), BF16)
    heads_per_kv = N_Q_HEADS // N_KV_HEADS

    units = [(h, e) for h in range(N_KV_HEADS) for e in range(2)]

    def score_stage(j):
        rows = slice(j * WINDOW, (j + 1) * WINDOW)
        blk0 = i * tq + j * WINDOW
        bias_prev = band_prev + jnp.where(blk0 >= WINDOW, 0.0, NEG_INF).astype(F32)
        bias_next = band_next + jnp.where(blk0 + WINDOW < seq, 0.0, NEG_INF).astype(F32)
        scores = []
        for h, e in units:
            q2 = jnp.concatenate([q_ref[rows, (2 * h) * LANES:(2 * h + 1) * LANES],
                                  q_ref[rows, (2 * h + 1) * LANES:(2 * h + 2) * LANES]], axis=0)
            sl = slice((2 * h + e) * LANES, (2 * h + e + 1) * LANES)
            keys = jnp.concatenate([kall[j * WINDOW:j * WINDOW + nband, sl], kmeta[:, sl]], axis=0)
            s = _dot_nt(keys, q2)
            scores.append(jnp.concatenate([s[:WINDOW] + bias_prev, s[WINDOW:2 * WINDOW],
                                           s[2 * WINDOW:nband] + bias_next, s[nband:]], axis=0))
        return scores

    def softmax_stage(scores):
        probs = []
        for (h, e), s in zip(units, scores):
            sink = LOG2E * jnp.where(first_pair, sink_ref[heads_per_kv * h + e],
                                     sink_ref[heads_per_kv * h + 2 + e])
            m = jnp.maximum(jnp.max(s, axis=0, keepdims=True), sink)
            probs.append((jnp.exp2(s - m).astype(BF16), jnp.exp2(sink - m)))
        return probs

    def value_stage(j, probs):
        rows = slice(j * WINDOW, (j + 1) * WINDOW)
        halves = []
        for (h, e), (pb, p_sink) in zip(units, probs):
            vband = jnp.concatenate([vall[h * HEAD_DIM:(h + 1) * HEAD_DIM, j * WINDOW:j * WINDOW + nband],
                                     ones_band], axis=0)
            vmet = jnp.concatenate([vmeta[h * HEAD_DIM:(h + 1) * HEAD_DIM, :], ones_meta], axis=0)
            p_meta = jnp.concatenate([pb[nband:], p_pad], axis=0)
            o = _dot(vband, pb[:nband]) + _dot(vmet, p_meta)
            denom = o[HEAD_DIM:HEAD_DIM + 1] + p_sink
            halves.append(o[:HEAD_DIM] * (1.0 / denom))
        for h in range(N_KV_HEADS):
            o_t = jnp.concatenate(halves[2 * h:2 * h + 2], axis=0)
            for pp in range(2):
                o_ref[rows, (2 * h + pp) * LANES:(2 * h + pp + 1) * LANES] = (
                    o_t[:, pp * LANES:(pp + 1) * LANES].T.astype(BF16))

    scores, probs = {}, {}
    for t in range(nblk + 2):
        if t < nblk:
            scores[t] = score_stage(t)
        if 0 <= t - 1 < nblk:
            probs[t - 1] = softmax_stage(scores.pop(t - 1))
        if 0 <= t - 2 < nblk:
            value_stage(t - 2, probs.pop(t - 2))


def _attn(q, k4, vt, k4_meta, vt_meta, sink, *, tq):
    B, S, _ = q.shape
    r = tq // WINDOW
    last = S // WINDOW - 1
    prev_idx = lambda i: jnp.maximum(i * r - 1, 0)
    next_idx = lambda i: jnp.minimum((i + 1) * r, last)
    cur = lambda w: pl.BlockSpec((None, tq, w), lambda b, i: (b, i, 0))
    kw = 4 * LANES
    return pl.pallas_call(
        functools.partial(_attn_kernel, tq=tq, seq=S),
        grid=(B, S // tq),
        in_specs=[pl.BlockSpec(memory_space=pltpu.SMEM), cur(ATTN_WIDTH),
                  cur(kw),
                  pl.BlockSpec((None, WINDOW, kw), lambda b, i: (b, prev_idx(i), 0)),
                  pl.BlockSpec((None, WINDOW, kw), lambda b, i: (b, next_idx(i), 0)),
                  pl.BlockSpec((None, KV_WIDTH, tq), lambda b, i: (b, 0, i)),
                  pl.BlockSpec((None, KV_WIDTH, WINDOW), lambda b, i: (b, 0, prev_idx(i))),
                  pl.BlockSpec((None, KV_WIDTH, WINDOW), lambda b, i: (b, 0, next_idx(i))),
                  pl.BlockSpec((N_META, kw), lambda b, i: (0, 0)),
                  pl.BlockSpec((KV_WIDTH, LANES), lambda b, i: (0, 0))],
        out_specs=cur(ATTN_WIDTH),
        out_shape=jax.ShapeDtypeStruct((B, S, ATTN_WIDTH), BF16),
        compiler_params=pltpu.CompilerParams(
            dimension_semantics=("parallel", "parallel"), vmem_limit_bytes=VMEM_LIMIT_BYTES),
        name="attn",
    )(sink, q, k4, k4, k4, vt, vt, vt, k4_meta, vt_meta)


def _factor(L):
    n1 = 16
    while L % (2 * n1) == 0:
        n1 *= 2
    rest = L // n1
    for f in (3, 5, 7, 9):
        while rest % f == 0 and n1 * f <= rest // f * 4:
            n1, rest = n1 * f, rest // f
    return n1, rest


def _round_up(x, m):
    return (x + m - 1) // m * m


def _odd_tiles(rows):
    p = _round_up(rows, SUBLANES)
    return p if (p // SUBLANES) % 2 else p + SUBLANES


def _fmix_consts(S):
    L = S + N_META
    N1, N2 = _factor(L)
    N2p = _round_up(N2, 2 * SUBLANES)
    dims = dict(L=L, N1=N1, N2=N2, N2p=N2p, P=_odd_tiles(2 * N1), Q=_odd_tiles(N2p),
                Lp=_round_up(L + N2p - N2, SUBLANES))
    k1 = np.arange(N1, dtype=np.int64)
    ang1 = 2.0 * np.pi * ((k1[:, None] * k1[None, :]) % N1) / N1
    c1, s1 = np.cos(ang1) / np.sqrt(N1), np.sin(ang1) / np.sqrt(N1)
    f1 = np.block([[c1, s1], [-s1, c1]]).astype(np.float32)
    k = k1[:, None, None] + N1 * np.arange(N2p, dtype=np.int64)[None, :, None]
    n2 = np.arange(N2p, dtype=np.int64)[None, None, :]
    ang3 = 2.0 * np.pi * (((n2 + N_META) * k) % L) / L
    valid = ((np.arange(N2p) < N2)[None, :, None] & (np.arange(N2p) < N2)[None, None, :])
    c3 = np.where(valid, np.cos(ang3), 0.0) / np.sqrt(N2)
    s3 = np.where(valid, np.sin(ang3), 0.0) / np.sqrt(N2)
    m3 = np.concatenate([c3, s3], axis=2).astype(np.float32)
    c = np.arange(FOURIER_GROUP_WIDTH, dtype=np.int64)
    angc = 2.0 * np.pi * ((c[:, None] * c[None, :]) % FOURIER_GROUP_WIDTH) / FOURIER_GROUP_WIDTH
    cd = (np.concatenate([np.cos(angc), -np.sin(angc)], axis=1)
          / np.sqrt(FOURIER_GROUP_WIDTH)).astype(np.float32)
    return dims, f1, m3, cd


def _fmix_kernel(u_ref, um_ref, cd_ref, f1_ref, m3_ref, o_ref, *scratch,
                 S, L, Lp, N1, N2, N2p, P, Q, G, NB, U1, U2, chunk):
    W = FOURIER_GROUP_WIDTH
    xr, xi, a_s, y_s = (scratch[i * G:(i + 1) * G] for i in range(4))
    cd = cd_ref[...]
    lanes = lambda g: slice(g * W, (g + 1) * W)

    per_iter = 4 // G
    def p0(c, carry):
        for uu in range(per_iter):
            r0 = pl.multiple_of((c * per_iter + uu) * chunk, chunk)
            for g in range(G):
                v = _dot(u_ref[pl.ds(r0, chunk), lanes(g)], cd)
                xr[g][pl.ds(r0, chunk), :] = v[:, :W]
                xi[g][pl.ds(r0, chunk), :] = v[:, W:]
        return carry

    lax.fori_loop(0, S // (chunk * per_iter), p0, 0)
    for g in range(G):
        vm = _dot(um_ref[:, lanes(g)], cd)
        xr[g][S:L, :] = vm[:, :W]
        xi[g][S:L, :] = vm[:, W:]
        xr[g][L:Lp, :] = jnp.zeros((Lp - L, W), F32)
        xi[g][L:Lp, :] = jnp.zeros((Lp - L, W), F32)

    def p1(t, carry):
        work = []
        for uu in range(U1):
            n2s = [(t * U1 + uu) * NB + j for j in range(NB)]
            cols = [jnp.concatenate([xr[g][pl.ds(n2, N1, stride=N2), :],
                                     xi[g][pl.ds(n2, N1, stride=N2), :]], axis=0)
                    for n2 in n2s for g in range(G)]
            work.append((n2s, _dot(f1_ref[...], jnp.concatenate(cols, axis=1).astype(BF16))))
        for n2s, a in work:
            for j, n2 in enumerate(n2s):
                for g in range(G):
                    a_s[g][pl.ds(pl.multiple_of(n2 * P, SUBLANES), 2 * N1), :] = a[:, lanes(j * G + g)]
        return carry

    lax.fori_loop(0, N2p // (NB * U1), p1, 0)

    def p2(t, carry):
        work = []
        for uu in range(U2):
            k1 = t * U2 + uu
            b = jnp.concatenate(
                [jnp.concatenate([a_s[g][pl.ds(k1, N2p, stride=P), :],
                                  a_s[g][pl.ds(N1 + k1, N2p, stride=P), :]], axis=0) for g in range(G)], axis=1)
            work.append((k1, _dot(m3_ref[k1], b.astype(BF16))))
        for k1, z in work:
            for g in range(G):
                y_s[g][pl.ds(pl.multiple_of(k1 * Q, SUBLANES), N2p), :] = z[:, lanes(g)]
        return carry

    lax.fori_loop(0, N1 // U2, p2, 0)

    for g in range(G):
        o_ref[0:N1 - N_META, lanes(g)] = y_s[g][pl.ds(0, N1, stride=Q), :][N_META:].astype(BF16)

    def p3(t, carry):
        for uu in range(2):
            k2 = 2 * t + 1 + uu
            r0 = pl.multiple_of(k2 * N1 - N_META, 2 * SUBLANES)
            for g in range(G):
                o_ref[pl.ds(r0, N1), lanes(g)] = y_s[g][pl.ds(k2, N1, stride=Q), :].astype(BF16)
        return carry

    lax.fori_loop(0, (N2 - 1) // 2, p3, 0)


def _fmix(u, u_meta, *, groups, u1, u2, chunk=512):
    B, S, _ = u.shape
    dims, f1, m3, cd = _fmix_consts(S)
    Lp, N1, N2p, P, Q = dims["Lp"], dims["N1"], dims["N2p"], dims["P"], dims["Q"]
    W, G = FOURIER_GROUP_WIDTH, groups
    nb = (2 * LANES * 2) // (G * W)
    const = lambda shape: pl.BlockSpec(shape, lambda b, g: (0,) * len(shape))
    scratch = ([pltpu.VMEM((Lp, W), F32)] * (2 * G) + [pltpu.VMEM((N2p * P, W), F32)] * G
               + [pltpu.VMEM((N1 * Q, W), F32)] * G)
    return pl.pallas_call(
        functools.partial(_fmix_kernel, S=S, chunk=chunk, G=G, NB=nb, U1=u1, U2=u2, **dims),
        grid=(B, N_FOURIER_GROUPS // G),
        in_specs=[pl.BlockSpec((None, S, G * W), lambda b, g: (b, 0, g)),
                  pl.BlockSpec((N_META, G * W), lambda b, g: (0, g)),
                  const(cd.shape), const(f1.shape), const(m3.shape)],
        out_specs=pl.BlockSpec((None, S, G * W), lambda b, g: (b, 0, g)),
        out_shape=jax.ShapeDtypeStruct((B, S, FOURIER_WIDTH), BF16),
        scratch_shapes=scratch,
        compiler_params=pltpu.CompilerParams(
            dimension_semantics=("parallel", "parallel"), vmem_limit_bytes=VMEM_LIMIT_BYTES),
        name="fmix",
    )(u, u_meta, jnp.asarray(cd).astype(BF16), jnp.asarray(f1).astype(BF16), jnp.asarray(m3).astype(BF16))


def _post_kernel(x_ref, a_ref, f_ref, gate_ref, wao_ref, wf_ref, wout_ref, g2_ref, wup_ref, wdown_ref,
                 gfin_ref, y_ref, *, ff_chunk, sub):
    tm = x_ref.shape[0]
    blocks = [slice(r, r + sub) for r in range(0, tm, sub)]

    def mixer(rows):
        a = _dot(a_ref[rows, :], wao_ref[...])
        f = _dot(f_ref[rows, :], wf_ref[...])
        merged = (gate_ref[rows, :D_MODEL].astype(F32) * a + gate_ref[rows, D_MODEL:].astype(F32) * f).astype(BF16)
        h = x_ref[rows, :] + _dot(merged, wout_ref[...])
        return h, _rms(h, g2_ref[...]).astype(BF16)

    def mlp_chunk(state, c):
        h, n = state
        t = _dot(n, wup_ref[:, c * ff_chunk:(c + 1) * ff_chunk])
        r = jnp.square(jnp.maximum(t, 0.0)).astype(BF16)
        return h + _dot(r, wdown_ref[c * ff_chunk:(c + 1) * ff_chunk, :]), n

    states = [mixer(rows) for rows in blocks]
    for c in range(D_FF // ff_chunk):
        states = [mlp_chunk(s, c) for s in states]
    for rows, (h, _) in zip(blocks, states):
        y_ref[rows, :] = _rms(h, gfin_ref[...])


def _post(x, a, f, gates, w_ao, w_f, w_out, g2, w_up, w_down, g_fin, *, tm, sub=256, ff_chunk=1024):
    B, S, _ = x.shape
    tok = lambda w: pl.BlockSpec((None, tm, w), lambda b, i: (b, i, 0))
    const = lambda shape: pl.BlockSpec(shape, lambda b, i: (0,) * len(shape), pipeline_mode=pl.Buffered(1))
    return pl.pallas_call(
        functools.partial(_post_kernel, ff_chunk=ff_chunk, sub=sub),
        grid=(B, S // tm),
        in_specs=[tok(D_MODEL), tok(ATTN_WIDTH), tok(FOURIER_WIDTH), tok(N_BRANCHES * D_MODEL),
                  const(w_ao.shape), const(w_f.shape), const(w_out.shape), const((1, D_MODEL)),
                  const(w_up.shape), const(w_down.shape), const((1, D_MODEL))],
        out_specs=tok(D_MODEL),
        out_shape=jax.ShapeDtypeStruct((B, S, D_MODEL), F32),
        compiler_params=pltpu.CompilerParams(
            dimension_semantics=("parallel", "parallel"), vmem_limit_bytes=VMEM_LIMIT_BYTES),
        name="post",
    )(x, a, f, gates, w_ao, w_f, w_out, g2, w_up, w_down, g_fin)


def _rope_table(first_pos, n_pos):
    half = ROT_DIM // 2
    inv_freq = ROPE_THETA ** (-jnp.arange(half, dtype=F32) / half)
    pos = jnp.arange(first_pos, first_pos + n_pos, dtype=jnp.int32).astype(F32)
    ang = pos[:, None] * inv_freq[None, :]
    cos, sin = jnp.cos(ang), jnp.sin(ang)
    rest = HEAD_DIM - ROT_DIM
    head = lambda a, b, fill: jnp.concatenate([a, b, jnp.full((n_pos, rest), fill, F32)], axis=1)
    zero = jnp.zeros_like(sin)
    per_head = (head(cos, cos, 1.0), head(zero, sin, 0.0), head(-sin, zero, 0.0))
    return jnp.concatenate([jnp.tile(t, (1, LANES // HEAD_DIM)) for t in per_head], axis=1)


def _trunk(x, meta_parts, table, wts, *, tm_proj, tm, tq, fmix):
    k4_m, vt_m, u_m = meta_parts
    q, k4, vt, u, gates = _proj(x, table, wts["norm_mix_g"], wts["w_in"], wts["b_gate"], tm=tm_proj,
                                transpose_v=True)
    a = _attn(q, k4, vt, k4_m, vt_m, wts["attn_sink"], tq=tq)
    f = _fmix(u, u_m, **fmix)
    return _post(x, a, f, gates, wts["w_attn_out"], wts["w_fourier"], wts["w_out"], wts["norm_mlp_g"],
                 wts["w_mlp_up"], wts["w_mlp_down"], wts["norm_final_g"], tm=tm)


def kernel(x_prompt, x_sample, meta_tokens, norm_mix_g, w_in, b_gate, attn_sink, w_attn_out, w_fourier, w_out,
           norm_mlp_g, w_mlp_up, w_mlp_down, norm_final_g):
    assert w_in.shape[0] == 1, "single-layer trunk: meta-token outputs are never consumed"
    wts = dict(
        norm_mix_g=norm_mix_g[0][None, :], w_in=w_in[0].astype(BF16), b_gate=b_gate[0][None, :],
        attn_sink=attn_sink[0], w_attn_out=w_attn_out[0].astype(BF16), w_fourier=w_fourier[0].astype(BF16),
        w_out=w_out[0].astype(BF16), norm_mlp_g=norm_mlp_g[0][None, :], w_mlp_up=w_mlp_up[0].astype(BF16),
        w_mlp_down=w_mlp_down[0].astype(BF16), norm_final_g=norm_final_g[None, :])
    table = _rope_table(N_META, max(x_prompt.shape[1], x_sample.shape[1]))
    _, k4_m, v_m, u_m, _ = _proj(meta_tokens[None], _rope_table(0, N_META),
                                 wts["norm_mix_g"], wts["w_in"], wts["b_gate"], tm=N_META, transpose_v=False)
    vt_m = jnp.pad(v_m[0].T, ((0, 0), (0, LANES - N_META)))
    meta_parts = (k4_m[0], vt_m, u_m[0])
    tiles = dict(tm_proj=1024, tm=512, tq=1024)
    y_prompt = _trunk(x_prompt, meta_parts, table, wts, fmix=dict(groups=4, u1=12, u2=24), **tiles)
    y_sample = _trunk(x_sample, meta_parts, table, wts, fmix=dict(groups=1, u1=4, u2=24), **tiles)
    return (y_prompt, y_sample)
```

```python
import functools

import numpy as np
import jax
import jax.numpy as jnp
from jax import lax
from jax.experimental import pallas as pl
from jax.experimental.pallas import tpu as pltpu

D_MODEL = 1024
HEAD_DIM = 64
N_Q_HEADS = 8
N_KV_HEADS = 2
ATTN_WIDTH = N_Q_HEADS * HEAD_DIM
KV_WIDTH = N_KV_HEADS * HEAD_DIM
WINDOW = 128
ROPE_THETA = 500000.0
ROT_DIM = HEAD_DIM // 4
N_FOURIER_GROUPS = 4
FOURIER_GROUP_WIDTH = 128
FOURIER_WIDTH = N_FOURIER_GROUPS * FOURIER_GROUP_WIDTH
N_BRANCHES = 2
IN_WIDTH = ATTN_WIDTH + 2 * KV_WIDTH + FOURIER_WIDTH + N_BRANCHES * D_MODEL
D_FF = 4 * D_MODEL
N_META = 16
RMS_EPS = 1e-6
NEG_INF = -1e30
LOG2E = 1.4426950408889634

LANES = 128
SUBLANES = 8
VMEM_LIMIT_BYTES = 56 * 1024 * 1024

_Q0, _K0, _V0, _U0, _G0 = 0, ATTN_WIDTH, ATTN_WIDTH + KV_WIDTH, ATTN_WIDTH + 2 * KV_WIDTH, \
    ATTN_WIDTH + 2 * KV_WIDTH + FOURIER_WIDTH

BF16 = jnp.bfloat16
F32 = jnp.float32


def _dot(a, b):
    return jnp.dot(a, b, preferred_element_type=F32)


def _dot_nt(a, b):
    return lax.dot_general(a, b, (((1,), (1,)), ((), ())), preferred_element_type=F32)


def _rms(x, g):
    return x * lax.rsqrt(jnp.mean(x * x, axis=-1, keepdims=True) + RMS_EPS) * g


def _proj_kernel(x_ref, tab_ref, g_ref, w_ref, b_ref,
                 q_ref, k4_ref, v_ref, u_ref, gate_ref, *, transpose_v, sub):
    tm = x_ref.shape[0]
    blocks = [slice(r, r + sub) for r in range(0, tm, sub)]
    normed = {rows.start: _rms(x_ref[rows, :], g_ref[...]).astype(BF16) for rows in blocks}
    project = lambda rows, lo, hi: _dot(normed[rows.start], w_ref[:, lo:hi])

    def rope(z, rows):
        cos_t, sin_a, sin_b = (tab_ref[rows, c * LANES:(c + 1) * LANES] for c in range(3))
        return (z * cos_t + pltpu.roll(z, ROT_DIM // 2, 1) * sin_a
                + pltpu.roll(z, LANES - ROT_DIM // 2, 1) * sin_b)

    for rows in blocks:
        zq = project(rows, _Q0, _K0)
        for c in range(ATTN_WIDTH // LANES):
            q_ref[rows, c * LANES:(c + 1) * LANES] = (
                rope(zq[:, c * LANES:(c + 1) * LANES], rows) * (LOG2E * HEAD_DIM ** -0.5)).astype(BF16)
    for rows in blocks:
        zkv = project(rows, _K0, _U0)
        zk = rope(zkv[:, :KV_WIDTH], rows)
        zv = zkv[:, KV_WIDTH:]
        zks = pltpu.roll(zk, HEAD_DIM, 1)
        low = lax.broadcasted_iota(jnp.int32, zk.shape, 1) < HEAD_DIM
        for s, (keep_low, src) in enumerate(((True, zk), (False, zks), (True, zks), (False, zk))):
            k4_ref[rows, s * LANES:(s + 1) * LANES] = jnp.where(low == keep_low, src, 0.0).astype(BF16)
        if transpose_v:
            v_ref[:, rows] = zv.T.astype(BF16)
        else:
            v_ref[rows, :] = zv.astype(BF16)
    for rows in blocks:
        u_ref[rows, :] = project(rows, _U0, _G0).astype(BF16)
    for c in range(N_BRANCHES):
        lo = _G0 + c * D_MODEL
        for rows in blocks:
            g = project(rows, lo, lo + D_MODEL) + b_ref[:, c * D_MODEL:(c + 1) * D_MODEL]
            gate_ref[rows, c * D_MODEL:(c + 1) * D_MODEL] = jax.nn.sigmoid(g).astype(BF16)


def _proj(x, table, g, w_in, b_gate, *, tm, transpose_v):
    B, S, _ = x.shape
    tok = lambda w: pl.BlockSpec((None, tm, w), lambda b, i: (b, i, 0))
    tab = pl.BlockSpec((tm, 3 * LANES), lambda b, i: (i, 0))
    const = lambda shape: pl.BlockSpec(shape, lambda b, i: (0,) * len(shape))
    out_w = (ATTN_WIDTH, 4 * LANES, KV_WIDTH, FOURIER_WIDTH, N_BRANCHES * D_MODEL)
    out_specs = [tok(w) for w in out_w]
    out_shape = [jax.ShapeDtypeStruct((B, S, w), BF16) for w in out_w]
    if transpose_v:
        out_specs[2] = pl.BlockSpec((None, KV_WIDTH, tm), lambda b, i: (b, 0, i))
        out_shape[2] = jax.ShapeDtypeStruct((B, KV_WIDTH, S), BF16)
    return pl.pallas_call(
        functools.partial(_proj_kernel, transpose_v=transpose_v, sub=min(tm, 512)),
        grid=(B, S // tm),
        in_specs=[tok(D_MODEL), tab, const((1, D_MODEL)),
                  const((D_MODEL, IN_WIDTH)), const((1, N_BRANCHES * D_MODEL))],
        out_specs=out_specs,
        out_shape=out_shape,
        compiler_params=pltpu.CompilerParams(
            dimension_semantics=("parallel", "parallel"), vmem_limit_bytes=VMEM_LIMIT_BYTES),
        name="proj",
    )(x, table, g, w_in, b_gate)


def _attn_slots(i, sink_ref, q_ref, kc_ref, kp_ref, kn_ref, vc_ref, vp_ref, vn_ref, km_ref, vm_ref,
                o_ref, *, tq, seq):
    nblk = tq // WINDOW
    nband = 3 * WINDOW
    nkey = nband + N_META
    kall = jnp.concatenate([kp_ref[...], kc_ref[...], kn_ref[...]], axis=0)
    vall = jnp.concatenate([vp_ref[...], vc_ref[...], vn_ref[...]], axis=1)
    kmeta, vmeta = km_ref[...], vm_ref[...]
    key = lax.broadcasted_iota(jnp.int32, (WINDOW, 2 * LANES), 0)
    qry = jnp.bitwise_and(lax.broadcasted_iota(jnp.int32, (WINDOW, 2 * LANES), 1), WINDOW - 1)
    band_prev = jnp.where(key >= qry, 0.0, NEG_INF).astype(F32)
    band_next = jnp.where(key <= qry, 0.0, NEG_INF).astype(F32)
    first_pair = lax.broadcasted_iota(jnp.int32, (1, 2 * LANES), 1) < LANES
    p_pad = jnp.zeros((LANES - N_META, 2 * LANES), BF16)
    ones_band = jnp.ones((2 * SUBLANES, nband), BF16)
    ones_meta = jnp.ones((2 * SUBLANES, LANES), BF16)
    heads_per_kv = N_Q_HEADS // N_KV_HEADS

    units = [(h, e) for h in range(N_KV_HEADS) for e in range(2)]

    def score_stage(j):
        rows = slice(j * WINDOW, (j + 1) * WINDOW)
        blk0 = i * tq + j * WINDOW
        bias_prev = band_prev + jnp.where(blk0 >= WINDOW, 0.0, NEG_INF).astype(F32)
        bias_next = band_next + jnp.where(blk0 + WINDOW < seq, 0.0, NEG_INF).astype(F32)
        scores = []
        for h, e in units:
            q2 = jnp.concatenate([q_ref[rows, (2 * h) * LANES:(2 * h + 1) * LANES],
                                  q_ref[rows, (2 * h + 1) * LANES:(2 * h + 2) * LANES]], axis=0)
            sl = slice((2 * h + e) * LANES, (2 * h + e + 1) * LANES)
            keys = jnp.concatenate([kall[j * WINDOW:j * WINDOW + nband, sl], kmeta[:, sl]], axis=0)
            s = _dot_nt(keys, q2)
            scores.append(jnp.concatenate([s[:WINDOW] + bias_prev, s[WINDOW:2 * WINDOW],
                                           s[2 * WINDOW:nband] + bias_next, s[nband:]], axis=0))
        return scores

    def softmax_stage(scores):
        probs = []
        for (h, e), s in zip(units, scores):
            sink = LOG2E * jnp.where(first_pair, sink_ref[heads_per_kv * h + e],
                                     sink_ref[heads_per_kv * h + 2 + e])
            m = jnp.maximum(jnp.max(s, axis=0, keepdims=True), sink)
            probs.append((jnp.exp2(s - m).astype(BF16), jnp.exp2(sink - m)))
        return probs

    def value_stage(j, probs):
        rows = slice(j * WINDOW, (j + 1) * WINDOW)
        halves = []
        for (h, e), (pb, p_sink) in zip(units, probs):
            vband = jnp.concatenate([vall[h * HEAD_DIM:(h + 1) * HEAD_DIM, j * WINDOW:j * WINDOW + nband],
                                     ones_band], axis=0)
            vmet = jnp.concatenate([vmeta[h * HEAD_DIM:(h + 1) * HEAD_DIM, :], ones_meta], axis=0)
            p_meta = jnp.concatenate([pb[nband:], p_pad], axis=0)
            o = _dot(vband, pb[:nband]) + _dot(vmet, p_meta)
            denom = o[HEAD_DIM:HEAD_DIM + 1] + p_sink
            halves.append(o[:HEAD_DIM] * (1.0 / denom))
        for h in range(N_KV_HEADS):
            o_t = jnp.concatenate(halves[2 * h:2 * h + 2], axis=0)
            for pp in range(2):
                o_ref[rows, (2 * h + pp) * LANES:(2 * h + pp + 1) * LANES] = (
                    o_t[:, pp * LANES:(pp + 1) * LANES].T.astype(BF16))

    scores, probs = {}, {}

    def slot(t):
        if t < nblk:
            scores[t] = score_stage(t)
        if 0 <= t - 1 < nblk:
            probs[t - 1] = softmax_stage(scores.pop(t - 1))
        if 0 <= t - 2 < nblk:
            value_stage(t - 2, probs.pop(t - 2))

    return [functools.partial(slot, t) for t in range(nblk + 2)]


def _factor(L):
    n1 = 16
    while L % (2 * n1) == 0:
        n1 *= 2
    rest = L // n1
    for f in (3, 5, 7, 9):
        while rest % f == 0 and n1 * f <= rest // f * 4:
            n1, rest = n1 * f, rest // f
    return n1, rest


def _round_up(x, m):
    return (x + m - 1) // m * m


def _odd_tiles(rows):
    p = _round_up(rows, SUBLANES)
    return p if (p // SUBLANES) % 2 else p + SUBLANES


def _fmix_consts(S):
    L = S + N_META
    N1, N2 = _factor(L)
    N2p = _round_up(N2, 2 * SUBLANES)
    dims = dict(L=L, N1=N1, N2=N2, N2p=N2p, P=_odd_tiles(2 * N1), Q=_odd_tiles(N2p),
                Lp=_round_up(L + N2p - N2, SUBLANES))
    k1 = np.arange(N1, dtype=np.int64)
    ang1 = 2.0 * np.pi * ((k1[:, None] * k1[None, :]) % N1) / N1
    c1, s1 = np.cos(ang1) / np.sqrt(N1), np.sin(ang1) / np.sqrt(N1)
    f1 = np.block([[c1, s1], [-s1, c1]]).astype(np.float32)
    k = k1[:, None, None] + N1 * np.arange(N2p, dtype=np.int64)[None, :, None]
    n2 = np.arange(N2p, dtype=np.int64)[None, None, :]
    ang3 = 2.0 * np.pi * (((n2 + N_META) * k) % L) / L
    valid = ((np.arange(N2p) < N2)[None, :, None] & (np.arange(N2p) < N2)[None, None, :])
    c3 = np.where(valid, np.cos(ang3), 0.0) / np.sqrt(N2)
    s3 = np.where(valid, np.sin(ang3), 0.0) / np.sqrt(N2)
    m3 = np.concatenate([c3, s3], axis=2).astype(np.float32)
    c = np.arange(FOURIER_GROUP_WIDTH, dtype=np.int64)
    angc = 2.0 * np.pi * ((c[:, None] * c[None, :]) % FOURIER_GROUP_WIDTH) / FOURIER_GROUP_WIDTH
    cd = (np.concatenate([np.cos(angc), -np.sin(angc)], axis=1)
          / np.sqrt(FOURIER_GROUP_WIDTH)).astype(np.float32)
    return dims, f1, m3, cd


def _fmix_kernel(u_ref, um_ref, cd_ref, f1_ref, m3_ref, o_ref, *scratch,
                 S, L, Lp, N1, N2, N2p, P, Q, G, NB, U1, U2, chunk):
    W = FOURIER_GROUP_WIDTH
    xr, xi, a_s, y_s = (scratch[i * G:(i + 1) * G] for i in range(4))
    cd = cd_ref[...]
    lanes = lambda g: slice(g * W, (g + 1) * W)

    per_iter = 4 // G
    def p0(c, carry):
        for uu in range(per_iter):
            r0 = pl.multiple_of((c * per_iter + uu) * chunk, chunk)
            for g in range(G):
                v = _dot(u_ref[pl.ds(r0, chunk), lanes(g)], cd)
                xr[g][pl.ds(r0, chunk), :] = v[:, :W]
                xi[g][pl.ds(r0, chunk), :] = v[:, W:]
        return carry

    lax.fori_loop(0, S // (chunk * per_iter), p0, 0)
    for g in range(G):
        vm = _dot(um_ref[:, lanes(g)], cd)
        xr[g][S:L, :] = vm[:, :W]
        xi[g][S:L, :] = vm[:, W:]
        xr[g][L:Lp, :] = jnp.zeros((Lp - L, W), F32)
        xi[g][L:Lp, :] = jnp.zeros((Lp - L, W), F32)

    def p1(t, carry):
        work = []
        for uu in range(U1):
            n2s = [(t * U1 + uu) * NB + j for j in range(NB)]
            cols = [jnp.concatenate([xr[g][pl.ds(n2, N1, stride=N2), :],
                                     xi[g][pl.ds(n2, N1, stride=N2), :]], axis=0)
                    for n2 in n2s for g in range(G)]
            work.append((n2s, _dot(f1_ref[...], jnp.concatenate(cols, axis=1).astype(BF16))))
        for n2s, a in work:
            for j, n2 in enumerate(n2s):
                for g in range(G):
                    a_s[g][pl.ds(pl.multiple_of(n2 * P, SUBLANES), 2 * N1), :] = a[:, lanes(j * G + g)]
        return carry

    lax.fori_loop(0, N2p // (NB * U1), p1, 0)

    def p2(t, carry):
        work = []
        for uu in range(U2):
            k1 = t * U2 + uu
            b = jnp.concatenate(
                [jnp.concatenate([a_s[g][pl.ds(k1, N2p, stride=P), :],
                                  a_s[g][pl.ds(N1 + k1, N2p, stride=P), :]], axis=0) for g in range(G)], axis=1)
            work.append((k1, _dot(m3_ref[k1], b.astype(BF16))))
        for k1, z in work:
            for g in range(G):
                y_s[g][pl.ds(pl.multiple_of(k1 * Q, SUBLANES), N2p), :] = z[:, lanes(g)]
        return carry

    lax.fori_loop(0, N1 // U2, p2, 0)

    for g in range(G):
        o_ref[0:N1 - N_META, lanes(g)] = y_s[g][pl.ds(0, N1, stride=Q), :][N_META:].astype(BF16)

    def p3(t, carry):
        for uu in range(2):
            k2 = 2 * t + 1 + uu
            r0 = pl.multiple_of(k2 * N1 - N_META, 2 * SUBLANES)
            for g in range(G):
                o_ref[pl.ds(r0, N1), lanes(g)] = y_s[g][pl.ds(k2, N1, stride=Q), :].astype(BF16)
        return carry

    lax.fori_loop(0, (N2 - 1) // 2, p3, 0)


def _fmix(u, u_meta, *, groups, u1, u2, chunk=512):
    B, S, _ = u.shape
    dims, f1, m3, cd = _fmix_consts(S)
    Lp, N1, N2p, P, Q = dims["Lp"], dims["N1"], dims["N2p"], dims["P"], dims["Q"]
    W, G = FOURIER_GROUP_WIDTH, groups
    nb = (2 * LANES * 2) // (G * W)
    const = lambda shape: pl.BlockSpec(shape, lambda b, g: (0,) * len(shape))
    scratch = ([pltpu.VMEM((Lp, W), F32)] * (2 * G) + [pltpu.VMEM((N2p * P, W), F32)] * G
               + [pltpu.VMEM((N1 * Q, W), F32)] * G)
    return pl.pallas_call(
        functools.partial(_fmix_kernel, S=S, chunk=chunk, G=G, NB=nb, U1=u1, U2=u2, **dims),
        grid=(B, N_FOURIER_GROUPS // G),
        in_specs=[pl.BlockSpec((None, S, G * W), lambda b, g: (b, 0, g)),
                  pl.BlockSpec((N_META, G * W), lambda b, g: (0, g)),
                  const(cd.shape), const(f1.shape), const(m3.shape)],
        out_specs=pl.BlockSpec((None, S, G * W), lambda b, g: (b, 0, g)),
        out_shape=jax.ShapeDtypeStruct((B, S, FOURIER_WIDTH), BF16),
        scratch_shapes=scratch,
        compiler_params=pltpu.CompilerParams(
            dimension_semantics=("parallel", "parallel"), vmem_limit_bytes=VMEM_LIMIT_BYTES),
        name="fmix",
    )(u, u_meta, jnp.asarray(cd).astype(BF16), jnp.asarray(f1).astype(BF16), jnp.asarray(m3).astype(BF16))


def _post_stages(x_ref, a_ref, f_ref, gate_ref, wao_ref, wf_ref, wout_ref, g2_ref, wup_ref, wdown_ref,
                 gfin_ref, y_ref, *, ff_chunk, sub):
    tm = x_ref.shape[0]
    blocks = [slice(r, r + sub) for r in range(0, tm, sub)]
    state = {}

    def mixer(rows):
        a = _dot(a_ref[rows, :], wao_ref[...])
        f = _dot(f_ref[rows, :], wf_ref[...])
        merged = (gate_ref[rows, :D_MODEL].astype(F32) * a + gate_ref[rows, D_MODEL:].astype(F32) * f).astype(BF16)
        h = x_ref[rows, :] + _dot(merged, wout_ref[...])
        state[rows.start] = (h, _rms(h, g2_ref[...]).astype(BF16))

    def mlp_chunk(c):
        for rows in blocks:
            h, n = state[rows.start]
            t = _dot(n, wup_ref[:, c * ff_chunk:(c + 1) * ff_chunk])
            r = jnp.square(jnp.maximum(t, 0.0)).astype(BF16)
            state[rows.start] = (h + _dot(r, wdown_ref[c * ff_chunk:(c + 1) * ff_chunk, :]), n)

    def final():
        for rows in blocks:
            y_ref[rows, :] = _rms(state[rows.start][0], gfin_ref[...])

    return ([functools.partial(mixer, rows) for rows in blocks]
            + [functools.partial(mlp_chunk, c) for c in range(D_FF // ff_chunk)] + [final])


def _attn_post_kernel(sink_ref, q_ref, kc_ref, kp_ref, kn_ref, vc_ref, vp_ref, vn_ref, km_ref, vm_ref,
                      x_ref, f_ref, gate_ref, wao_ref, wf_ref, wout_ref, g2_ref, wup_ref, wdown_ref, gfin_ref,
                      y_ref, a_scr, *, seq, tiles_per_seq, n_tiles, ff_chunk, sub):
    s = pl.program_id(0)
    tm = x_ref.shape[0]
    attn_tile = lax.rem(jnp.minimum(s, n_tiles - 1), tiles_per_seq)
    attn = functools.partial(_attn_slots, attn_tile, sink_ref, q_ref, kc_ref, kp_ref, kn_ref, vc_ref, vp_ref,
                             vn_ref, km_ref, vm_ref, a_scr, tq=tm, seq=seq)

    @pl.when(s == 0)
    def _():
        for slot in attn():
            slot()

    @pl.when(s > 0)
    def _():
        post = _post_stages(x_ref, a_scr, f_ref, gate_ref, wao_ref, wf_ref, wout_ref, g2_ref, wup_ref, wdown_ref,
                            gfin_ref, y_ref, ff_chunk=ff_chunk, sub=sub)
        slots = attn()
        n_lead = len(post) - len(slots)
        assert n_lead >= 0
        order = post[:n_lead]
        for p, a in zip(post[n_lead:], slots):
            order += [a, p]
        for stage in order:
            stage()


def _attn_post(x, q, k4, vt, k4_meta, vt_meta, sink, f, gates, w_ao, w_f, w_out, g2, w_up, w_down, g_fin,
               *, tm, sub=256, ff_chunk=1024):
    B, S, _ = x.shape
    nt = S // tm
    n_tiles = B * nt
    r = tm // WINDOW
    last = S // WINDOW - 1
    kw = 4 * LANES
    post_tile = lambda s: jnp.maximum(s - 1, 0)
    attn_tile = lambda s: jnp.minimum(s, n_tiles - 1)
    seq_of = lambda t: t // nt
    row_of = lambda t: t % nt
    tok = lambda w: pl.BlockSpec((None, tm, w), lambda s: (seq_of(post_tile(s)), row_of(post_tile(s)), 0))
    cur = lambda w: pl.BlockSpec((None, tm, w), lambda s: (seq_of(attn_tile(s)), row_of(attn_tile(s)), 0))
    prev_blk = lambda s: jnp.maximum(row_of(attn_tile(s)) * r - 1, 0)
    next_blk = lambda s: jnp.minimum((row_of(attn_tile(s)) + 1) * r, last)
    const = lambda shape: pl.BlockSpec(shape, lambda s: (0,) * len(shape), pipeline_mode=pl.Buffered(1))
    return pl.pallas_call(
        functools.partial(_attn_post_kernel, seq=S, tiles_per_seq=nt, n_tiles=n_tiles, ff_chunk=ff_chunk, sub=sub),
        grid=(n_tiles + 1,),
        in_specs=[pl.BlockSpec(memory_space=pltpu.SMEM), cur(ATTN_WIDTH), cur(kw),
                  pl.BlockSpec((None, WINDOW, kw), lambda s: (seq_of(attn_tile(s)), prev_blk(s), 0)),
                  pl.BlockSpec((None, WINDOW, kw), lambda s: (seq_of(attn_tile(s)), next_blk(s), 0)),
                  pl.BlockSpec((None, KV_WIDTH, tm), lambda s: (seq_of(attn_tile(s)), 0, row_of(attn_tile(s)))),
                  pl.BlockSpec((None, KV_WIDTH, WINDOW), lambda s: (seq_of(attn_tile(s)), 0, prev_blk(s))),
                  pl.BlockSpec((None, KV_WIDTH, WINDOW), lambda s: (seq_of(attn_tile(s)), 0, next_blk(s))),
                  pl.BlockSpec((N_META, kw), lambda s: (0, 0)),
                  pl.BlockSpec((KV_WIDTH, LANES), lambda s: (0, 0)),
                  tok(D_MODEL), tok(FOURIER_WIDTH), tok(N_BRANCHES * D_MODEL),
                  const(w_ao.shape), const(w_f.shape), const(w_out.shape), const((1, D_MODEL)),
                  const(w_up.shape), const(w_down.shape), const((1, D_MODEL))],
        out_specs=tok(D_MODEL),
        out_shape=jax.ShapeDtypeStruct((B, S, D_MODEL), F32),
        scratch_shapes=[pltpu.VMEM((tm, ATTN_WIDTH), BF16)],
        compiler_params=pltpu.CompilerParams(
            dimension_semantics=("arbitrary",), vmem_limit_bytes=VMEM_LIMIT_BYTES),
        name="attn_post",
    )(sink, q, k4, k4, k4, vt, vt, vt, k4_meta, vt_meta, x, f, gates, w_ao, w_f, w_out, g2, w_up, w_down, g_fin)


def _rope_table(first_pos, n_pos):
    half = ROT_DIM // 2
    inv_freq = ROPE_THETA ** (-jnp.arange(half, dtype=F32) / half)
    pos = jnp.arange(first_pos, first_pos + n_pos, dtype=jnp.int32).astype(F32)
    ang = pos[:, None] * inv_freq[None, :]
    cos, sin = jnp.cos(ang), jnp.sin(ang)
    rest = HEAD_DIM - ROT_DIM
    head = lambda a, b, fill: jnp.concatenate([a, b, jnp.full((n_pos, rest), fill, F32)], axis=1)
    zero = jnp.zeros_like(sin)
    per_head = (head(cos, cos, 1.0), head(zero, sin, 0.0), head(-sin, zero, 0.0))
    return jnp.concatenate([jnp.tile(t, (1, LANES // HEAD_DIM)) for t in per_head], axis=1)


def _trunk(x, meta_parts, table, wts, *, tm_proj, tm, fmix):
    k4_m, vt_m, u_m = meta_parts
    q, k4, vt, u, gates = _proj(x, table, wts["norm_mix_g"], wts["w_in"], wts["b_gate"], tm=tm_proj,
                                transpose_v=True)
    f = _fmix(u, u_m, **fmix)
    return _attn_post(x, q, k4, vt, k4_m, vt_m, wts["attn_sink"], f, gates, wts["w_attn_out"], wts["w_fourier"],
                      wts["w_out"], wts["norm_mlp_g"], wts["w_mlp_up"], wts["w_mlp_down"], wts["norm_final_g"], tm=tm)


def kernel(x_prompt, x_sample, meta_tokens, norm_mix_g, w_in, b_gate, attn_sink, w_attn_out, w_fourier, w_out,
           norm_mlp_g, w_mlp_up, w_mlp_down, norm_final_g):
    assert w_in.shape[0] == 1, "single-layer trunk: meta-token outputs are never consumed"
    wts = dict(
        norm_mix_g=norm_mix_g[0][None, :], w_in=w_in[0].astype(BF16), b_gate=b_gate[0][None, :],
        attn_sink=attn_sink[0], w_attn_out=w_attn_out[0].astype(BF16), w_fourier=w_fourier[0].astype(BF16),
        w_out=w_out[0].astype(BF16), norm_mlp_g=norm_mlp_g[0][None, :], w_mlp_up=w_mlp_up[0].astype(BF16),
        w_mlp_down=w_mlp_down[0].astype(BF16), norm_final_g=norm_final_g[None, :])
    table = _rope_table(N_META, max(x_prompt.shape[1], x_sample.shape[1]))
    _, k4_m, v_m, u_m, _ = _proj(meta_tokens[None], _rope_table(0, N_META),
                                 wts["norm_mix_g"], wts["w_in"], wts["b_gate"], tm=N_META, transpose_v=False)
    vt_m = jnp.pad(v_m[0].T, ((0, 0), (0, LANES - N_META)))
    meta_parts = (k4_m[0], vt_m, u_m[0])
    tiles = dict(tm_proj=1024, tm=512)
    y_prompt = _trunk(x_prompt, meta_parts, table, wts, fmix=dict(groups=4, u1=12, u2=24), **tiles)
    y_sample = _trunk(x_sample, meta_parts, table, wts, fmix=dict(groups=1, u1=4, u2=24), **tiles)
    return (y_prompt, y_sample)
```

```python
import functools

import numpy as np
import jax
import jax.numpy as jnp
from jax import lax
from jax.experimental import pallas as pl
from jax.experimental.pallas import tpu as pltpu

D_MODEL = 1024
HEAD_DIM = 64
N_Q_HEADS = 8
N_KV_HEADS = 2
ATTN_WIDTH = N_Q_HEADS * HEAD_DIM
KV_WIDTH = N_KV_HEADS * HEAD_DIM
WINDOW = 128
ROPE_THETA = 500000.0
ROT_DIM = HEAD_DIM // 4
N_FOURIER_GROUPS = 4
FOURIER_GROUP_WIDTH = 128
FOURIER_WIDTH = N_FOURIER_GROUPS * FOURIER_GROUP_WIDTH
N_BRANCHES = 2
IN_WIDTH = ATTN_WIDTH + 2 * KV_WIDTH + FOURIER_WIDTH + N_BRANCHES * D_MODEL
D_FF = 4 * D_MODEL
N_META = 16
RMS_EPS = 1e-6
NEG_INF = -1e30
LOG2E = 1.4426950408889634

LANES = 128
SUBLANES = 8
VMEM_LIMIT_BYTES = 56 * 1024 * 1024

_Q0, _K0, _V0, _U0, _G0 = 0, ATTN_WIDTH, ATTN_WIDTH + KV_WIDTH, ATTN_WIDTH + 2 * KV_WIDTH, \
    ATTN_WIDTH + 2 * KV_WIDTH + FOURIER_WIDTH

BF16 = jnp.bfloat16
F32 = jnp.float32


def _dot(a, b):
    return jnp.dot(a, b, preferred_element_type=F32)


def _dot_nt(a, b):
    return lax.dot_general(a, b, (((1,), (1,)), ((), ())), preferred_element_type=F32)


def _rms(x, g):
    return x * lax.rsqrt(jnp.mean(x * x, axis=-1, keepdims=True) + RMS_EPS) * g


def _proj_kernel(x_ref, tab_ref, g_ref, w_ref, b_ref,
                 q_ref, k4_ref, v_ref, u_ref, gate_ref, *, transpose_v, sub):
    tm = x_ref.shape[0]
    blocks = [slice(r, r + sub) for r in range(0, tm, sub)]
    normed = {rows.start: _rms(x_ref[rows, :], g_ref[...]).astype(BF16) for rows in blocks}
    project = lambda rows, lo, hi: _dot(normed[rows.start], w_ref[:, lo:hi])

    def rope(z, rows):
        cos_t, sin_a, sin_b = (tab_ref[rows, c * LANES:(c + 1) * LANES] for c in range(3))
        return (z * cos_t + pltpu.roll(z, ROT_DIM // 2, 1) * sin_a
                + pltpu.roll(z, LANES - ROT_DIM // 2, 1) * sin_b)

    for rows in blocks:
        zq = project(rows, _Q0, _K0)
        for c in range(ATTN_WIDTH // LANES):
            q_ref[rows, c * LANES:(c + 1) * LANES] = (
                rope(zq[:, c * LANES:(c + 1) * LANES], rows) * (LOG2E * HEAD_DIM ** -0.5)).astype(BF16)
    for rows in blocks:
        zkv = project(rows, _K0, _U0)
        zk = rope(zkv[:, :KV_WIDTH], rows)
        zv = zkv[:, KV_WIDTH:]
        zks = pltpu.roll(zk, HEAD_DIM, 1)
        low = lax.broadcasted_iota(jnp.int32, zk.shape, 1) < HEAD_DIM
        for s, (keep_low, src) in enumerate(((True, zk), (False, zks), (True, zks), (False, zk))):
            k4_ref[rows, s * LANES:(s + 1) * LANES] = jnp.where(low == keep_low, src, 0.0).astype(BF16)
        if transpose_v:
            v_ref[:, rows] = zv.T.astype(BF16)
        else:
            v_ref[rows, :] = zv.astype(BF16)
    for rows in blocks:
        u_ref[rows, :] = project(rows, _U0, _G0).astype(BF16)
    for c in range(N_BRANCHES):
        lo = _G0 + c * D_MODEL
        for rows in blocks:
            g = project(rows, lo, lo + D_MODEL) + b_ref[:, c * D_MODEL:(c + 1) * D_MODEL]
            gate_ref[rows, c * D_MODEL:(c + 1) * D_MODEL] = jax.nn.sigmoid(g).astype(BF16)


def _proj(x, table, g, w_in, b_gate, *, tm, transpose_v):
    B, S, _ = x.shape
    tok = lambda w: pl.BlockSpec((None, tm, w), lambda b, i: (b, i, 0))
    tab = pl.BlockSpec((tm, 3 * LANES), lambda b, i: (i, 0))
    const = lambda shape: pl.BlockSpec(shape, lambda b, i: (0,) * len(shape))
    out_w = (ATTN_WIDTH, 4 * LANES, KV_WIDTH, FOURIER_WIDTH, N_BRANCHES * D_MODEL)
    out_specs = [tok(w) for w in out_w]
    out_shape = [jax.ShapeDtypeStruct((B, S, w), BF16) for w in out_w]
    if transpose_v:
        out_specs[2] = pl.BlockSpec((None, KV_WIDTH, tm), lambda b, i: (b, 0, i))
        out_shape[2] = jax.ShapeDtypeStruct((B, KV_WIDTH, S), BF16)
    return pl.pallas_call(
        functools.partial(_proj_kernel, transpose_v=transpose_v, sub=min(tm, 512)),
        grid=(B, S // tm),
        in_specs=[tok(D_MODEL), tab, const((1, D_MODEL)),
                  const((D_MODEL, IN_WIDTH)), const((1, N_BRANCHES * D_MODEL))],
        out_specs=out_specs,
        out_shape=out_shape,
        compiler_params=pltpu.CompilerParams(
            dimension_semantics=("parallel", "parallel"), vmem_limit_bytes=VMEM_LIMIT_BYTES),
        name="proj",
    )(x, table, g, w_in, b_gate)


def _attn_slots(i, sink_ref, q_ref, kc_ref, kp_ref, kn_ref, vc_ref, vp_ref, vn_ref, km_ref, vm_ref,
                o_ref, *, tq, seq):
    nblk = tq // WINDOW
    nband = 3 * WINDOW
    nkey = nband + N_META
    kall = jnp.concatenate([kp_ref[...], kc_ref[...], kn_ref[...]], axis=0)
    vall = jnp.concatenate([vp_ref[...], vc_ref[...], vn_ref[...]], axis=1)
    kmeta, vmeta = km_ref[...], vm_ref[...]
    key = lax.broadcasted_iota(jnp.int32, (WINDOW, 2 * LANES), 0)
    qry = jnp.bitwise_and(lax.broadcasted_iota(jnp.int32, (WINDOW, 2 * LANES), 1), WINDOW - 1)
    band_prev = jnp.where(key >= qry, 0.0, NEG_INF).astype(F32)
    band_next = jnp.where(key <= qry, 0.0, NEG_INF).astype(F32)
    first_pair = lax.broadcasted_iota(jnp.int32, (1, 2 * LANES), 1) < LANES
    p_pad = jnp.zeros((LANES - N_META, 2 * LANES), BF16)
    ones_all = jnp.ones((2 * SUBLANES, nband + LANES), BF16)
    heads_per_kv = N_Q_HEADS // N_KV_HEADS

    units = [(h, e) for h in range(N_KV_HEADS) for e in range(2)]

    def score_stage(j):
        rows = slice(j * WINDOW, (j + 1) * WINDOW)
        blk0 = i * tq + j * WINDOW
        bias_prev = band_prev + jnp.where(blk0 >= WINDOW, 0.0, NEG_INF).astype(F32)
        bias_next = band_next + jnp.where(blk0 + WINDOW < seq, 0.0, NEG_INF).astype(F32)
        scores = []
        for h, e in units:
            q2 = jnp.concatenate([q_ref[rows, (2 * h) * LANES:(2 * h + 1) * LANES],
                                  q_ref[rows, (2 * h + 1) * LANES:(2 * h + 2) * LANES]], axis=0)
            sl = slice((2 * h + e) * LANES, (2 * h + e + 1) * LANES)
            keys = jnp.concatenate([kall[j * WINDOW:j * WINDOW + nband, sl], kmeta[:, sl]], axis=0)
            s = _dot_nt(keys, q2)
            scores.append(jnp.concatenate([s[:WINDOW] + bias_prev, s[WINDOW:2 * WINDOW],
                                           s[2 * WINDOW:nband] + bias_next, s[nband:]], axis=0))
        return scores

    def softmax_stage(scores):
        probs = []
        for (h, e), s in zip(units, scores):
            sink = LOG2E * jnp.where(first_pair, sink_ref[heads_per_kv * h + e],
                                     sink_ref[heads_per_kv * h + 2 + e])
            m = jnp.maximum(jnp.max(s, axis=0, keepdims=True), sink)
            probs.append((jnp.exp2(s - m).astype(BF16), jnp.exp2(sink - m)))
        return probs

    def value_stage(j, probs):
        rows = slice(j * WINDOW, (j + 1) * WINDOW)
        halves = []
        for (h, e), (pb, p_sink) in zip(units, probs):
            v_all = jnp.concatenate([vall[h * HEAD_DIM:(h + 1) * HEAD_DIM, j * WINDOW:j * WINDOW + nband],
                                     vmeta[h * HEAD_DIM:(h + 1) * HEAD_DIM, :]], axis=1)
            v_all = jnp.concatenate([v_all, ones_all], axis=0)
            o = _dot(v_all, jnp.concatenate([pb, p_pad], axis=0))
            denom = o[HEAD_DIM:HEAD_DIM + 1] + p_sink
            halves.append(o[:HEAD_DIM] * (1.0 / denom))
        for h in range(N_KV_HEADS):
            o_t = jnp.concatenate(halves[2 * h:2 * h + 2], axis=0)
            for pp in range(2):
                o_ref[rows, (2 * h + pp) * LANES:(2 * h + pp + 1) * LANES] = (
                    o_t[:, pp * LANES:(pp + 1) * LANES].T.astype(BF16))

    scores, probs = {}, {}

    def slot(t):
        if t < nblk:
            scores[t] = score_stage(t)
        if 0 <= t - 1 < nblk:
            probs[t - 1] = softmax_stage(scores.pop(t - 1))
        if 0 <= t - 2 < nblk:
            value_stage(t - 2, probs.pop(t - 2))

    return [functools.partial(slot, t) for t in range(nblk + 2)]


def _attn_kernel(*refs, tq, seq):
    for slot in _attn_slots(pl.program_id(1), *refs, tq=tq, seq=seq):
        slot()


def _attn(q, k4, vt, k4_meta, vt_meta, sink, *, tq):
    B, S, _ = q.shape
    r = tq // WINDOW
    last = S // WINDOW - 1
    prev_idx = lambda i: jnp.maximum(i * r - 1, 0)
    next_idx = lambda i: jnp.minimum((i + 1) * r, last)
    cur = lambda w: pl.BlockSpec((None, tq, w), lambda b, i: (b, i, 0))
    kw = 4 * LANES
    return pl.pallas_call(
        functools.partial(_attn_kernel, tq=tq, seq=S),
        grid=(B, S // tq),
        in_specs=[pl.BlockSpec(memory_space=pltpu.SMEM), cur(ATTN_WIDTH),
                  cur(kw),
                  pl.BlockSpec((None, WINDOW, kw), lambda b, i: (b, prev_idx(i), 0)),
                  pl.BlockSpec((None, WINDOW, kw), lambda b, i: (b, next_idx(i), 0)),
                  pl.BlockSpec((None, KV_WIDTH, tq), lambda b, i: (b, 0, i)),
                  pl.BlockSpec((None, KV_WIDTH, WINDOW), lambda b, i: (b, 0, prev_idx(i))),
                  pl.BlockSpec((None, KV_WIDTH, WINDOW), lambda b, i: (b, 0, next_idx(i))),
                  pl.BlockSpec((N_META, kw), lambda b, i: (0, 0)),
                  pl.BlockSpec((KV_WIDTH, LANES), lambda b, i: (0, 0))],
        out_specs=cur(ATTN_WIDTH),
        out_shape=jax.ShapeDtypeStruct((B, S, ATTN_WIDTH), BF16),
        compiler_params=pltpu.CompilerParams(
            dimension_semantics=("parallel", "parallel"), vmem_limit_bytes=VMEM_LIMIT_BYTES),
        name="attn",
    )(sink, q, k4, k4, k4, vt, vt, vt, k4_meta, vt_meta)


def _factor(L):
    n1 = 16
    while L % (2 * n1) == 0:
        n1 *= 2
    rest = L // n1
    for f in (3, 5, 7, 9):
        while rest % f == 0 and n1 * f <= rest // f * 4:
            n1, rest = n1 * f, rest // f
    return n1, rest


def _round_up(x, m):
    return (x + m - 1) // m * m


def _odd_tiles(rows):
    p = _round_up(rows, SUBLANES)
    return p if (p // SUBLANES) % 2 else p + SUBLANES


def _fmix_consts(S):
    L = S + N_META
    N1, N2 = _factor(L)
    N2p = _round_up(N2, 2 * SUBLANES)
    dims = dict(L=L, N1=N1, N2=N2, N2p=N2p, P=_odd_tiles(2 * N1), Q=_odd_tiles(N2p),
                Lp=_round_up(L + N2p - N2, SUBLANES))
    k1 = np.arange(N1, dtype=np.int64)
    ang1 = 2.0 * np.pi * ((k1[:, None] * k1[None, :]) % N1) / N1
    c1, s1 = np.cos(ang1) / np.sqrt(N1), np.sin(ang1) / np.sqrt(N1)
    f1 = np.block([[c1, s1], [-s1, c1]]).astype(np.float32)
    k = k1[:, None, None] + N1 * np.arange(N2p, dtype=np.int64)[None, :, None]
    n2 = np.arange(N2p, dtype=np.int64)[None, None, :]
    ang3 = 2.0 * np.pi * (((n2 + N_META) * k) % L) / L
    valid = ((np.arange(N2p) < N2)[None, :, None] & (np.arange(N2p) < N2)[None, None, :])
    c3 = np.where(valid, np.cos(ang3), 0.0) / np.sqrt(N2)
    s3 = np.where(valid, np.sin(ang3), 0.0) / np.sqrt(N2)
    m3 = np.concatenate([c3, s3], axis=2).astype(np.float32)
    c = np.arange(FOURIER_GROUP_WIDTH, dtype=np.int64)
    angc = 2.0 * np.pi * ((c[:, None] * c[None, :]) % FOURIER_GROUP_WIDTH) / FOURIER_GROUP_WIDTH
    cd = (np.concatenate([np.cos(angc), -np.sin(angc)], axis=1)
          / np.sqrt(FOURIER_GROUP_WIDTH)).astype(np.float32)
    return dims, f1, m3, cd


def _fmix_kernel(u_ref, um_ref, cd_ref, f1_ref, m3_ref, o_ref, *scratch,
                 S, L, Lp, N1, N2, N2p, P, Q, G, NB, U1, U2, chunk):
    W = FOURIER_GROUP_WIDTH
    xr, xi, a_s, y_s = (scratch[i * G:(i + 1) * G] for i in range(4))
    cd = cd_ref[...]
    lanes = lambda g: slice(g * W, (g + 1) * W)

    per_iter = 4 // G
    def p0(c, carry):
        for uu in range(per_iter):
            r0 = pl.multiple_of((c * per_iter + uu) * chunk, chunk)
            for g in range(G):
                v = _dot(u_ref[pl.ds(r0, chunk), lanes(g)], cd)
                xr[g][pl.ds(r0, chunk), :] = v[:, :W]
                xi[g][pl.ds(r0, chunk), :] = v[:, W:]
        return carry

    lax.fori_loop(0, S // (chunk * per_iter), p0, 0)
    for g in range(G):
        vm = _dot(um_ref[:, lanes(g)], cd)
        xr[g][S:L, :] = vm[:, :W]
        xi[g][S:L, :] = vm[:, W:]
        xr[g][L:Lp, :] = jnp.zeros((Lp - L, W), F32)
        xi[g][L:Lp, :] = jnp.zeros((Lp - L, W), F32)

    def p1(t, carry):
        work = []
        for uu in range(U1):
            n2s = [(t * U1 + uu) * NB + j for j in range(NB)]
            cols = [jnp.concatenate([xr[g][pl.ds(n2, N1, stride=N2), :],
                                     xi[g][pl.ds(n2, N1, stride=N2), :]], axis=0)
                    for n2 in n2s for g in range(G)]
            work.append((n2s, _dot(f1_ref[...], jnp.concatenate(cols, axis=1).astype(BF16))))
        for n2s, a in work:
            for j, n2 in enumerate(n2s):
                for g in range(G):
                    a_s[g][pl.ds(pl.multiple_of(n2 * P, SUBLANES), 2 * N1), :] = a[:, lanes(j * G + g)]
        return carry

    lax.fori_loop(0, N2p // (NB * U1), p1, 0)

    def p2(t, carry):
        work = []
        for uu in range(U2):
            k1 = t * U2 + uu
            b = jnp.concatenate(
                [jnp.concatenate([a_s[g][pl.ds(k1, N2p, stride=P), :],
                                  a_s[g][pl.ds(N1 + k1, N2p, stride=P), :]], axis=0) for g in range(G)], axis=1)
            work.append((k1, _dot(m3_ref[k1], b.astype(BF16))))
        for k1, z in work:
            for g in range(G):
                y_s[g][pl.ds(pl.multiple_of(k1 * Q, SUBLANES), N2p), :] = z[:, lanes(g)]
        return carry

    lax.fori_loop(0, N1 // U2, p2, 0)

    for g in range(G):
        o_ref[0:N1 - N_META, lanes(g)] = y_s[g][pl.ds(0, N1, stride=Q), :][N_META:].astype(BF16)

    def p3(t, carry):
        for uu in range(2):
            k2 = 2 * t + 1 + uu
            r0 = pl.multiple_of(k2 * N1 - N_META, 2 * SUBLANES)
            for g in range(G):
                o_ref[pl.ds(r0, N1), lanes(g)] = y_s[g][pl.ds(k2, N1, stride=Q), :].astype(BF16)
        return carry

    lax.fori_loop(0, (N2 - 1) // 2, p3, 0)


def _fmix(u, u_meta, *, groups, u1, u2, chunk=512):
    B, S, _ = u.shape
    dims, f1, m3, cd = _fmix_consts(S)
    Lp, N1, N2p, P, Q = dims["Lp"], dims["N1"], dims["N2p"], dims["P"], dims["Q"]
    W, G = FOURIER_GROUP_WIDTH, groups
    nb = (2 * LANES * 2) // (G * W)
    const = lambda shape: pl.BlockSpec(shape, lambda b, g: (0,) * len(shape))
    scratch = ([pltpu.VMEM((Lp, W), F32)] * (2 * G) + [pltpu.VMEM((N2p * P, W), F32)] * G
               + [pltpu.VMEM((N1 * Q, W), F32)] * G)
    return pl.pallas_call(
        functools.partial(_fmix_kernel, S=S, chunk=chunk, G=G, NB=nb, U1=u1, U2=u2, **dims),
        grid=(B, N_FOURIER_GROUPS // G),
        in_specs=[pl.BlockSpec((None, S, G * W), lambda b, g: (b, 0, g)),
                  pl.BlockSpec((N_META, G * W), lambda b, g: (0, g)),
                  const(cd.shape), const(f1.shape), const(m3.shape)],
        out_specs=pl.BlockSpec((None, S, G * W), lambda b, g: (b, 0, g)),
        out_shape=jax.ShapeDtypeStruct((B, S, FOURIER_WIDTH), BF16),
        scratch_shapes=scratch,
        compiler_params=pltpu.CompilerParams(
            dimension_semantics=("parallel", "parallel"), vmem_limit_bytes=VMEM_LIMIT_BYTES),
        name="fmix",
    )(u, u_meta, jnp.asarray(cd).astype(BF16), jnp.asarray(f1).astype(BF16), jnp.asarray(m3).astype(BF16))


def _post_stages(x_ref, a_ref, f_ref, gate_ref, wao_ref, wf_ref, wout_ref, g2_ref, wup_ref, wdown_ref,
                 gfin_ref, y_ref, *, ff_chunk, sub):
    tm = x_ref.shape[0]
    blocks = [slice(r, r + sub) for r in range(0, tm, sub)]
    state = {}

    def mixer(rows):
        a = _dot(a_ref[rows, :], wao_ref[...])
        f = _dot(f_ref[rows, :], wf_ref[...])
        merged = (gate_ref[rows, :D_MODEL].astype(F32) * a + gate_ref[rows, D_MODEL:].astype(F32) * f).astype(BF16)
        h = x_ref[rows, :] + _dot(merged, wout_ref[...])
        state[rows.start] = (h, _rms(h, g2_ref[...]).astype(BF16))

    def mlp_chunk(c):
        for rows in blocks:
            h, n = state[rows.start]
            t = _dot(n, wup_ref[:, c * ff_chunk:(c + 1) * ff_chunk])
            r = jnp.square(jnp.maximum(t, 0.0)).astype(BF16)
            state[rows.start] = (h + _dot(r, wdown_ref[c * ff_chunk:(c + 1) * ff_chunk, :]), n)

    def final():
        for rows in blocks:
            y_ref[rows, :] = _rms(state[rows.start][0], gfin_ref[...])

    return ([functools.partial(mixer, rows) for rows in blocks]
            + [functools.partial(mlp_chunk, c) for c in range(D_FF // ff_chunk)] + [final])


def _post_kernel(*refs, ff_chunk, sub):
    for stage in _post_stages(*refs, ff_chunk=ff_chunk, sub=sub):
        stage()


def _post(x, a, f, gates, w_ao, w_f, w_out, g2, w_up, w_down, g_fin, *, tm, sub=256, ff_chunk=1024):
    B, S, _ = x.shape
    tok = lambda w: pl.BlockSpec((None, tm, w), lambda b, i: (b, i, 0))
    const = lambda shape: pl.BlockSpec(shape, lambda b, i: (0,) * len(shape), pipeline_mode=pl.Buffered(1))
    return pl.pallas_call(
        functools.partial(_post_kernel, ff_chunk=ff_chunk, sub=sub),
        grid=(B, S // tm),
        in_specs=[tok(D_MODEL), tok(ATTN_WIDTH), tok(FOURIER_WIDTH), tok(N_BRANCHES * D_MODEL),
                  const(w_ao.shape), const(w_f.shape), const(w_out.shape), const((1, D_MODEL)),
                  const(w_up.shape), const(w_down.shape), const((1, D_MODEL))],
        out_specs=tok(D_MODEL),
        out_shape=jax.ShapeDtypeStruct((B, S, D_MODEL), F32),
        compiler_params=pltpu.CompilerParams(
            dimension_semantics=("parallel", "parallel"), vmem_limit_bytes=VMEM_LIMIT_BYTES),
        name="post",
    )(x, a, f, gates, w_ao, w_f, w_out, g2, w_up, w_down, g_fin)


def _rope_table(first_pos, n_pos):
    half = ROT_DIM // 2
    inv_freq = ROPE_THETA ** (-jnp.arange(half, dtype=F32) / half)
    pos = jnp.arange(first_pos, first_pos + n_pos, dtype=jnp.int32).astype(F32)
    ang = pos[:, None] * inv_freq[None, :]
    cos, sin = jnp.cos(ang), jnp.sin(ang)
    rest = HEAD_DIM - ROT_DIM
    head = lambda a, b, fill: jnp.concatenate([a, b, jnp.full((n_pos, rest), fill, F32)], axis=1)
    zero = jnp.zeros_like(sin)
    per_head = (head(cos, cos, 1.0), head(zero, sin, 0.0), head(-sin, zero, 0.0))
    return jnp.concatenate([jnp.tile(t, (1, LANES // HEAD_DIM)) for t in per_head], axis=1)


def _trunk(x, meta_parts, table, wts, *, tm_proj, tm, tq, fmix):
    k4_m, vt_m, u_m = meta_parts
    q, k4, vt, u, gates = _proj(x, table, wts["norm_mix_g"], wts["w_in"], wts["b_gate"], tm=tm_proj,
                                transpose_v=True)
    a = _attn(q, k4, vt, k4_m, vt_m, wts["attn_sink"], tq=tq)
    f = _fmix(u, u_m, **fmix)
    return _post(x, a, f, gates, wts["w_attn_out"], wts["w_fourier"], wts["w_out"], wts["norm_mlp_g"],
                 wts["w_mlp_up"], wts["w_mlp_down"], wts["norm_final_g"], tm=tm)


def kernel(x_prompt, x_sample, meta_tokens, norm_mix_g, w_in, b_gate, attn_sink, w_attn_out, w_fourier, w_out,
           norm_mlp_g, w_mlp_up, w_mlp_down, norm_final_g):
    assert w_in.shape[0] == 1, "single-layer trunk: meta-token outputs are never consumed"
    wts = dict(
        norm_mix_g=norm_mix_g[0][None, :], w_in=w_in[0].astype(BF16), b_gate=b_gate[0][None, :],
        attn_sink=attn_sink[0], w_attn_out=w_attn_out[0].astype(BF16), w_fourier=w_fourier[0].astype(BF16),
        w_out=w_out[0].astype(BF16), norm_mlp_g=norm_mlp_g[0][None, :], w_mlp_up=w_mlp_up[0].astype(BF16),
        w_mlp_down=w_mlp_down[0].astype(BF16), norm_final_g=norm_final_g[None, :])
    table = _rope_table(N_META, max(x_prompt.shape[1], x_sample.shape[1]))
    _, k4_m, v_m, u_m, _ = _proj(meta_tokens[None], _rope_table(0, N_META),
                                 wts["norm_mix_g"], wts["w_in"], wts["b_gate"], tm=N_META, transpose_v=False)
    vt_m = jnp.pad(v_m[0].T, ((0, 0), (0, LANES - N_META)))
    meta_parts = (k4_m[0], vt_m, u_m[0])
    tiles = dict(tm_proj=1024, tm=512, tq=1024)
    y_prompt = _trunk(x_prompt, meta_parts, table, wts, fmix=dict(groups=4, u1=12, u2=24), **tiles)
    y_sample = _trunk(x_sample, meta_parts, table, wts, fmix=dict(groups=1, u1=4, u2=24), **tiles)
    return (y_prompt, y_sample)
```

```python
import functools

import numpy as np
import jax
import jax.numpy as jnp
from jax import lax
from jax.experimental import pallas as pl
from jax.experimental.pallas import tpu as pltpu

D_MODEL = 1024
HEAD_DIM = 64
N_Q_HEADS = 8
N_KV_HEADS = 2
ATTN_WIDTH = N_Q_HEADS * HEAD_DIM
KV_WIDTH = N_KV_HEADS * HEAD_DIM
WINDOW = 128
ROPE_THETA = 500000.0
ROT_DIM = HEAD_DIM // 4
N_FOURIER_GROUPS = 4
FOURIER_GROUP_WIDTH = 128
FOURIER_WIDTH = N_FOURIER_GROUPS * FOURIER_GROUP_WIDTH
N_BRANCHES = 2
IN_WIDTH = ATTN_WIDTH + 2 * KV_WIDTH + FOURIER_WIDTH + N_BRANCHES * D_MODEL
D_FF = 4 * D_MODEL
N_META = 16
RMS_EPS = 1e-6
NEG_INF = -1e30
LOG2E = 1.4426950408889634

LANES = 128
SUBLANES = 8
VMEM_LIMIT_BYTES = 56 * 1024 * 1024

_Q0, _K0, _V0, _U0, _G0 = 0, ATTN_WIDTH, ATTN_WIDTH + KV_WIDTH, ATTN_WIDTH + 2 * KV_WIDTH, \
    ATTN_WIDTH + 2 * KV_WIDTH + FOURIER_WIDTH

BF16 = jnp.bfloat16
F32 = jnp.float32


def _dot(a, b):
    return jnp.dot(a, b, preferred_element_type=F32)


def _dot_nt(a, b):
    return lax.dot_general(a, b, (((1,), (1,)), ((), ())), preferred_element_type=F32)


def _rms(x, g):
    return x * lax.rsqrt(jnp.mean(x * x, axis=-1, keepdims=True) + RMS_EPS) * g


def _proj_kernel(x_ref, tab_ref, g_ref, w_ref, b_ref,
                 q_ref, k4_ref, v_ref, u_ref, gate_ref, *, transpose_v, sub):
    tm = x_ref.shape[0]
    blocks = [slice(r, r + sub) for r in range(0, tm, sub)]
    normed = {rows.start: _rms(x_ref[rows, :], g_ref[...]).astype(BF16) for rows in blocks}
    project = lambda rows, lo, hi: _dot(normed[rows.start], w_ref[:, lo:hi])

    def rope(z, rows):
        cos_t, sin_a, sin_b = (tab_ref[rows, c * LANES:(c + 1) * LANES] for c in range(3))
        return (z * cos_t + pltpu.roll(z, ROT_DIM // 2, 1) * sin_a
                + pltpu.roll(z, LANES - ROT_DIM // 2, 1) * sin_b)

    for rows in blocks:
        zq = project(rows, _Q0, _K0)
        for c in range(ATTN_WIDTH // LANES):
            q_ref[rows, c * LANES:(c + 1) * LANES] = (
                rope(zq[:, c * LANES:(c + 1) * LANES], rows) * (LOG2E * HEAD_DIM ** -0.5)).astype(BF16)
    for rows in blocks:
        zkv = project(rows, _K0, _U0)
        zk = rope(zkv[:, :KV_WIDTH], rows)
        zv = zkv[:, KV_WIDTH:]
        zks = pltpu.roll(zk, HEAD_DIM, 1)
        low = lax.broadcasted_iota(jnp.int32, zk.shape, 1) < HEAD_DIM
        for s, (keep_low, src) in enumerate(((True, zk), (False, zks), (True, zks), (False, zk))):
            k4_ref[rows, s * LANES:(s + 1) * LANES] = jnp.where(low == keep_low, src, 0.0).astype(BF16)
        if transpose_v:
            v_ref[:, rows] = zv.T.astype(BF16)
        else:
            v_ref[rows, :] = zv.astype(BF16)
    for rows in blocks:
        u_ref[rows, :] = project(rows, _U0, _G0).astype(BF16)
    for c in range(N_BRANCHES):
        lo = _G0 + c * D_MODEL
        for rows in blocks:
            g = project(rows, lo, lo + D_MODEL) + b_ref[:, c * D_MODEL:(c + 1) * D_MODEL]
            gate_ref[rows, c * D_MODEL:(c + 1) * D_MODEL] = jax.nn.sigmoid(g).astype(BF16)


def _proj(x, table, g, w_in, b_gate, *, tm, transpose_v):
    B, S, _ = x.shape
    tok = lambda w: pl.BlockSpec((None, tm, w), lambda b, i: (b, i, 0))
    tab = pl.BlockSpec((tm, 3 * LANES), lambda b, i: (i, 0))
    const = lambda shape: pl.BlockSpec(shape, lambda b, i: (0,) * len(shape))
    out_w = (ATTN_WIDTH, 4 * LANES, KV_WIDTH, FOURIER_WIDTH, N_BRANCHES * D_MODEL)
    out_specs = [tok(w) for w in out_w]
    out_shape = [jax.ShapeDtypeStruct((B, S, w), BF16) for w in out_w]
    if transpose_v:
        out_specs[2] = pl.BlockSpec((None, KV_WIDTH, tm), lambda b, i: (b, 0, i))
        out_shape[2] = jax.ShapeDtypeStruct((B, KV_WIDTH, S), BF16)
    return pl.pallas_call(
        functools.partial(_proj_kernel, transpose_v=transpose_v, sub=min(tm, 512)),
        grid=(B, S // tm),
        in_specs=[tok(D_MODEL), tab, const((1, D_MODEL)),
                  const((D_MODEL, IN_WIDTH)), const((1, N_BRANCHES * D_MODEL))],
        out_specs=out_specs,
        out_shape=out_shape,
        compiler_params=pltpu.CompilerParams(
            dimension_semantics=("parallel", "parallel"), vmem_limit_bytes=VMEM_LIMIT_BYTES),
        name="proj",
    )(x, table, g, w_in, b_gate)


def _attn_slots(i, sink_ref, q_ref, kc_ref, kp_ref, kn_ref, vc_ref, vp_ref, vn_ref, km_ref, vm_ref,
                o_ref, *, tq, seq):
    nblk = tq // WINDOW
    nband = 3 * WINDOW
    nkey = nband + N_META
    kall = jnp.concatenate([kp_ref[...], kc_ref[...], kn_ref[...]], axis=0)
    vall = jnp.concatenate([vp_ref[...], vc_ref[...], vn_ref[...]], axis=1)
    kmeta, vmeta = km_ref[...], vm_ref[...]
    key = lax.broadcasted_iota(jnp.int32, (WINDOW, 2 * LANES), 0)
    qry = jnp.bitwise_and(lax.broadcasted_iota(jnp.int32, (WINDOW, 2 * LANES), 1), WINDOW - 1)
    band_prev = jnp.where(key >= qry, 0.0, NEG_INF).astype(F32)
    band_next = jnp.where(key <= qry, 0.0, NEG_INF).astype(F32)
    first_pair = lax.broadcasted_iota(jnp.int32, (1, 2 * LANES), 1) < LANES
    p_pad = jnp.zeros((LANES - N_META, 2 * LANES), BF16)
    ones_all = jnp.ones((2 * SUBLANES, nband + LANES), BF16)
    heads_per_kv = N_Q_HEADS // N_KV_HEADS

    units = [(h, e) for h in range(N_KV_HEADS) for e in range(2)]

    def score_stage(j):
        rows = slice(j * WINDOW, (j + 1) * WINDOW)
        blk0 = i * tq + j * WINDOW
        bias_prev = band_prev + jnp.where(blk0 >= WINDOW, 0.0, NEG_INF).astype(F32)
        bias_next = band_next + jnp.where(blk0 + WINDOW < seq, 0.0, NEG_INF).astype(F32)
        scores = []
        for h, e in units:
            q2 = jnp.concatenate([q_ref[rows, (2 * h) * LANES:(2 * h + 1) * LANES],
                                  q_ref[rows, (2 * h + 1) * LANES:(2 * h + 2) * LANES]], axis=0)
            sl = slice((2 * h + e) * LANES, (2 * h + e + 1) * LANES)
            keys = jnp.concatenate([kall[j * WINDOW:j * WINDOW + nband, sl], kmeta[:, sl]], axis=0)
            s = _dot_nt(keys, q2)
            scores.append(jnp.concatenate([s[:WINDOW] + bias_prev, s[WINDOW:2 * WINDOW],
                                           s[2 * WINDOW:nband] + bias_next, s[nband:]], axis=0))
        return scores

    def softmax_stage(scores):
        probs = []
        for (h, e), s in zip(units, scores):
            sink = LOG2E * jnp.where(first_pair, sink_ref[heads_per_kv * h + e],
                                     sink_ref[heads_per_kv * h + 2 + e])
            m = jnp.maximum(jnp.max(s, axis=0, keepdims=True), sink)
            probs.append((jnp.exp2(s - m).astype(BF16), jnp.exp2(sink - m)))
        return probs

    def value_stage(j, probs):
        rows = slice(j * WINDOW, (j + 1) * WINDOW)
        halves = []
        for (h, e), (pb, p_sink) in zip(units, probs):
            v_all = jnp.concatenate([vall[h * HEAD_DIM:(h + 1) * HEAD_DIM, j * WINDOW:j * WINDOW + nband],
                                     vmeta[h * HEAD_DIM:(h + 1) * HEAD_DIM, :]], axis=1)
            v_all = jnp.concatenate([v_all, ones_all], axis=0)
            o = _dot(v_all, jnp.concatenate([pb, p_pad], axis=0))
            denom = o[HEAD_DIM:HEAD_DIM + 1] + p_sink
            halves.append(o[:HEAD_DIM] * (1.0 / denom))
        for h in range(N_KV_HEADS):
            o_t = jnp.concatenate(halves[2 * h:2 * h + 2], axis=0)
            for pp in range(2):
                o_ref[rows, (2 * h + pp) * LANES:(2 * h + pp + 1) * LANES] = (
                    o_t[:, pp * LANES:(pp + 1) * LANES].T.astype(BF16))

    scores, probs = {}, {}

    def slot(t):
        if t < nblk:
            scores[t] = score_stage(t)
        if 0 <= t - 1 < nblk:
            probs[t - 1] = softmax_stage(scores.pop(t - 1))
        if 0 <= t - 2 < nblk:
            value_stage(t - 2, probs.pop(t - 2))

    return [functools.partial(slot, t) for t in range(nblk + 2)]


def _attn_kernel(*refs, tq, seq):
    for slot in _attn_slots(pl.program_id(1), *refs, tq=tq, seq=seq):
        slot()


def _attn(q, k4, vt, k4_meta, vt_meta, sink, *, tq):
    B, S, _ = q.shape
    r = tq // WINDOW
    last = S // WINDOW - 1
    prev_idx = lambda i: jnp.maximum(i * r - 1, 0)
    next_idx = lambda i: jnp.minimum((i + 1) * r, last)
    cur = lambda w: pl.BlockSpec((None, tq, w), lambda b, i: (b, i, 0))
    kw = 4 * LANES
    return pl.pallas_call(
        functools.partial(_attn_kernel, tq=tq, seq=S),
        grid=(B, S // tq),
        in_specs=[pl.BlockSpec(memory_space=pltpu.SMEM), cur(ATTN_WIDTH),
                  cur(kw),
                  pl.BlockSpec((None, WINDOW, kw), lambda b, i: (b, prev_idx(i), 0)),
                  pl.BlockSpec((None, WINDOW, kw), lambda b, i: (b, next_idx(i), 0)),
                  pl.BlockSpec((None, KV_WIDTH, tq), lambda b, i: (b, 0, i)),
                  pl.BlockSpec((None, KV_WIDTH, WINDOW), lambda b, i: (b, 0, prev_idx(i))),
                  pl.BlockSpec((None, KV_WIDTH, WINDOW), lambda b, i: (b, 0, next_idx(i))),
                  pl.BlockSpec((N_META, kw), lambda b, i: (0, 0)),
                  pl.BlockSpec((KV_WIDTH, LANES), lambda b, i: (0, 0))],
        out_specs=cur(ATTN_WIDTH),
        out_shape=jax.ShapeDtypeStruct((B, S, ATTN_WIDTH), BF16),
        compiler_params=pltpu.CompilerParams(
            dimension_semantics=("parallel", "parallel"), vmem_limit_bytes=VMEM_LIMIT_BYTES),
        name="attn",
    )(sink, q, k4, k4, k4, vt, vt, vt, k4_meta, vt_meta)


MXU_DEPTH = 256


def _factor(L):
    best = None
    for n1 in range(SUBLANES, L, SUBLANES):
        if L % n1 or (L // n1) % 8 == 0:
            continue
        n2p = _round_up(L // n1, 2 * SUBLANES)
        cost = (-(-2 * n1 // MXU_DEPTH) + -(-2 * n2p // MXU_DEPTH), abs(n1 - L ** 0.5))
        if best is None or cost < best[0]:
            best = (cost, n1)
    return best[1], L // best[1]


def _round_up(x, m):
    return (x + m - 1) // m * m


def _odd_tiles(rows):
    p = _round_up(rows, SUBLANES)
    return p if (p // SUBLANES) % 2 else p + SUBLANES


def _fmix_consts(S):
    L = S + N_META
    N1, N2 = _factor(L)
    N2p = _round_up(N2, 2 * SUBLANES)
    dims = dict(L=L, N1=N1, N2=N2, N2p=N2p, P=_odd_tiles(2 * N1), Q=_odd_tiles(N2p),
                Lp=_round_up(L + N2p - N2, SUBLANES))
    k1 = np.arange(N1, dtype=np.int64)
    ang1 = 2.0 * np.pi * ((k1[:, None] * k1[None, :]) % N1) / N1
    c1, s1 = np.cos(ang1) / np.sqrt(N1), np.sin(ang1) / np.sqrt(N1)
    f1 = np.block([[c1, s1], [-s1, c1]]).astype(np.float32)
    k = k1[:, None, None] + N1 * np.arange(N2p, dtype=np.int64)[None, :, None]
    n2 = np.arange(N2p, dtype=np.int64)[None, None, :]
    ang3 = 2.0 * np.pi * (((n2 + N_META) * k) % L) / L
    valid = ((np.arange(N2p) < N2)[None, :, None] & (np.arange(N2p) < N2)[None, None, :])
    c3 = np.where(valid, np.cos(ang3), 0.0) / np.sqrt(N2)
    s3 = np.where(valid, np.sin(ang3), 0.0) / np.sqrt(N2)
    m3 = np.concatenate([c3, s3], axis=2).astype(np.float32)
    c = np.arange(FOURIER_GROUP_WIDTH, dtype=np.int64)
    angc = 2.0 * np.pi * ((c[:, None] * c[None, :]) % FOURIER_GROUP_WIDTH) / FOURIER_GROUP_WIDTH
    cd = (np.concatenate([np.cos(angc), -np.sin(angc)], axis=1)
          / np.sqrt(FOURIER_GROUP_WIDTH)).astype(np.float32)
    return dims, f1, m3, cd


def _fmix_kernel(u_ref, um_ref, cd_ref, f1_ref, m3_ref, o_ref, *scratch,
                 S, L, Lp, N1, N2, N2p, P, Q, G, NB, U1, U2, chunk):
    W = FOURIER_GROUP_WIDTH
    xr, xi, a_s, y_s = (scratch[i * G:(i + 1) * G] for i in range(4))
    cd = cd_ref[...]
    lanes = lambda g: slice(g * W, (g + 1) * W)

    per_iter = 4 // G
    def p0(c, carry):
        for uu in range(per_iter):
            r0 = pl.multiple_of((c * per_iter + uu) * chunk, chunk)
            for g in range(G):
                v = _dot(u_ref[pl.ds(r0, chunk), lanes(g)], cd)
                xr[g][pl.ds(r0, chunk), :] = v[:, :W]
                xi[g][pl.ds(r0, chunk), :] = v[:, W:]
        return carry

    lax.fori_loop(0, S // (chunk * per_iter), p0, 0)
    for g in range(G):
        vm = _dot(um_ref[:, lanes(g)], cd)
        xr[g][S:L, :] = vm[:, :W]
        xi[g][S:L, :] = vm[:, W:]
        xr[g][L:Lp, :] = jnp.zeros((Lp - L, W), F32)
        xi[g][L:Lp, :] = jnp.zeros((Lp - L, W), F32)

    def p1(t, carry):
        work = []
        for uu in range(U1):
            n2s = [(t * U1 + uu) * NB + j for j in range(NB)]
            cols = [jnp.concatenate([xr[g][pl.ds(n2, N1, stride=N2), :],
                                     xi[g][pl.ds(n2, N1, stride=N2), :]], axis=0)
                    for n2 in n2s for g in range(G)]
            work.append((n2s, _dot(f1_ref[...], jnp.concatenate(cols, axis=1).astype(BF16))))
        for n2s, a in work:
            for j, n2 in enumerate(n2s):
                for g in range(G):
                    a_s[g][pl.ds(pl.multiple_of(n2 * P, SUBLANES), 2 * N1), :] = a[:, lanes(j * G + g)]
        return carry

    lax.fori_loop(0, N2p // (NB * U1), p1, 0)

    def p2(t, carry):
        work = []
        for uu in range(U2):
            k1 = t * U2 + uu
            b = jnp.concatenate(
                [jnp.concatenate([a_s[g][pl.ds(k1, N2p, stride=P), :],
                                  a_s[g][pl.ds(N1 + k1, N2p, stride=P), :]], axis=0) for g in range(G)], axis=1)
            work.append((k1, _dot(m3_ref[k1], b.astype(BF16))))
        for k1, z in work:
            for g in range(G):
                y_s[g][pl.ds(pl.multiple_of(k1 * Q, SUBLANES), N2p), :] = z[:, lanes(g)]
        return carry

    lax.fori_loop(0, N1 // U2, p2, 0)

    def pair(g, k2):
        return jnp.concatenate([y_s[g][pl.ds(k2, N1, stride=Q), :],
                                y_s[g][pl.ds(k2 + 1, N1, stride=Q), :]], axis=0)

    for g in range(G):
        o_ref[0:2 * N1 - N_META, lanes(g)] = pair(g, 0)[N_META:].astype(BF16)

    def p3(t, carry):
        k2 = 2 * t + 2
        r0 = pl.multiple_of(k2 * N1 - N_META, 2 * SUBLANES)
        for g in range(G):
            o_ref[pl.ds(r0, 2 * N1), lanes(g)] = pair(g, k2).astype(BF16)
        return carry

    lax.fori_loop(0, N2 // 2 - 1, p3, 0)
    if N2 % 2:
        r0 = (N2 - 1) * N1 - N_META
        for g in range(G):
            o_ref[r0:r0 + N1, lanes(g)] = y_s[g][pl.ds(N2 - 1, N1, stride=Q), :].astype(BF16)


def _fmix(u, u_meta, *, groups, u1, u2, chunk=512):
    B, S, _ = u.shape
    dims, f1, m3, cd = _fmix_consts(S)
    Lp, N1, N2p, P, Q = dims["Lp"], dims["N1"], dims["N2p"], dims["P"], dims["Q"]
    W, G = FOURIER_GROUP_WIDTH, groups
    nb = (2 * LANES * 2) // (G * W)
    assert N2p % (nb * u1) == 0 and N1 % u2 == 0 and S % (chunk * (4 // G)) == 0
    const = lambda shape: pl.BlockSpec(shape, lambda b, g: (0,) * len(shape))
    scratch = ([pltpu.VMEM((Lp, W), F32)] * (2 * G) + [pltpu.VMEM((N2p * P, W), F32)] * G
               + [pltpu.VMEM((N1 * Q, W), F32)] * G)
    return pl.pallas_call(
        functools.partial(_fmix_kernel, S=S, chunk=chunk, G=G, NB=nb, U1=u1, U2=u2, **dims),
        grid=(B, N_FOURIER_GROUPS // G),
        in_specs=[pl.BlockSpec((None, S, G * W), lambda b, g: (b, 0, g)),
                  pl.BlockSpec((N_META, G * W), lambda b, g: (0, g)),
                  const(cd.shape), const(f1.shape), const(m3.shape)],
        out_specs=pl.BlockSpec((None, S, G * W), lambda b, g: (b, 0, g)),
        out_shape=jax.ShapeDtypeStruct((B, S, FOURIER_WIDTH), BF16),
        scratch_shapes=scratch,
        compiler_params=pltpu.CompilerParams(
            dimension_semantics=("parallel", "parallel"), vmem_limit_bytes=VMEM_LIMIT_BYTES),
        name="fmix",
    )(u, u_meta, jnp.asarray(cd).astype(BF16), jnp.asarray(f1).astype(BF16), jnp.asarray(m3).astype(BF16))


def _post_stages(x_ref, a_ref, f_ref, gate_ref, wao_ref, wf_ref, wout_ref, g2_ref, wup_ref, wdown_ref,
                 gfin_ref, y_ref, *, ff_chunk, sub):
    tm = x_ref.shape[0]
    blocks = [slice(r, r + sub) for r in range(0, tm, sub)]
    state = {}

    def mixer(rows):
        a = _dot(a_ref[rows, :], wao_ref[...])
        f = _dot(f_ref[rows, :], wf_ref[...])
        merged = (gate_ref[rows, :D_MODEL].astype(F32) * a + gate_ref[rows, D_MODEL:].astype(F32) * f).astype(BF16)
        h = x_ref[rows, :] + _dot(merged, wout_ref[...])
        state[rows.start] = (h, _rms(h, g2_ref[...]).astype(BF16))

    def mlp_chunk(c):
        for rows in blocks:
            h, n = state[rows.start]
            t = _dot(n, wup_ref[:, c * ff_chunk:(c + 1) * ff_chunk])
            r = jnp.square(jnp.maximum(t, 0.0)).astype(BF16)
            state[rows.start] = (h + _dot(r, wdown_ref[c * ff_chunk:(c + 1) * ff_chunk, :]), n)

    def final():
        for rows in blocks:
            y_ref[rows, :] = _rms(state[rows.start][0], gfin_ref[...])

    return ([functools.partial(mixer, rows) for rows in blocks]
            + [functools.partial(mlp_chunk, c) for c in range(D_FF // ff_chunk)] + [final])


def _post_kernel(*refs, ff_chunk, sub):
    for stage in _post_stages(*refs, ff_chunk=ff_chunk, sub=sub):
        stage()


def _post(x, a, f, gates, w_ao, w_f, w_out, g2, w_up, w_down, g_fin, *, tm, sub=256, ff_chunk=1024):
    B, S, _ = x.shape
    tok = lambda w: pl.BlockSpec((None, tm, w), lambda b, i: (b, i, 0))
    const = lambda shape: pl.BlockSpec(shape, lambda b, i: (0,) * len(shape), pipeline_mode=pl.Buffered(1))
    return pl.pallas_call(
        functools.partial(_post_kernel, ff_chunk=ff_chunk, sub=sub),
        grid=(B, S // tm),
        in_specs=[tok(D_MODEL), tok(ATTN_WIDTH), tok(FOURIER_WIDTH), tok(N_BRANCHES * D_MODEL),
                  const(w_ao.shape), const(w_f.shape), const(w_out.shape), const((1, D_MODEL)),
                  const(w_up.shape), const(w_down.shape), const((1, D_MODEL))],
        out_specs=tok(D_MODEL),
        out_shape=jax.ShapeDtypeStruct((B, S, D_MODEL), F32),
        compiler_params=pltpu.CompilerParams(
            dimension_semantics=("parallel", "parallel"), vmem_limit_bytes=VMEM_LIMIT_BYTES),
        name="post",
    )(x, a, f, gates, w_ao, w_f, w_out, g2, w_up, w_down, g_fin)


def _rope_table(first_pos, n_pos):
    half = ROT_DIM // 2
    inv_freq = ROPE_THETA ** (-jnp.arange(half, dtype=F32) / half)
    pos = jnp.arange(first_pos, first_pos + n_pos, dtype=jnp.int32).astype(F32)
    ang = pos[:, None] * inv_freq[None, :]
    cos, sin = jnp.cos(ang), jnp.sin(ang)
    rest = HEAD_DIM - ROT_DIM
    head = lambda a, b, fill: jnp.concatenate([a, b, jnp.full((n_pos, rest), fill, F32)], axis=1)
    zero = jnp.zeros_like(sin)
    per_head = (head(cos, cos, 1.0), head(zero, sin, 0.0), head(-sin, zero, 0.0))
    return jnp.concatenate([jnp.tile(t, (1, LANES // HEAD_DIM)) for t in per_head], axis=1)


def _trunk(x, meta_parts, table, wts, *, tm_proj, tm, tq, fmix):
    k4_m, vt_m, u_m = meta_parts
    q, k4, vt, u, gates = _proj(x, table, wts["norm_mix_g"], wts["w_in"], wts["b_gate"], tm=tm_proj,
                                transpose_v=True)
    a = _attn(q, k4, vt, k4_m, vt_m, wts["attn_sink"], tq=tq)
    f = _fmix(u, u_m, **fmix)
    return _post(x, a, f, gates, wts["w_attn_out"], wts["w_fourier"], wts["w_out"], wts["norm_mlp_g"],
                 wts["w_mlp_up"], wts["w_mlp_down"], wts["norm_final_g"], tm=tm)


def kernel(x_prompt, x_sample, meta_tokens, norm_mix_g, w_in, b_gate, attn_sink, w_attn_out, w_fourier, w_out,
           norm_mlp_g, w_mlp_up, w_mlp_down, norm_final_g):
    assert w_in.shape[0] == 1, "single-layer trunk: meta-token outputs are never consumed"
    wts = dict(
        norm_mix_g=norm_mix_g[0][None, :], w_in=w_in[0].astype(BF16), b_gate=b_gate[0][None, :],
        attn_sink=attn_sink[0], w_attn_out=w_attn_out[0].astype(BF16), w_fourier=w_fourier[0].astype(BF16),
        w_out=w_out[0].astype(BF16), norm_mlp_g=norm_mlp_g[0][None, :], w_mlp_up=w_mlp_up[0].astype(BF16),
        w_mlp_down=w_mlp_down[0].astype(BF16), norm_final_g=norm_final_g[None, :])
    table = _rope_table(N_META, max(x_prompt.shape[1], x_sample.shape[1]))
    _, k4_m, v_m, u_m, _ = _proj(meta_tokens[None], _rope_table(0, N_META),
                                 wts["norm_mix_g"], wts["w_in"], wts["b_gate"], tm=N_META, transpose_v=False)
    vt_m = jnp.pad(v_m[0].T, ((0, 0), (0, LANES - N_META)))
    meta_parts = (k4_m[0], vt_m, u_m[0])
    tiles = dict(tm_proj=1024, tm=512, tq=2048)
    y_prompt = _trunk(x_prompt, meta_parts, table, wts, fmix=dict(groups=4, u1=12, u2=24), **tiles)
    y_sample = _trunk(x_sample, meta_parts, table, wts, fmix=dict(groups=1, u1=4, u2=24), **tiles)
    return (y_prompt, y_sample)
```

```python
import functools

import numpy as np
import jax
import jax.numpy as jnp
from jax import lax
from jax.experimental import pallas as pl
from jax.experimental.pallas import tpu as pltpu

D_MODEL = 1024
HEAD_DIM = 64
N_Q_HEADS = 8
N_KV_HEADS = 2
ATTN_WIDTH = N_Q_HEADS * HEAD_DIM
KV_WIDTH = N_KV_HEADS * HEAD_DIM
WINDOW = 128
ROPE_THETA = 500000.0
ROT_DIM = HEAD_DIM // 4
N_FOURIER_GROUPS = 4
FOURIER_GROUP_WIDTH = 128
FOURIER_WIDTH = N_FOURIER_GROUPS * FOURIER_GROUP_WIDTH
N_BRANCHES = 2
IN_WIDTH = ATTN_WIDTH + 2 * KV_WIDTH + FOURIER_WIDTH + N_BRANCHES * D_MODEL
D_FF = 4 * D_MODEL
N_META = 16
RMS_EPS = 1e-6
NEG_INF = -1e30
LOG2E = 1.4426950408889634

LANES = 128
SUBLANES = 8
VMEM_LIMIT_BYTES = 56 * 1024 * 1024

_Q0, _K0, _V0, _U0, _G0 = 0, ATTN_WIDTH, ATTN_WIDTH + KV_WIDTH, ATTN_WIDTH + 2 * KV_WIDTH, \
    ATTN_WIDTH + 2 * KV_WIDTH + FOURIER_WIDTH

BF16 = jnp.bfloat16
F32 = jnp.float32


def _dot(a, b):
    return jnp.dot(a, b, preferred_element_type=F32)


def _dot_nt(a, b):
    return lax.dot_general(a, b, (((1,), (1,)), ((), ())), preferred_element_type=F32)


def _rms(x, g):
    return x * lax.rsqrt(jnp.mean(x * x, axis=-1, keepdims=True) + RMS_EPS) * g


def _proj_kernel(x_ref, tab_ref, g_ref, w_ref, b_ref,
                 q_ref, k4_ref, v_ref, u_ref, gate_ref, *, transpose_v, sub):
    tm = x_ref.shape[0]
    blocks = [slice(r, r + sub) for r in range(0, tm, sub)]
    normed = {rows.start: _rms(x_ref[rows, :], g_ref[...]).astype(BF16) for rows in blocks}
    project = lambda rows, lo, hi: _dot(normed[rows.start], w_ref[:, lo:hi])

    def rope(z, rows):
        cos_t, sin_a, sin_b = (tab_ref[rows, c * LANES:(c + 1) * LANES] for c in range(3))
        return (z * cos_t + pltpu.roll(z, ROT_DIM // 2, 1) * sin_a
                + pltpu.roll(z, LANES - ROT_DIM // 2, 1) * sin_b)

    for rows in blocks:
        zq = project(rows, _Q0, _K0)
        for c in range(ATTN_WIDTH // LANES):
            q_ref[rows, c * LANES:(c + 1) * LANES] = (
                rope(zq[:, c * LANES:(c + 1) * LANES], rows) * (LOG2E * HEAD_DIM ** -0.5)).astype(BF16)
    for rows in blocks:
        zkv = project(rows, _K0, _U0)
        zk = rope(zkv[:, :KV_WIDTH], rows)
        zv = zkv[:, KV_WIDTH:]
        zks = pltpu.roll(zk, HEAD_DIM, 1)
        low = lax.broadcasted_iota(jnp.int32, zk.shape, 1) < HEAD_DIM
        for s, (keep_low, src) in enumerate(((True, zk), (False, zks), (True, zks), (False, zk))):
            k4_ref[rows, s * LANES:(s + 1) * LANES] = jnp.where(low == keep_low, src, 0.0).astype(BF16)
        if transpose_v:
            v_ref[:, rows] = zv.T.astype(BF16)
        else:
            v_ref[rows, :] = zv.astype(BF16)
    for rows in blocks:
        u_ref[rows, :] = project(rows, _U0, _G0).astype(BF16)
    for c in range(N_BRANCHES):
        lo = _G0 + c * D_MODEL
        for rows in blocks:
            g = project(rows, lo, lo + D_MODEL) + b_ref[:, c * D_MODEL:(c + 1) * D_MODEL]
            gate_ref[rows, c * D_MODEL:(c + 1) * D_MODEL] = jax.nn.sigmoid(g).astype(BF16)


def _proj(x, table, g, w_in, b_gate, *, tm, transpose_v):
    B, S, _ = x.shape
    tok = lambda w: pl.BlockSpec((None, tm, w), lambda b, i: (b, i, 0))
    tab = pl.BlockSpec((tm, 3 * LANES), lambda b, i: (i, 0))
    const = lambda shape: pl.BlockSpec(shape, lambda b, i: (0,) * len(shape))
    out_w = (ATTN_WIDTH, 4 * LANES, KV_WIDTH, FOURIER_WIDTH, N_BRANCHES * D_MODEL)
    out_specs = [tok(w) for w in out_w]
    out_shape = [jax.ShapeDtypeStruct((B, S, w), BF16) for w in out_w]
    if transpose_v:
        out_specs[2] = pl.BlockSpec((None, KV_WIDTH, tm), lambda b, i: (b, 0, i))
        out_shape[2] = jax.ShapeDtypeStruct((B, KV_WIDTH, S), BF16)
    return pl.pallas_call(
        functools.partial(_proj_kernel, transpose_v=transpose_v, sub=min(tm, 512)),
        grid=(B, S // tm),
        in_specs=[tok(D_MODEL), tab, const((1, D_MODEL)),
                  const((D_MODEL, IN_WIDTH)), const((1, N_BRANCHES * D_MODEL))],
        out_specs=out_specs,
        out_shape=out_shape,
        compiler_params=pltpu.CompilerParams(
            dimension_semantics=("parallel", "parallel"), vmem_limit_bytes=VMEM_LIMIT_BYTES),
        name="proj",
    )(x, table, g, w_in, b_gate)


def _attn_slots(i, sink_ref, q_ref, kc_ref, kp_ref, kn_ref, vc_ref, vp_ref, vn_ref, km_ref, vm_ref,
                o_ref, *, tq, seq):
    nblk = tq // WINDOW
    nband = 3 * WINDOW
    nkey = nband + N_META
    kall = jnp.concatenate([kp_ref[...], kc_ref[...], kn_ref[...]], axis=0)
    vall = jnp.concatenate([vp_ref[...], vc_ref[...], vn_ref[...]], axis=1)
    kmeta, vmeta = km_ref[...], vm_ref[...]
    key = lax.broadcasted_iota(jnp.int32, (WINDOW, 2 * LANES), 0)
    qry = jnp.bitwise_and(lax.broadcasted_iota(jnp.int32, (WINDOW, 2 * LANES), 1), WINDOW - 1)
    band_prev = jnp.where(key >= qry, 0.0, NEG_INF).astype(F32)
    band_next = jnp.where(key <= qry, 0.0, NEG_INF).astype(F32)
    first_pair = lax.broadcasted_iota(jnp.int32, (1, 2 * LANES), 1) < LANES
    p_pad = jnp.zeros((LANES - N_META, 2 * LANES), BF16)
    ones_all = jnp.ones((2 * SUBLANES, nband + LANES), BF16)
    heads_per_kv = N_Q_HEADS // N_KV_HEADS

    units = [(h, e) for h in range(N_KV_HEADS) for e in range(2)]

    def score_stage(j):
        rows = slice(j * WINDOW, (j + 1) * WINDOW)
        blk0 = i * tq + j * WINDOW
        bias_prev = band_prev + jnp.where(blk0 >= WINDOW, 0.0, NEG_INF).astype(F32)
        bias_next = band_next + jnp.where(blk0 + WINDOW < seq, 0.0, NEG_INF).astype(F32)
        scores = []
        for h, e in units:
            q2 = jnp.concatenate([q_ref[rows, (2 * h) * LANES:(2 * h + 1) * LANES],
                                  q_ref[rows, (2 * h + 1) * LANES:(2 * h + 2) * LANES]], axis=0)
            sl = slice((2 * h + e) * LANES, (2 * h + e + 1) * LANES)
            keys = jnp.concatenate([kall[j * WINDOW:j * WINDOW + nband, sl], kmeta[:, sl]], axis=0)
            s = _dot_nt(keys, q2)
            scores.append(jnp.concatenate([s[:WINDOW] + bias_prev, s[WINDOW:2 * WINDOW],
                                           s[2 * WINDOW:nband] + bias_next, s[nband:]], axis=0))
        return scores

    def softmax_stage(scores):
        probs = []
        for (h, e), s in zip(units, scores):
            sink = LOG2E * jnp.where(first_pair, sink_ref[heads_per_kv * h + e],
                                     sink_ref[heads_per_kv * h + 2 + e])
            m = jnp.maximum(jnp.max(s, axis=0, keepdims=True), sink)
            probs.append((jnp.exp2(s - m).astype(BF16), jnp.exp2(sink - m)))
        return probs

    def value_stage(j, probs):
        rows = slice(j * WINDOW, (j + 1) * WINDOW)
        halves = []
        for (h, e), (pb, p_sink) in zip(units, probs):
            v_all = jnp.concatenate([vall[h * HEAD_DIM:(h + 1) * HEAD_DIM, j * WINDOW:j * WINDOW + nband],
                                     vmeta[h * HEAD_DIM:(h + 1) * HEAD_DIM, :]], axis=1)
            v_all = jnp.concatenate([v_all, ones_all], axis=0)
            o = _dot(v_all, jnp.concatenate([pb, p_pad], axis=0))
            denom = o[HEAD_DIM:HEAD_DIM + 1] + p_sink
            halves.append(o[:HEAD_DIM] * (1.0 / denom))
        for h in range(N_KV_HEADS):
            o_t = jnp.concatenate(halves[2 * h:2 * h + 2], axis=0)
            for pp in range(2):
                o_ref[rows, (2 * h + pp) * LANES:(2 * h + pp + 1) * LANES] = (
                    o_t[:, pp * LANES:(pp + 1) * LANES].T.astype(BF16))

    scores, probs = {}, {}

    def slot(t):
        if t < nblk:
            scores[t] = score_stage(t)
        if 0 <= t - 1 < nblk:
            probs[t - 1] = softmax_stage(scores.pop(t - 1))
        if 0 <= t - 2 < nblk:
            value_stage(t - 2, probs.pop(t - 2))

    return [functools.partial(slot, t) for t in range(nblk + 2)]


def _attn_kernel(*refs, tq, seq):
    for slot in _attn_slots(pl.program_id(1), *refs, tq=tq, seq=seq):
        slot()


def _attn(q, k4, vt, k4_meta, vt_meta, sink, *, tq):
    B, S, _ = q.shape
    r = tq // WINDOW
    last = S // WINDOW - 1
    prev_idx = lambda i: jnp.maximum(i * r - 1, 0)
    next_idx = lambda i: jnp.minimum((i + 1) * r, last)
    cur = lambda w: pl.BlockSpec((None, tq, w), lambda b, i: (b, i, 0))
    kw = 4 * LANES
    return pl.pallas_call(
        functools.partial(_attn_kernel, tq=tq, seq=S),
        grid=(B, S // tq),
        in_specs=[pl.BlockSpec(memory_space=pltpu.SMEM), cur(ATTN_WIDTH),
                  cur(kw),
                  pl.BlockSpec((None, WINDOW, kw), lambda b, i: (b, prev_idx(i), 0)),
                  pl.BlockSpec((None, WINDOW, kw), lambda b, i: (b, next_idx(i), 0)),
                  pl.BlockSpec((None, KV_WIDTH, tq), lambda b, i: (b, 0, i)),
                  pl.BlockSpec((None, KV_WIDTH, WINDOW), lambda b, i: (b, 0, prev_idx(i))),
                  pl.BlockSpec((None, KV_WIDTH, WINDOW), lambda b, i: (b, 0, next_idx(i))),
                  pl.BlockSpec((N_META, kw), lambda b, i: (0, 0)),
                  pl.BlockSpec((KV_WIDTH, LANES), lambda b, i: (0, 0))],
        out_specs=cur(ATTN_WIDTH),
        out_shape=jax.ShapeDtypeStruct((B, S, ATTN_WIDTH), BF16),
        compiler_params=pltpu.CompilerParams(
            dimension_semantics=("parallel", "parallel"), vmem_limit_bytes=VMEM_LIMIT_BYTES),
        name="attn",
    )(sink, q, k4, k4, k4, vt, vt, vt, k4_meta, vt_meta)


MXU_DEPTH = 256


def _factor(L):
    best = None
    for n1 in range(SUBLANES, L, SUBLANES):
        if L % n1 or (L // n1) % 8 == 0:
            continue
        n2p = _round_up(L // n1, 2 * SUBLANES)
        cost = (-(-2 * n1 // MXU_DEPTH) + -(-2 * n2p // MXU_DEPTH), abs(n1 - L ** 0.5))
        if best is None or cost < best[0]:
            best = (cost, n1)
    return best[1], L // best[1]


def _round_up(x, m):
    return (x + m - 1) // m * m


def _odd_tiles(rows):
    p = _round_up(rows, SUBLANES)
    return p if (p // SUBLANES) % 2 else p + SUBLANES


def _fmix_consts(S):
    L = S + N_META
    N1, N2 = _factor(L)
    N2p = _round_up(N2, 2 * SUBLANES)
    dims = dict(L=L, N1=N1, N2=N2, N2p=N2p, P=_odd_tiles(2 * N1), Q=_odd_tiles(N2p),
                Lp=_round_up(L + N2p - N2, SUBLANES))
    k1 = np.arange(N1, dtype=np.int64)
    ang1 = 2.0 * np.pi * ((k1[:, None] * k1[None, :]) % N1) / N1
    c1, s1 = np.cos(ang1) / np.sqrt(N1), np.sin(ang1) / np.sqrt(N1)
    f1 = np.block([[c1, s1], [-s1, c1]]).astype(np.float32)
    k = k1[:, None, None] + N1 * np.arange(N2p, dtype=np.int64)[None, :, None]
    n2 = np.arange(N2p, dtype=np.int64)[None, None, :]
    ang3 = 2.0 * np.pi * (((n2 + N_META) * k) % L) / L
    valid = ((np.arange(N2p) < N2)[None, :, None] & (np.arange(N2p) < N2)[None, None, :])
    c3 = np.where(valid, np.cos(ang3), 0.0) / np.sqrt(N2)
    s3 = np.where(valid, np.sin(ang3), 0.0) / np.sqrt(N2)
    m3 = np.concatenate([c3, s3], axis=2).astype(np.float32)
    c = np.arange(FOURIER_GROUP_WIDTH, dtype=np.int64)
    angc = 2.0 * np.pi * ((c[:, None] * c[None, :]) % FOURIER_GROUP_WIDTH) / FOURIER_GROUP_WIDTH
    cd = (np.concatenate([np.cos(angc), -np.sin(angc)], axis=1)
          / np.sqrt(FOURIER_GROUP_WIDTH)).astype(np.float32)
    return dims, f1, m3, cd


def _fmix_kernel(u_ref, um_ref, cd_ref, f1_ref, m3_ref, o_ref, *scratch,
                 S, L, Lp, N1, N2, N2p, P, Q, G, NB, U1, U2, chunk):
    W = FOURIER_GROUP_WIDTH
    xr, xi, a_s, y_s = (scratch[i * G:(i + 1) * G] for i in range(4))
    cd = cd_ref[...]
    lanes = lambda g: slice(g * W, (g + 1) * W)

    per_iter = 4 // G
    def p0(c, carry):
        for uu in range(per_iter):
            r0 = pl.multiple_of((c * per_iter + uu) * chunk, chunk)
            for g in range(G):
                v = _dot(u_ref[pl.ds(r0, chunk), lanes(g)], cd)
                xr[g][pl.ds(r0, chunk), :] = v[:, :W]
                xi[g][pl.ds(r0, chunk), :] = v[:, W:]
        return carry

    lax.fori_loop(0, S // (chunk * per_iter), p0, 0)
    for g in range(G):
        vm = _dot(um_ref[:, lanes(g)], cd)
        xr[g][S:L, :] = vm[:, :W]
        xi[g][S:L, :] = vm[:, W:]
        xr[g][L:Lp, :] = jnp.zeros((Lp - L, W), F32)
        xi[g][L:Lp, :] = jnp.zeros((Lp - L, W), F32)

    def p1(t, carry):
        work = []
        for uu in range(U1):
            n2s = [(t * U1 + uu) * NB + j for j in range(NB)]
            cols = [jnp.concatenate([xr[g][pl.ds(n2, N1, stride=N2), :],
                                     xi[g][pl.ds(n2, N1, stride=N2), :]], axis=0)
                    for n2 in n2s for g in range(G)]
            work.append((n2s, _dot(f1_ref[...], jnp.concatenate(cols, axis=1).astype(BF16))))
        for n2s, a in work:
            for j, n2 in enumerate(n2s):
                for g in range(G):
                    a_s[g][pl.ds(pl.multiple_of(n2 * P, SUBLANES), 2 * N1), :] = a[:, lanes(j * G + g)]
        return carry

    lax.fori_loop(0, N2p // (NB * U1), p1, 0)

    def p2(t, carry):
        work = []
        for uu in range(U2):
            k1 = t * U2 + uu
            b = jnp.concatenate(
                [jnp.concatenate([a_s[g][pl.ds(k1, N2p, stride=P), :],
                                  a_s[g][pl.ds(N1 + k1, N2p, stride=P), :]], axis=0) for g in range(G)], axis=1)
            work.append((k1, _dot(m3_ref[k1], b.astype(BF16))))
        for k1, z in work:
            for g in range(G):
                y_s[g][pl.ds(pl.multiple_of(k1 * Q, SUBLANES), N2p), :] = z[:, lanes(g)]
        return carry

    lax.fori_loop(0, N1 // U2, p2, 0)

    def pair(g, k2):
        return jnp.concatenate([y_s[g][pl.ds(k2, N1, stride=Q), :],
                                y_s[g][pl.ds(k2 + 1, N1, stride=Q), :]], axis=0)

    for g in range(G):
        o_ref[0:2 * N1 - N_META, lanes(g)] = pair(g, 0)[N_META:].astype(BF16)

    def p3(t, carry):
        k2 = 2 * t + 2
        r0 = pl.multiple_of(k2 * N1 - N_META, 2 * SUBLANES)
        for g in range(G):
            o_ref[pl.ds(r0, 2 * N1), lanes(g)] = pair(g, k2).astype(BF16)
        return carry

    lax.fori_loop(0, N2 // 2 - 1, p3, 0)
    if N2 % 2:
        r0 = (N2 - 1) * N1 - N_META
        for g in range(G):
            o_ref[r0:r0 + N1, lanes(g)] = y_s[g][pl.ds(N2 - 1, N1, stride=Q), :].astype(BF16)


def _fmix(u, u_meta, *, groups, u1, u2, chunk=512):
    B, S, _ = u.shape
    dims, f1, m3, cd = _fmix_consts(S)
    Lp, N1, N2p, P, Q = dims["Lp"], dims["N1"], dims["N2p"], dims["P"], dims["Q"]
    W, G = FOURIER_GROUP_WIDTH, groups
    nb = (2 * LANES * 2) // (G * W)
    assert N2p % (nb * u1) == 0 and N1 % u2 == 0 and S % (chunk * (4 // G)) == 0
    const = lambda shape: pl.BlockSpec(shape, lambda b, g: (0,) * len(shape))
    scratch = ([pltpu.VMEM((Lp, W), F32)] * (2 * G) + [pltpu.VMEM((N2p * P, W), F32)] * G
               + [pltpu.VMEM((N1 * Q, W), F32)] * G)
    return pl.pallas_call(
        functools.partial(_fmix_kernel, S=S, chunk=chunk, G=G, NB=nb, U1=u1, U2=u2, **dims),
        grid=(B, N_FOURIER_GROUPS // G),
        in_specs=[pl.BlockSpec((None, S, G * W), lambda b, g: (b, 0, g)),
                  pl.BlockSpec((N_META, G * W), lambda b, g: (0, g)),
                  const(cd.shape), const(f1.shape), const(m3.shape)],
        out_specs=pl.BlockSpec((None, S, G * W), lambda b, g: (b, 0, g)),
        out_shape=jax.ShapeDtypeStruct((B, S, FOURIER_WIDTH), BF16),
        scratch_shapes=scratch,
        compiler_params=pltpu.CompilerParams(
            dimension_semantics=("parallel", "parallel"), vmem_limit_bytes=VMEM_LIMIT_BYTES),
        name="fmix",
    )(u, u_meta, jnp.asarray(cd).astype(BF16), jnp.asarray(f1).astype(BF16), jnp.asarray(m3).astype(BF16))


def _post_stages(x_ref, a_ref, f_ref, gate_ref, wao_ref, wf_ref, wout_ref, g2_ref, wup_ref, wdown_ref,
                 gfin_ref, y_ref, *, ff_chunk, sub):
    tm = x_ref.shape[0]
    blocks = [slice(r, r + sub) for r in range(0, tm, sub)]
    state = {}

    def mixer(rows):
        a = _dot(a_ref[rows, :], wao_ref[...])
        f = _dot(f_ref[rows, :], wf_ref[...])
        merged = (gate_ref[rows, :D_MODEL].astype(F32) * a + gate_ref[rows, D_MODEL:].astype(F32) * f).astype(BF16)
        h = x_ref[rows, :] + _dot(merged, wout_ref[...])
        state[rows.start] = (h, _rms(h, g2_ref[...]).astype(BF16))

    def mlp_chunk(c):
        for rows in blocks:
            h, n = state[rows.start]
            t = _dot(n, wup_ref[:, c * ff_chunk:(c + 1) * ff_chunk])
            r = jnp.square(jnp.maximum(t, 0.0)).astype(BF16)
            state[rows.start] = (h + _dot(r, wdown_ref[c * ff_chunk:(c + 1) * ff_chunk, :]), n)

    def final():
        for rows in blocks:
            y_ref[rows, :] = _rms(state[rows.start][0], gfin_ref[...])

    return ([functools.partial(mixer, rows) for rows in blocks]
            + [functools.partial(mlp_chunk, c) for c in range(D_FF // ff_chunk)] + [final])


def _post_kernel(*refs, ff_chunk, sub):
    for stage in _post_stages(*refs, ff_chunk=ff_chunk, sub=sub):
        stage()


def _post(x, a, f, gates, w_ao, w_f, w_out, g2, w_up, w_down, g_fin, *, tm, sub=512, ff_chunk=512):
    B, S, _ = x.shape
    tok = lambda w: pl.BlockSpec((None, tm, w), lambda b, i: (b, i, 0))
    const = lambda shape: pl.BlockSpec(shape, lambda b, i: (0,) * len(shape), pipeline_mode=pl.Buffered(1))
    return pl.pallas_call(
        functools.partial(_post_kernel, ff_chunk=ff_chunk, sub=sub),
        grid=(B, S // tm),
        in_specs=[tok(D_MODEL), tok(ATTN_WIDTH), tok(FOURIER_WIDTH), tok(N_BRANCHES * D_MODEL),
                  const(w_ao.shape), const(w_f.shape), const(w_out.shape), const((1, D_MODEL)),
                  const(w_up.shape), const(w_down.shape), const((1, D_MODEL))],
        out_specs=tok(D_MODEL),
        out_shape=jax.ShapeDtypeStruct((B, S, D_MODEL), F32),
        compiler_params=pltpu.CompilerParams(
            dimension_semantics=("parallel", "parallel"), vmem_limit_bytes=VMEM_LIMIT_BYTES),
        name="post",
    )(x, a, f, gates, w_ao, w_f, w_out, g2, w_up, w_down, g_fin)


def _rope_table(first_pos, n_pos):
    half = ROT_DIM // 2
    inv_freq = ROPE_THETA ** (-np.arange(half, dtype=np.float64) / half)
    ang = np.arange(first_pos, first_pos + n_pos, dtype=np.float64)[:, None] * inv_freq[None, :]
    cos, sin = np.cos(ang), np.sin(ang)
    rest = HEAD_DIM - ROT_DIM
    head = lambda a, b, fill: np.concatenate([a, b, np.full((n_pos, rest), fill)], axis=1)
    zero = np.zeros_like(sin)
    per_head = (head(cos, cos, 1.0), head(zero, sin, 0.0), head(-sin, zero, 0.0))
    table = np.concatenate([np.tile(t, (1, LANES // HEAD_DIM)) for t in per_head], axis=1)
    return jnp.asarray(table.astype(np.float32))


def _trunk(x, meta_parts, table, wts, *, tm_proj, tm, tq, fmix):
    k4_m, vt_m, u_m = meta_parts
    q, k4, vt, u, gates = _proj(x, table, wts["norm_mix_g"], wts["w_in"], wts["b_gate"], tm=tm_proj,
                                transpose_v=True)
    a = _attn(q, k4, vt, k4_m, vt_m, wts["attn_sink"], tq=tq)
    f = _fmix(u, u_m, **fmix)
    return _post(x, a, f, gates, wts["w_attn_out"], wts["w_fourier"], wts["w_out"], wts["norm_mlp_g"],
                 wts["w_mlp_up"], wts["w_mlp_down"], wts["norm_final_g"], tm=tm)


def kernel(x_prompt, x_sample, meta_tokens, norm_mix_g, w_in, b_gate, attn_sink, w_attn_out, w_fourier, w_out,
           norm_mlp_g, w_mlp_up, w_mlp_down, norm_final_g):
    assert w_in.shape[0] == 1, "single-layer trunk: meta-token outputs are never consumed"
    wts = dict(
        norm_mix_g=norm_mix_g[0][None, :], w_in=w_in[0].astype(BF16), b_gate=b_gate[0][None, :],
        attn_sink=attn_sink[0], w_attn_out=w_attn_out[0].astype(BF16), w_fourier=w_fourier[0].astype(BF16),
        w_out=w_out[0].astype(BF16), norm_mlp_g=norm_mlp_g[0][None, :], w_mlp_up=w_mlp_up[0].astype(BF16),
        w_mlp_down=w_mlp_down[0].astype(BF16), norm_final_g=norm_final_g[None, :])
    table = _rope_table(N_META, max(x_prompt.shape[1], x_sample.shape[1]))
    _, k4_m, v_m, u_m, _ = _proj(meta_tokens[None], _rope_table(0, N_META),
                                 wts["norm_mix_g"], wts["w_in"], wts["b_gate"], tm=N_META, transpose_v=False)
    vt_m = jnp.pad(v_m[0].T, ((0, 0), (0, LANES - N_META)))
    meta_parts = (k4_m[0], vt_m, u_m[0])
    tiles = dict(tm_proj=1024, tm=512, tq=2048)
    y_prompt = _trunk(x_prompt, meta_parts, table, wts, fmix=dict(groups=4, u1=12, u2=24), **tiles)
    y_sample = _trunk(x_sample, meta_parts, table, wts, fmix=dict(groups=1, u1=4, u2=24), **tiles)
    return (y_prompt, y_sample)
```

```python
import functools

import numpy as np
import jax
import jax.numpy as jnp
from jax import lax
from jax.experimental import pallas as pl
from jax.experimental.pallas import tpu as pltpu

D_MODEL = 1024
HEAD_DIM = 64
N_Q_HEADS = 8
N_KV_HEADS = 2
ATTN_WIDTH = N_Q_HEADS * HEAD_DIM
KV_WIDTH = N_KV_HEADS * HEAD_DIM
WINDOW = 128
ROPE_THETA = 500000.0
ROT_DIM = HEAD_DIM // 4
N_FOURIER_GROUPS = 4
FOURIER_GROUP_WIDTH = 128
FOURIER_WIDTH = N_FOURIER_GROUPS * FOURIER_GROUP_WIDTH
N_BRANCHES = 2
IN_WIDTH = ATTN_WIDTH + 2 * KV_WIDTH + FOURIER_WIDTH + N_BRANCHES * D_MODEL
D_FF = 4 * D_MODEL
N_META = 16
RMS_EPS = 1e-6
NEG_INF = -1e30
LOG2E = 1.4426950408889634

LANES = 128
SUBLANES = 8
VMEM_LIMIT_BYTES = 56 * 1024 * 1024

_Q0, _K0, _V0, _U0, _G0 = 0, ATTN_WIDTH, ATTN_WIDTH + KV_WIDTH, ATTN_WIDTH + 2 * KV_WIDTH, \
    ATTN_WIDTH + 2 * KV_WIDTH + FOURIER_WIDTH

BF16 = jnp.bfloat16
F32 = jnp.float32


def _dot(a, b):
    return jnp.dot(a, b, preferred_element_type=F32)


def _dot_nt(a, b):
    return lax.dot_general(a, b, (((1,), (1,)), ((), ())), preferred_element_type=F32)


def _rms(x, g):
    return x * lax.rsqrt(jnp.mean(x * x, axis=-1, keepdims=True) + RMS_EPS) * g


def _proj_kernel(x_ref, tab_ref, g_ref, w_ref, b_ref,
                 q_ref, k4_ref, v_ref, u_ref, gate_ref, *, transpose_v, sub):
    tm = x_ref.shape[0]
    blocks = [slice(r, r + sub) for r in range(0, tm, sub)]
    normed = {rows.start: _rms(x_ref[rows, :], g_ref[...]).astype(BF16) for rows in blocks}
    project = lambda rows, lo, hi: _dot(normed[rows.start], w_ref[:, lo:hi])

    def rope(z, rows):
        cos_t, sin_a, sin_b = (tab_ref[rows, c * LANES:(c + 1) * LANES] for c in range(3))
        return (z * cos_t + pltpu.roll(z, ROT_DIM // 2, 1) * sin_a
                + pltpu.roll(z, LANES - ROT_DIM // 2, 1) * sin_b)

    for rows in blocks:
        zq = project(rows, _Q0, _K0)
        for c in range(ATTN_WIDTH // LANES):
            q_ref[rows, c * LANES:(c + 1) * LANES] = (
                rope(zq[:, c * LANES:(c + 1) * LANES], rows) * (LOG2E * HEAD_DIM ** -0.5)).astype(BF16)
    for rows in blocks:
        zkv = project(rows, _K0, _U0)
        zk = rope(zkv[:, :KV_WIDTH], rows)
        zv = zkv[:, KV_WIDTH:]
        zks = pltpu.roll(zk, HEAD_DIM, 1)
        low = lax.broadcasted_iota(jnp.int32, zk.shape, 1) < HEAD_DIM
        for s, (keep_low, src) in enumerate(((True, zk), (False, zks), (True, zks), (False, zk))):
            k4_ref[rows, s * LANES:(s + 1) * LANES] = jnp.where(low == keep_low, src, 0.0).astype(BF16)
        if transpose_v:
            v_ref[:, rows] = zv.T.astype(BF16)
        else:
            v_ref[rows, :] = zv.astype(BF16)
    for rows in blocks:
        u_ref[rows, :] = project(rows, _U0, _G0).astype(BF16)
    for c in range(N_BRANCHES):
        lo = _G0 + c * D_MODEL
        for rows in blocks:
            g = project(rows, lo, lo + D_MODEL) + b_ref[:, c * D_MODEL:(c + 1) * D_MODEL]
            gate_ref[rows, c * D_MODEL:(c + 1) * D_MODEL] = jax.nn.sigmoid(g).astype(BF16)


def _proj(x, table, g, w_in, b_gate, *, tm, transpose_v):
    B, S, _ = x.shape
    tok = lambda w: pl.BlockSpec((None, tm, w), lambda b, i: (b, i, 0))
    tab = pl.BlockSpec((tm, 3 * LANES), lambda b, i: (i, 0))
    const = lambda shape: pl.BlockSpec(shape, lambda b, i: (0,) * len(shape))
    out_w = (ATTN_WIDTH, 4 * LANES, KV_WIDTH, FOURIER_WIDTH, N_BRANCHES * D_MODEL)
    out_specs = [tok(w) for w in out_w]
    out_shape = [jax.ShapeDtypeStruct((B, S, w), BF16) for w in out_w]
    if transpose_v:
        out_specs[2] = pl.BlockSpec((None, KV_WIDTH, tm), lambda b, i: (b, 0, i))
        out_shape[2] = jax.ShapeDtypeStruct((B, KV_WIDTH, S), BF16)
    return pl.pallas_call(
        functools.partial(_proj_kernel, transpose_v=transpose_v, sub=min(tm, 256)),
        grid=(B, S // tm),
        in_specs=[tok(D_MODEL), tab, const((1, D_MODEL)),
                  const((D_MODEL, IN_WIDTH)), const((1, N_BRANCHES * D_MODEL))],
        out_specs=out_specs,
        out_shape=out_shape,
        compiler_params=pltpu.CompilerParams(
            dimension_semantics=("parallel", "parallel"), vmem_limit_bytes=VMEM_LIMIT_BYTES),
        name="proj",
    )(x, table, g, w_in, b_gate)


def _attn_slots(i, sink_ref, q_ref, kc_ref, kp_ref, kn_ref, vc_ref, vp_ref, vn_ref, km_ref, vm_ref,
                o_ref, *, tq, seq):
    nblk = tq // WINDOW
    nband = 3 * WINDOW
    nkey = nband + N_META
    kall = jnp.concatenate([kp_ref[...], kc_ref[...], kn_ref[...]], axis=0)
    vall = jnp.concatenate([vp_ref[...], vc_ref[...], vn_ref[...]], axis=1)
    kmeta, vmeta = km_ref[...], vm_ref[...]
    key = lax.broadcasted_iota(jnp.int32, (WINDOW, 2 * LANES), 0)
    qry = jnp.bitwise_and(lax.broadcasted_iota(jnp.int32, (WINDOW, 2 * LANES), 1), WINDOW - 1)
    band_prev = jnp.where(key >= qry, 0.0, NEG_INF).astype(F32)
    band_next = jnp.where(key <= qry, 0.0, NEG_INF).astype(F32)
    first_pair = lax.broadcasted_iota(jnp.int32, (1, 2 * LANES), 1) < LANES
    p_pad = jnp.zeros((LANES - N_META, 2 * LANES), BF16)
    ones_all = jnp.ones((2 * SUBLANES, nband + LANES), BF16)
    heads_per_kv = N_Q_HEADS // N_KV_HEADS

    units = [(h, e) for h in range(N_KV_HEADS) for e in range(2)]

    def score(j, h, e):
        rows = slice(j * WINDOW, (j + 1) * WINDOW)
        blk0 = i * tq + j * WINDOW
        bias_prev = band_prev + jnp.where(blk0 >= WINDOW, 0.0, NEG_INF).astype(F32)
        bias_next = band_next + jnp.where(blk0 + WINDOW < seq, 0.0, NEG_INF).astype(F32)
        q2 = jnp.concatenate([q_ref[rows, (2 * h) * LANES:(2 * h + 1) * LANES],
                              q_ref[rows, (2 * h + 1) * LANES:(2 * h + 2) * LANES]], axis=0)
        sl = slice((2 * h + e) * LANES, (2 * h + e + 1) * LANES)
        keys = jnp.concatenate([kall[j * WINDOW:j * WINDOW + nband, sl], kmeta[:, sl]], axis=0)
        s = _dot_nt(keys, q2)
        return jnp.concatenate([s[:WINDOW] + bias_prev, s[WINDOW:2 * WINDOW],
                                s[2 * WINDOW:nband] + bias_next, s[nband:]], axis=0)

    def softmax(h, e, s):
        sink = LOG2E * jnp.where(first_pair, sink_ref[heads_per_kv * h + e], sink_ref[heads_per_kv * h + 2 + e])
        m = jnp.maximum(jnp.max(s, axis=0, keepdims=True), sink)
        return jnp.exp2(s - m).astype(BF16), jnp.exp2(sink - m)

    def value(j, h, pb, p_sink):
        v_all = jnp.concatenate([vall[h * HEAD_DIM:(h + 1) * HEAD_DIM, j * WINDOW:j * WINDOW + nband],
                                 vmeta[h * HEAD_DIM:(h + 1) * HEAD_DIM, :]], axis=1)
        v_all = jnp.concatenate([v_all, ones_all], axis=0)
        o = _dot(v_all, jnp.concatenate([pb, p_pad], axis=0))
        denom = o[HEAD_DIM:HEAD_DIM + 1] + p_sink
        return o[:HEAD_DIM] * (1.0 / denom)

    def write(j, h, halves):
        rows = slice(j * WINDOW, (j + 1) * WINDOW)
        o_t = jnp.concatenate(halves, axis=0)
        for pp in range(2):
            o_ref[rows, (2 * h + pp) * LANES:(2 * h + pp + 1) * LANES] = (
                o_t[:, pp * LANES:(pp + 1) * LANES].T.astype(BF16))

    scores, probs, outs = {}, {}, {}

    def slot(t):
        for h, e in units:
            if 0 <= t - 2 < nblk:
                outs[h, e] = value(t - 2, h, *probs.pop((t - 2, h, e)))
                if e == 1:
                    write(t - 2, h, [outs.pop((h, 0)), outs.pop((h, 1))])
            if t < nblk:
                scores[t, h, e] = score(t, h, e)
            if 0 <= t - 1 < nblk:
                probs[t - 1, h, e] = softmax(h, e, scores.pop((t - 1, h, e)))

    return [functools.partial(slot, t) for t in range(nblk + 2)]


def _attn_kernel(*refs, tq, seq):
    for slot in _attn_slots(pl.program_id(1), *refs, tq=tq, seq=seq):
        slot()


def _attn(q, k4, vt, k4_meta, vt_meta, sink, *, tq):
    B, S, _ = q.shape
    r = tq // WINDOW
    last = S // WINDOW - 1
    prev_idx = lambda i: jnp.maximum(i * r - 1, 0)
    next_idx = lambda i: jnp.minimum((i + 1) * r, last)
    cur = lambda w: pl.BlockSpec((None, tq, w), lambda b, i: (b, i, 0))
    kw = 4 * LANES
    return pl.pallas_call(
        functools.partial(_attn_kernel, tq=tq, seq=S),
        grid=(B, S // tq),
        in_specs=[pl.BlockSpec(memory_space=pltpu.SMEM), cur(ATTN_WIDTH),
                  cur(kw),
                  pl.BlockSpec((None, WINDOW, kw), lambda b, i: (b, prev_idx(i), 0)),
                  pl.BlockSpec((None, WINDOW, kw), lambda b, i: (b, next_idx(i), 0)),
                  pl.BlockSpec((None, KV_WIDTH, tq), lambda b, i: (b, 0, i)),
                  pl.BlockSpec((None, KV_WIDTH, WINDOW), lambda b, i: (b, 0, prev_idx(i))),
                  pl.BlockSpec((None, KV_WIDTH, WINDOW), lambda b, i: (b, 0, next_idx(i))),
                  pl.BlockSpec((N_META, kw), lambda b, i: (0, 0)),
                  pl.BlockSpec((KV_WIDTH, LANES), lambda b, i: (0, 0))],
        out_specs=cur(ATTN_WIDTH),
        out_shape=jax.ShapeDtypeStruct((B, S, ATTN_WIDTH), BF16),
        compiler_params=pltpu.CompilerParams(
            dimension_semantics=("parallel", "parallel"), vmem_limit_bytes=VMEM_LIMIT_BYTES),
        name="attn",
    )(sink, q, k4, k4, k4, vt, vt, vt, k4_meta, vt_meta)


MXU_DEPTH = 256


def _factor(L):
    best = None
    for n1 in range(SUBLANES, L, SUBLANES):
        if L % n1 or (L // n1) % 8 == 0:
            continue
        n2p = _round_up(L // n1, 2 * SUBLANES)
        cost = (-(-2 * n1 // MXU_DEPTH) + -(-2 * n2p // MXU_DEPTH), abs(n1 - L ** 0.5))
        if best is None or cost < best[0]:
            best = (cost, n1)
    return best[1], L // best[1]


def _round_up(x, m):
    return (x + m - 1) // m * m


def _odd_tiles(rows):
    p = _round_up(rows, SUBLANES)
    return p if (p // SUBLANES) % 2 else p + SUBLANES


def _fmix_consts(S):
    L = S + N_META
    N1, N2 = _factor(L)
    N2p = _round_up(N2, 2 * SUBLANES)
    dims = dict(L=L, N1=N1, N2=N2, N2p=N2p, P=_odd_tiles(2 * N1), Q=_odd_tiles(N2p),
                Lp=_round_up(L + N2p - N2, SUBLANES))
    k1 = np.arange(N1, dtype=np.int64)
    ang1 = 2.0 * np.pi * ((k1[:, None] * k1[None, :]) % N1) / N1
    c1, s1 = np.cos(ang1) / np.sqrt(N1), np.sin(ang1) / np.sqrt(N1)
    f1 = np.block([[c1, s1], [-s1, c1]]).astype(np.float32)
    k = k1[:, None, None] + N1 * np.arange(N2p, dtype=np.int64)[None, :, None]
    n2 = np.arange(N2p, dtype=np.int64)[None, None, :]
    ang3 = 2.0 * np.pi * (((n2 + N_META) * k) % L) / L
    valid = ((np.arange(N2p) < N2)[None, :, None] & (np.arange(N2p) < N2)[None, None, :])
    c3 = np.where(valid, np.cos(ang3), 0.0) / np.sqrt(N2)
    s3 = np.where(valid, np.sin(ang3), 0.0) / np.sqrt(N2)
    m3 = np.concatenate([c3, s3], axis=2).astype(np.float32)
    c = np.arange(FOURIER_GROUP_WIDTH, dtype=np.int64)
    angc = 2.0 * np.pi * ((c[:, None] * c[None, :]) % FOURIER_GROUP_WIDTH) / FOURIER_GROUP_WIDTH
    cd = (np.concatenate([np.cos(angc), -np.sin(angc)], axis=1)
          / np.sqrt(FOURIER_GROUP_WIDTH)).astype(np.float32)
    return dims, f1, m3, cd


def _fmix_kernel(u_ref, um_ref, cd_ref, f1_ref, m3_ref, o_ref, *scratch,
                 S, L, Lp, N1, N2, N2p, P, Q, G, NB, U1, U2, chunk):
    W = FOURIER_GROUP_WIDTH
    xr, xi, a_s, y_s = (scratch[i * G:(i + 1) * G] for i in range(4))
    cd = cd_ref[...]
    lanes = lambda g: slice(g * W, (g + 1) * W)

    per_iter = 4 // G
    def p0(c, carry):
        for uu in range(per_iter):
            r0 = pl.multiple_of((c * per_iter + uu) * chunk, chunk)
            for g in range(G):
                v = _dot(u_ref[pl.ds(r0, chunk), lanes(g)], cd)
                xr[g][pl.ds(r0, chunk), :] = v[:, :W]
                xi[g][pl.ds(r0, chunk), :] = v[:, W:]
        return carry

    lax.fori_loop(0, S // (chunk * per_iter), p0, 0)
    for g in range(G):
        vm = _dot(um_ref[:, lanes(g)], cd)
        xr[g][S:L, :] = vm[:, :W]
        xi[g][S:L, :] = vm[:, W:]
        xr[g][L:Lp, :] = jnp.zeros((Lp - L, W), F32)
        xi[g][L:Lp, :] = jnp.zeros((Lp - L, W), F32)

    def p1(t, carry):
        work = []
        for uu in range(U1):
            n2s = [(t * U1 + uu) * NB + j for j in range(NB)]
            cols = [jnp.concatenate([xr[g][pl.ds(n2, N1, stride=N2), :],
                                     xi[g][pl.ds(n2, N1, stride=N2), :]], axis=0)
                    for n2 in n2s for g in range(G)]
            work.append((n2s, _dot(f1_ref[...], jnp.concatenate(cols, axis=1).astype(BF16))))
        for n2s, a in work:
            for j, n2 in enumerate(n2s):
                for g in range(G):
                    a_s[g][pl.ds(pl.multiple_of(n2 * P, SUBLANES), 2 * N1), :] = a[:, lanes(j * G + g)]
        return carry

    lax.fori_loop(0, N2p // (NB * U1), p1, 0)

    def p2(t, carry):
        work = []
        for uu in range(U2):
            k1 = t * U2 + uu
            b = jnp.concatenate(
                [jnp.concatenate([a_s[g][pl.ds(k1, N2p, stride=P), :],
                                  a_s[g][pl.ds(N1 + k1, N2p, stride=P), :]], axis=0) for g in range(G)], axis=1)
            work.append((k1, _dot(m3_ref[k1], b.astype(BF16))))
        for k1, z in work:
            for g in range(G):
                y_s[g][pl.ds(pl.multiple_of(k1 * Q, SUBLANES), N2p), :] = z[:, lanes(g)]
        return carry

    lax.fori_loop(0, N1 // U2, p2, 0)

    def pair(g, k2):
        return jnp.concatenate([y_s[g][pl.ds(k2, N1, stride=Q), :],
                                y_s[g][pl.ds(k2 + 1, N1, stride=Q), :]], axis=0)

    for g in range(G):
        o_ref[0:2 * N1 - N_META, lanes(g)] = pair(g, 0)[N_META:].astype(BF16)

    def p3(t, carry):
        k2 = 2 * t + 2
        r0 = pl.multiple_of(k2 * N1 - N_META, 2 * SUBLANES)
        for g in range(G):
            o_ref[pl.ds(r0, 2 * N1), lanes(g)] = pair(g, k2).astype(BF16)
        return carry

    lax.fori_loop(0, N2 // 2 - 1, p3, 0)
    if N2 % 2:
        r0 = (N2 - 1) * N1 - N_META
        for g in range(G):
            o_ref[r0:r0 + N1, lanes(g)] = y_s[g][pl.ds(N2 - 1, N1, stride=Q), :].astype(BF16)


def _fmix(u, u_meta, *, groups, u1, u2, chunk=512):
    B, S, _ = u.shape
    dims, f1, m3, cd = _fmix_consts(S)
    Lp, N1, N2p, P, Q = dims["Lp"], dims["N1"], dims["N2p"], dims["P"], dims["Q"]
    W, G = FOURIER_GROUP_WIDTH, groups
    nb = (2 * LANES * 2) // (G * W)
    assert N2p % (nb * u1) == 0 and N1 % u2 == 0 and S % (chunk * (4 // G)) == 0
    const = lambda shape: pl.BlockSpec(shape, lambda b, g: (0,) * len(shape))
    scratch = ([pltpu.VMEM((Lp, W), F32)] * (2 * G) + [pltpu.VMEM((N2p * P, W), F32)] * G
               + [pltpu.VMEM((N1 * Q, W), F32)] * G)
    return pl.pallas_call(
        functools.partial(_fmix_kernel, S=S, chunk=chunk, G=G, NB=nb, U1=u1, U2=u2, **dims),
        grid=(B, N_FOURIER_GROUPS // G),
        in_specs=[pl.BlockSpec((None, S, G * W), lambda b, g: (b, 0, g)),
                  pl.BlockSpec((N_META, G * W), lambda b, g: (0, g)),
                  const(cd.shape), const(f1.shape), const(m3.shape)],
        out_specs=pl.BlockSpec((None, S, G * W), lambda b, g: (b, 0, g)),
        out_shape=jax.ShapeDtypeStruct((B, S, FOURIER_WIDTH), BF16),
        scratch_shapes=scratch,
        compiler_params=pltpu.CompilerParams(
            dimension_semantics=("parallel", "parallel"), vmem_limit_bytes=VMEM_LIMIT_BYTES),
        name="fmix",
    )(u, u_meta, jnp.asarray(cd).astype(BF16), jnp.asarray(f1).astype(BF16), jnp.asarray(m3).astype(BF16))


def _post_stages(x_ref, a_ref, f_ref, gate_ref, wao_ref, wf_ref, wout_ref, g2_ref, wup_ref, wdown_ref,
                 gfin_ref, y_ref, *, ff_chunk, sub):
    tm = x_ref.shape[0]
    blocks = [slice(r, r + sub) for r in range(0, tm, sub)]
    state = {}

    def mixer(rows):
        a = _dot(a_ref[rows, :], wao_ref[...])
        f = _dot(f_ref[rows, :], wf_ref[...])
        merged = (gate_ref[rows, :D_MODEL].astype(F32) * a + gate_ref[rows, D_MODEL:].astype(F32) * f).astype(BF16)
        h = x_ref[rows, :] + _dot(merged, wout_ref[...])
        state[rows.start] = (h, _rms(h, g2_ref[...]).astype(BF16))

    def mlp_chunk(c):
        for rows in blocks:
            h, n = state[rows.start]
            t = _dot(n, wup_ref[:, c * ff_chunk:(c + 1) * ff_chunk])
            r = jnp.square(jnp.maximum(t, 0.0)).astype(BF16)
            state[rows.start] = (h + _dot(r, wdown_ref[c * ff_chunk:(c + 1) * ff_chunk, :]), n)

    def final():
        for rows in blocks:
            y_ref[rows, :] = _rms(state[rows.start][0], gfin_ref[...])

    return ([functools.partial(mixer, rows) for rows in blocks]
            + [functools.partial(mlp_chunk, c) for c in range(D_FF // ff_chunk)] + [final])


def _post_kernel(*refs, ff_chunk, sub):
    for stage in _post_stages(*refs, ff_chunk=ff_chunk, sub=sub):
        stage()


def _post(x, a, f, gates, w_ao, w_f, w_out, g2, w_up, w_down, g_fin, *, tm, sub=512, ff_chunk=512):
    B, S, _ = x.shape
    tok = lambda w: pl.BlockSpec((None, tm, w), lambda b, i: (b, i, 0))
    const = lambda shape: pl.BlockSpec(shape, lambda b, i: (0,) * len(shape), pipeline_mode=pl.Buffered(1))
    return pl.pallas_call(
        functools.partial(_post_kernel, ff_chunk=ff_chunk, sub=sub),
        grid=(B, S // tm),
        in_specs=[tok(D_MODEL), tok(ATTN_WIDTH), tok(FOURIER_WIDTH), tok(N_BRANCHES * D_MODEL),
                  const(w_ao.shape), const(w_f.shape), const(w_out.shape), const((1, D_MODEL)),
                  const(w_up.shape), const(w_down.shape), const((1, D_MODEL))],
        out_specs=tok(D_MODEL),
        out_shape=jax.ShapeDtypeStruct((B, S, D_MODEL), F32),
        compiler_params=pltpu.CompilerParams(
            dimension_semantics=("parallel", "parallel"), vmem_limit_bytes=VMEM_LIMIT_BYTES),
        name="post",
    )(x, a, f, gates, w_ao, w_f, w_out, g2, w_up, w_down, g_fin)


def _rope_table(first_pos, n_pos):
    half = ROT_DIM // 2
    inv_freq = ROPE_THETA ** (-np.arange(half, dtype=np.float64) / half)
    ang = np.arange(first_pos, first_pos + n_pos, dtype=np.float64)[:, None] * inv_freq[None, :]
    cos, sin = np.cos(ang), np.sin(ang)
    rest = HEAD_DIM - ROT_DIM
    head = lambda a, b, fill: np.concatenate([a, b, np.full((n_pos, rest), fill)], axis=1)
    zero = np.zeros_like(sin)
    per_head = (head(cos, cos, 1.0), head(zero, sin, 0.0), head(-sin, zero, 0.0))
    table = np.concatenate([np.tile(t, (1, LANES // HEAD_DIM)) for t in per_head], axis=1)
    return jnp.asarray(table.astype(np.float32))


def _trunk(x, meta_parts, table, wts, *, tm_proj, tm, tq, fmix):
    k4_m, vt_m, u_m = meta_parts
    q, k4, vt, u, gates = _proj(x, table, wts["norm_mix_g"], wts["w_in"], wts["b_gate"], tm=tm_proj,
                                transpose_v=True)
    a = _attn(q, k4, vt, k4_m, vt_m, wts["attn_sink"], tq=tq)
    f = _fmix(u, u_m, **fmix)
    return _post(x, a, f, gates, wts["w_attn_out"], wts["w_fourier"], wts["w_out"], wts["norm_mlp_g"],
                 wts["w_mlp_up"], wts["w_mlp_down"], wts["norm_final_g"], tm=tm)


def kernel(x_prompt, x_sample, meta_tokens, norm_mix_g, w_in, b_gate, attn_sink, w_attn_out, w_fourier, w_out,
           norm_mlp_g, w_mlp_up, w_mlp_down, norm_final_g):
    assert w_in.shape[0] == 1, "single-layer trunk: meta-token outputs are never consumed"
    wts = dict(
        norm_mix_g=norm_mix_g[0][None, :], w_in=w_in[0].astype(BF16), b_gate=b_gate[0][None, :],
        attn_sink=attn_sink[0], w_attn_out=w_attn_out[0].astype(BF16), w_fourier=w_fourier[0].astype(BF16),
        w_out=w_out[0].astype(BF16), norm_mlp_g=norm_mlp_g[0][None, :], w_mlp_up=w_mlp_up[0].astype(BF16),
        w_mlp_down=w_mlp_down[0].astype(BF16), norm_final_g=norm_final_g[None, :])
    table = _rope_table(N_META, max(x_prompt.shape[1], x_sample.shape[1]))
    _, k4_m, v_m, u_m, _ = _proj(meta_tokens[None], _rope_table(0, N_META),
                                 wts["norm_mix_g"], wts["w_in"], wts["b_gate"], tm=N_META, transpose_v=False)
    vt_m = jnp.pad(v_m[0].T, ((0, 0), (0, LANES - N_META)))
    meta_parts = (k4_m[0], vt_m, u_m[0])
    tiles = dict(tm_proj=1024, tm=512, tq=2048)
    y_prompt = _trunk(x_prompt, meta_parts, table, wts, fmix=dict(groups=4, u1=12, u2=24), **tiles)
    y_sample = _trunk(x_sample, meta_parts, table, wts, fmix=dict(groups=1, u1=4, u2=24), **tiles)
    return (y_prompt, y_sample)
```

```python
import functools

import numpy as np
import jax
import jax.numpy as jnp
from jax import lax
from jax.experimental import pallas as pl
from jax.experimental.pallas import tpu as pltpu

D_MODEL = 1024
HEAD_DIM = 64
N_Q_HEADS = 8
N_KV_HEADS = 2
ATTN_WIDTH = N_Q_HEADS * HEAD_DIM
KV_WIDTH = N_KV_HEADS * HEAD_DIM
WINDOW = 128
ROPE_THETA = 500000.0
ROT_DIM = HEAD_DIM // 4
N_FOURIER_GROUPS = 4
FOURIER_GROUP_WIDTH = 128
FOURIER_WIDTH = N_FOURIER_GROUPS * FOURIER_GROUP_WIDTH
N_BRANCHES = 2
IN_WIDTH = ATTN_WIDTH + 2 * KV_WIDTH + FOURIER_WIDTH + N_BRANCHES * D_MODEL
D_FF = 4 * D_MODEL
N_META = 16
RMS_EPS = 1e-6
NEG_INF = -1e30
LOG2E = 1.4426950408889634

LANES = 128
SUBLANES = 8
VMEM_LIMIT_BYTES = 56 * 1024 * 1024

_Q0, _K0, _V0, _U0, _G0 = 0, ATTN_WIDTH, ATTN_WIDTH + KV_WIDTH, ATTN_WIDTH + 2 * KV_WIDTH, \
    ATTN_WIDTH + 2 * KV_WIDTH + FOURIER_WIDTH

BF16 = jnp.bfloat16
F32 = jnp.float32


def _dot(a, b):
    return jnp.dot(a, b, preferred_element_type=F32)


def _dot_nt(a, b):
    return lax.dot_general(a, b, (((1,), (1,)), ((), ())), preferred_element_type=F32)


def _rms(x, g):
    return x * lax.rsqrt(jnp.mean(x * x, axis=-1, keepdims=True) + RMS_EPS) * g


def _proj_kernel(x_ref, tab_ref, g_ref, w_ref, b_ref,
                 q_ref, k4_ref, v_ref, u_ref, gate_ref, *, transpose_v, sub):
    tm = x_ref.shape[0]
    blocks = [slice(r, r + sub) for r in range(0, tm, sub)]
    normed = {rows.start: _rms(x_ref[rows, :], g_ref[...]).astype(BF16) for rows in blocks}
    project = lambda rows, lo, hi: _dot(normed[rows.start], w_ref[:, lo:hi])

    def rope(z, rows):
        cos_t, sin_a, sin_b = (tab_ref[rows, c * LANES:(c + 1) * LANES] for c in range(3))
        return (z * cos_t + pltpu.roll(z, ROT_DIM // 2, 1) * sin_a
                + pltpu.roll(z, LANES - ROT_DIM // 2, 1) * sin_b)

    for rows in blocks:
        zq = project(rows, _Q0, _K0)
        for c in range(ATTN_WIDTH // LANES):
            q_ref[rows, c * LANES:(c + 1) * LANES] = (
                rope(zq[:, c * LANES:(c + 1) * LANES], rows) * (LOG2E * HEAD_DIM ** -0.5)).astype(BF16)
    for rows in blocks:
        zkv = project(rows, _K0, _U0)
        zk = rope(zkv[:, :KV_WIDTH], rows)
        zv = zkv[:, KV_WIDTH:]
        zks = pltpu.roll(zk, HEAD_DIM, 1)
        low = lax.broadcasted_iota(jnp.int32, zk.shape, 1) < HEAD_DIM
        for s, (keep_low, src) in enumerate(((True, zk), (False, zks), (True, zks), (False, zk))):
            k4_ref[rows, s * LANES:(s + 1) * LANES] = jnp.where(low == keep_low, src, 0.0).astype(BF16)
        if transpose_v:
            v_ref[:, rows] = zv.T.astype(BF16)
        else:
            v_ref[rows, :] = zv.astype(BF16)
    for rows in blocks:
        u_ref[rows, :] = project(rows, _U0, _G0).astype(BF16)
    for c in range(N_BRANCHES):
        lo = _G0 + c * D_MODEL
        for rows in blocks:
            g = project(rows, lo, lo + D_MODEL) + b_ref[:, c * D_MODEL:(c + 1) * D_MODEL]
            gate_ref[rows, c * D_MODEL:(c + 1) * D_MODEL] = jax.nn.sigmoid(g).astype(BF16)


def _proj(x, table, g, w_in, b_gate, *, tm, transpose_v):
    B, S, _ = x.shape
    tok = lambda w: pl.BlockSpec((None, tm, w), lambda b, i: (b, i, 0))
    tab = pl.BlockSpec((tm, 3 * LANES), lambda b, i: (i, 0))
    const = lambda shape: pl.BlockSpec(shape, lambda b, i: (0,) * len(shape))
    out_w = (ATTN_WIDTH, 4 * LANES, KV_WIDTH, FOURIER_WIDTH, N_BRANCHES * D_MODEL)
    out_specs = [tok(w) for w in out_w]
    out_shape = [jax.ShapeDtypeStruct((B, S, w), BF16) for w in out_w]
    if transpose_v:
        out_specs[2] = pl.BlockSpec((None, KV_WIDTH, tm), lambda b, i: (b, 0, i))
        out_shape[2] = jax.ShapeDtypeStruct((B, KV_WIDTH, S), BF16)
    return pl.pallas_call(
        functools.partial(_proj_kernel, transpose_v=transpose_v, sub=min(tm, 256)),
        grid=(B, S // tm),
        in_specs=[tok(D_MODEL), tab, const((1, D_MODEL)),
                  const((D_MODEL, IN_WIDTH)), const((1, N_BRANCHES * D_MODEL))],
        out_specs=out_specs,
        out_shape=out_shape,
        compiler_params=pltpu.CompilerParams(
            dimension_semantics=("parallel", "parallel"), vmem_limit_bytes=VMEM_LIMIT_BYTES),
        name="proj",
    )(x, table, g, w_in, b_gate)


def _attn_slots(i, sink_ref, q_ref, kc_ref, kp_ref, kn_ref, vc_ref, vp_ref, vn_ref, km_ref, vm_ref,
                o_ref, *, tq, seq):
    nblk = tq // WINDOW
    nband = 3 * WINDOW
    nkey = nband + N_META
    kall = jnp.concatenate([kp_ref[...], kc_ref[...], kn_ref[...]], axis=0)
    vall = jnp.concatenate([vp_ref[...], vc_ref[...], vn_ref[...]], axis=1)
    kmeta, vmeta = km_ref[...], vm_ref[...]
    key = lax.broadcasted_iota(jnp.int32, (WINDOW, 2 * LANES), 0)
    qry = jnp.bitwise_and(lax.broadcasted_iota(jnp.int32, (WINDOW, 2 * LANES), 1), WINDOW - 1)
    band_prev = jnp.where(key >= qry, 0.0, NEG_INF).astype(F32)
    band_next = jnp.where(key <= qry, 0.0, NEG_INF).astype(F32)
    first_pair = lax.broadcasted_iota(jnp.int32, (1, 2 * LANES), 1) < LANES
    p_pad = jnp.zeros((LANES - N_META, 2 * LANES), BF16)
    ones_all = jnp.ones((2 * SUBLANES, nband + LANES), BF16)
    heads_per_kv = N_Q_HEADS // N_KV_HEADS

    units = [(h, e) for h in range(N_KV_HEADS) for e in range(2)]

    def score(j, h, e):
        rows = slice(j * WINDOW, (j + 1) * WINDOW)
        blk0 = i * tq + j * WINDOW
        bias_prev = band_prev + jnp.where(blk0 >= WINDOW, 0.0, NEG_INF).astype(F32)
        bias_next = band_next + jnp.where(blk0 + WINDOW < seq, 0.0, NEG_INF).astype(F32)
        q2 = jnp.concatenate([q_ref[rows, (2 * h) * LANES:(2 * h + 1) * LANES],
                              q_ref[rows, (2 * h + 1) * LANES:(2 * h + 2) * LANES]], axis=0)
        sl = slice((2 * h + e) * LANES, (2 * h + e + 1) * LANES)
        keys = jnp.concatenate([kall[j * WINDOW:j * WINDOW + nband, sl], kmeta[:, sl]], axis=0)
        s = _dot_nt(keys, q2)
        return jnp.concatenate([s[:WINDOW] + bias_prev, s[WINDOW:2 * WINDOW],
                                s[2 * WINDOW:nband] + bias_next, s[nband:]], axis=0)

    def softmax(h, e, s):
        sink = LOG2E * jnp.where(first_pair, sink_ref[heads_per_kv * h + e], sink_ref[heads_per_kv * h + 2 + e])
        m = jnp.maximum(jnp.max(s, axis=0, keepdims=True), sink)
        return jnp.exp2(s - m).astype(BF16), jnp.exp2(sink - m)

    def value(j, h, pb, p_sink):
        v_all = jnp.concatenate([vall[h * HEAD_DIM:(h + 1) * HEAD_DIM, j * WINDOW:j * WINDOW + nband],
                                 vmeta[h * HEAD_DIM:(h + 1) * HEAD_DIM, :]], axis=1)
        v_all = jnp.concatenate([v_all, ones_all], axis=0)
        o = _dot(v_all, jnp.concatenate([pb, p_pad], axis=0))
        denom = o[HEAD_DIM:HEAD_DIM + 1] + p_sink
        return o[:HEAD_DIM] * (1.0 / denom)

    def write(j, h, halves):
        rows = slice(j * WINDOW, (j + 1) * WINDOW)
        o_t = jnp.concatenate(halves, axis=0)
        for pp in range(2):
            o_ref[rows, (2 * h + pp) * LANES:(2 * h + pp + 1) * LANES] = (
                o_t[:, pp * LANES:(pp + 1) * LANES].T.astype(BF16))

    scores, probs, outs = {}, {}, {}

    def slot(t):
        for h, e in units:
            if 0 <= t - 2 < nblk:
                outs[h, e] = value(t - 2, h, *probs.pop((t - 2, h, e)))
                if e == 1:
                    write(t - 2, h, [outs.pop((h, 0)), outs.pop((h, 1))])
            if t < nblk:
                scores[t, h, e] = score(t, h, e)
            if 0 <= t - 1 < nblk:
                probs[t - 1, h, e] = softmax(h, e, scores.pop((t - 1, h, e)))

    return [functools.partial(slot, t) for t in range(nblk + 2)]


def _attn_kernel(*refs, tq, seq):
    for slot in _attn_slots(pl.program_id(1), *refs, tq=tq, seq=seq):
        slot()


def _attn(q, k4, vt, k4_meta, vt_meta, sink, *, tq):
    B, S, _ = q.shape
    r = tq // WINDOW
    last = S // WINDOW - 1
    prev_idx = lambda i: jnp.maximum(i * r - 1, 0)
    next_idx = lambda i: jnp.minimum((i + 1) * r, last)
    cur = lambda w: pl.BlockSpec((None, tq, w), lambda b, i: (b, i, 0))
    kw = 4 * LANES
    return pl.pallas_call(
        functools.partial(_attn_kernel, tq=tq, seq=S),
        grid=(B, S // tq),
        in_specs=[pl.BlockSpec(memory_space=pltpu.SMEM), cur(ATTN_WIDTH),
                  cur(kw),
                  pl.BlockSpec((None, WINDOW, kw), lambda b, i: (b, prev_idx(i), 0)),
                  pl.BlockSpec((None, WINDOW, kw), lambda b, i: (b, next_idx(i), 0)),
                  pl.BlockSpec((None, KV_WIDTH, tq), lambda b, i: (b, 0, i)),
                  pl.BlockSpec((None, KV_WIDTH, WINDOW), lambda b, i: (b, 0, prev_idx(i))),
                  pl.BlockSpec((None, KV_WIDTH, WINDOW), lambda b, i: (b, 0, next_idx(i))),
                  pl.BlockSpec((N_META, kw), lambda b, i: (0, 0)),
                  pl.BlockSpec((KV_WIDTH, LANES), lambda b, i: (0, 0))],
        out_specs=cur(ATTN_WIDTH),
        out_shape=jax.ShapeDtypeStruct((B, S, ATTN_WIDTH), BF16),
        compiler_params=pltpu.CompilerParams(
            dimension_semantics=("parallel", "parallel"), vmem_limit_bytes=VMEM_LIMIT_BYTES),
        name="attn",
    )(sink, q, k4, k4, k4, vt, vt, vt, k4_meta, vt_meta)


MXU_DEPTH = 256


def _factor(L):
    best = None
    for n1 in range(SUBLANES, L, SUBLANES):
        if L % n1 or (L // n1) % 8 == 0:
            continue
        n2p = _round_up(L // n1, 2 * SUBLANES)
        cost = (-(-2 * n1 // MXU_DEPTH) + -(-2 * n2p // MXU_DEPTH), abs(n1 - L ** 0.5))
        if best is None or cost < best[0]:
            best = (cost, n1)
    return best[1], L // best[1]


def _round_up(x, m):
    return (x + m - 1) // m * m


def _odd_tiles(rows):
    p = _round_up(rows, SUBLANES)
    return p if (p // SUBLANES) % 2 else p + SUBLANES


def _fmix_consts(S):
    L = S + N_META
    N1, N2 = _factor(L)
    N2p = _round_up(N2, 2 * SUBLANES)
    dims = dict(L=L, N1=N1, N2=N2, N2p=N2p, P=_odd_tiles(2 * N1), Q=_odd_tiles(N2p),
                Lp=_round_up(L + N2p - N2, SUBLANES))
    k1 = np.arange(N1, dtype=np.int64)
    ang1 = 2.0 * np.pi * ((k1[:, None] * k1[None, :]) % N1) / N1
    c1, s1 = np.cos(ang1) / np.sqrt(N1), np.sin(ang1) / np.sqrt(N1)
    f1 = np.block([[c1, s1], [-s1, c1]]).astype(np.float32)
    k = k1[:, None, None] + N1 * np.arange(N2p, dtype=np.int64)[None, :, None]
    n2 = np.arange(N2p, dtype=np.int64)[None, None, :]
    ang3 = 2.0 * np.pi * (((n2 + N_META) * k) % L) / L
    valid = ((np.arange(N2p) < N2)[None, :, None] & (np.arange(N2p) < N2)[None, None, :])
    c3 = np.where(valid, np.cos(ang3), 0.0) / np.sqrt(N2)
    s3 = np.where(valid, np.sin(ang3), 0.0) / np.sqrt(N2)
    m3 = np.concatenate([c3, s3], axis=2).astype(np.float32)
    c = np.arange(FOURIER_GROUP_WIDTH, dtype=np.int64)
    angc = 2.0 * np.pi * ((c[:, None] * c[None, :]) % FOURIER_GROUP_WIDTH) / FOURIER_GROUP_WIDTH
    cd = (np.concatenate([np.cos(angc), -np.sin(angc)], axis=1)
          / np.sqrt(FOURIER_GROUP_WIDTH)).astype(np.float32)
    return dims, f1, m3, cd


def _fmix_kernel(u_ref, um_ref, cd_ref, f1_ref, m3_ref, o_ref, *scratch,
                 S, L, Lp, N1, N2, N2p, P, Q, G, NB, U1, U2, chunk):
    W = FOURIER_GROUP_WIDTH
    xr, xi, a_s, y_s = (scratch[i * G:(i + 1) * G] for i in range(4))
    cd = cd_ref[...]
    lanes = lambda g: slice(g * W, (g + 1) * W)

    for r0 in range(0, S, chunk):
        for g in range(G):
            v = _dot(u_ref[r0:r0 + chunk, lanes(g)], cd)
            xr[g][r0:r0 + chunk, :] = v[:, :W]
            xi[g][r0:r0 + chunk, :] = v[:, W:]
    for g in range(G):
        vm = _dot(um_ref[:, lanes(g)], cd)
        xr[g][S:L, :] = vm[:, :W]
        xi[g][S:L, :] = vm[:, W:]
        xr[g][L:Lp, :] = jnp.zeros((Lp - L, W), F32)
        xi[g][L:Lp, :] = jnp.zeros((Lp - L, W), F32)

    def p1(t, carry):
        work = []
        for uu in range(U1):
            n2s = [(t * U1 + uu) * NB + j for j in range(NB)]
            cols = [jnp.concatenate([xr[g][pl.ds(n2, N1, stride=N2), :],
                                     xi[g][pl.ds(n2, N1, stride=N2), :]], axis=0)
                    for n2 in n2s for g in range(G)]
            work.append((n2s, _dot(f1_ref[...], jnp.concatenate(cols, axis=1).astype(BF16))))
        for n2s, a in work:
            for j, n2 in enumerate(n2s):
                for g in range(G):
                    a_s[g][pl.ds(pl.multiple_of(n2 * P, SUBLANES), 2 * N1), :] = a[:, lanes(j * G + g)]
        return carry

    lax.fori_loop(0, N2p // (NB * U1), p1, 0)

    def p2(t, carry):
        work = []
        for uu in range(U2):
            k1 = t * U2 + uu
            b = jnp.concatenate(
                [jnp.concatenate([a_s[g][pl.ds(k1, N2p, stride=P), :],
                                  a_s[g][pl.ds(N1 + k1, N2p, stride=P), :]], axis=0) for g in range(G)], axis=1)
            work.append((k1, _dot(m3_ref[k1], b.astype(BF16))))
        for k1, z in work:
            for g in range(G):
                y_s[g][pl.ds(pl.multiple_of(k1 * Q, SUBLANES), N2p), :] = z[:, lanes(g)]
        return carry

    lax.fori_loop(0, N1 // U2, p2, 0)

    def pair(g, k2):
        return jnp.concatenate([y_s[g][pl.ds(k2, N1, stride=Q), :],
                                y_s[g][pl.ds(k2 + 1, N1, stride=Q), :]], axis=0)

    for g in range(G):
        o_ref[0:2 * N1 - N_META, lanes(g)] = pair(g, 0)[N_META:].astype(BF16)

    def p3(t, carry):
        k2 = 2 * t + 2
        r0 = pl.multiple_of(k2 * N1 - N_META, 2 * SUBLANES)
        for g in range(G):
            o_ref[pl.ds(r0, 2 * N1), lanes(g)] = pair(g, k2).astype(BF16)
        return carry

    lax.fori_loop(0, N2 // 2 - 1, p3, 0)
    if N2 % 2:
        r0 = (N2 - 1) * N1 - N_META
        for g in range(G):
            o_ref[r0:r0 + N1, lanes(g)] = y_s[g][pl.ds(N2 - 1, N1, stride=Q), :].astype(BF16)


def _fmix(u, u_meta, *, groups, u1, u2, chunk=512):
    B, S, _ = u.shape
    dims, f1, m3, cd = _fmix_consts(S)
    Lp, N1, N2p, P, Q = dims["Lp"], dims["N1"], dims["N2p"], dims["P"], dims["Q"]
    W, G = FOURIER_GROUP_WIDTH, groups
    nb = (2 * LANES * 2) // (G * W)
    assert N2p % (nb * u1) == 0 and N1 % u2 == 0 and S % chunk == 0
    const = lambda shape: pl.BlockSpec(shape, lambda b, g: (0,) * len(shape))
    scratch = ([pltpu.VMEM((Lp, W), F32)] * (2 * G) + [pltpu.VMEM((N2p * P, W), F32)] * G
               + [pltpu.VMEM((N1 * Q, W), F32)] * G)
    return pl.pallas_call(
        functools.partial(_fmix_kernel, S=S, chunk=chunk, G=G, NB=nb, U1=u1, U2=u2, **dims),
        grid=(B, N_FOURIER_GROUPS // G),
        in_specs=[pl.BlockSpec((None, S, G * W), lambda b, g: (b, 0, g)),
                  pl.BlockSpec((N_META, G * W), lambda b, g: (0, g)),
                  const(cd.shape), const(f1.shape), const(m3.shape)],
        out_specs=pl.BlockSpec((None, S, G * W), lambda b, g: (b, 0, g)),
        out_shape=jax.ShapeDtypeStruct((B, S, FOURIER_WIDTH), BF16),
        scratch_shapes=scratch,
        compiler_params=pltpu.CompilerParams(
            dimension_semantics=("parallel", "parallel"), vmem_limit_bytes=VMEM_LIMIT_BYTES),
        name="fmix",
    )(u, u_meta, jnp.asarray(cd).astype(BF16), jnp.asarray(f1).astype(BF16), jnp.asarray(m3).astype(BF16))


def _post_stages(x_ref, a_ref, f_ref, gate_ref, wao_ref, wf_ref, wout_ref, g2_ref, wup_ref, wdown_ref,
                 gfin_ref, y_ref, *, ff_chunk, sub):
    tm = x_ref.shape[0]
    blocks = [slice(r, r + sub) for r in range(0, tm, sub)]
    state = {}

    def mixer(rows):
        a = _dot(a_ref[rows, :], wao_ref[...])
        f = _dot(f_ref[rows, :], wf_ref[...])
        merged = (gate_ref[rows, :D_MODEL].astype(F32) * a + gate_ref[rows, D_MODEL:].astype(F32) * f).astype(BF16)
        h = x_ref[rows, :] + _dot(merged, wout_ref[...])
        state[rows.start] = (h, _rms(h, g2_ref[...]).astype(BF16))

    def mlp_chunk(c):
        for rows in blocks:
            h, n = state[rows.start]
            t = _dot(n, wup_ref[:, c * ff_chunk:(c + 1) * ff_chunk])
            r = jnp.square(jnp.maximum(t, 0.0)).astype(BF16)
            state[rows.start] = (h + _dot(r, wdown_ref[c * ff_chunk:(c + 1) * ff_chunk, :]), n)

    def final():
        for rows in blocks:
            y_ref[rows, :] = _rms(state[rows.start][0], gfin_ref[...])

    return ([functools.partial(mixer, rows) for rows in blocks]
            + [functools.partial(mlp_chunk, c) for c in range(D_FF // ff_chunk)] + [final])


def _post_kernel(*refs, ff_chunk, sub):
    for stage in _post_stages(*refs, ff_chunk=ff_chunk, sub=sub):
        stage()


def _post(x, a, f, gates, w_ao, w_f, w_out, g2, w_up, w_down, g_fin, *, tm, sub=512, ff_chunk=512):
    B, S, _ = x.shape
    tok = lambda w: pl.BlockSpec((None, tm, w), lambda b, i: (b, i, 0))
    const = lambda shape: pl.BlockSpec(shape, lambda b, i: (0,) * len(shape), pipeline_mode=pl.Buffered(1))
    return pl.pallas_call(
        functools.partial(_post_kernel, ff_chunk=ff_chunk, sub=sub),
        grid=(B, S // tm),
        in_specs=[tok(D_MODEL), tok(ATTN_WIDTH), tok(FOURIER_WIDTH), tok(N_BRANCHES * D_MODEL),
                  const(w_ao.shape), const(w_f.shape), const(w_out.shape), const((1, D_MODEL)),
                  const(w_up.shape), const(w_down.shape), const((1, D_MODEL))],
        out_specs=tok(D_MODEL),
        out_shape=jax.ShapeDtypeStruct((B, S, D_MODEL), F32),
        compiler_params=pltpu.CompilerParams(
            dimension_semantics=("parallel", "parallel"), vmem_limit_bytes=VMEM_LIMIT_BYTES),
        name="post",
    )(x, a, f, gates, w_ao, w_f, w_out, g2, w_up, w_down, g_fin)


def _rope_table(first_pos, n_pos):
    half = ROT_DIM // 2
    inv_freq = ROPE_THETA ** (-np.arange(half, dtype=np.float64) / half)
    ang = np.arange(first_pos, first_pos + n_pos, dtype=np.float64)[:, None] * inv_freq[None, :]
    cos, sin = np.cos(ang), np.sin(ang)
    rest = HEAD_DIM - ROT_DIM
    head = lambda a, b, fill: np.concatenate([a, b, np.full((n_pos, rest), fill)], axis=1)
    zero = np.zeros_like(sin)
    per_head = (head(cos, cos, 1.0), head(zero, sin, 0.0), head(-sin, zero, 0.0))
    table = np.concatenate([np.tile(t, (1, LANES // HEAD_DIM)) for t in per_head], axis=1)
    return jnp.asarray(table.astype(np.float32))


def _trunk(x, meta_parts, table, wts, *, tm_proj, tm, tq, fmix):
    k4_m, vt_m, u_m = meta_parts
    q, k4, vt, u, gates = _proj(x, table, wts["norm_mix_g"], wts["w_in"], wts["b_gate"], tm=tm_proj,
                                transpose_v=True)
    a = _attn(q, k4, vt, k4_m, vt_m, wts["attn_sink"], tq=tq)
    f = _fmix(u, u_m, **fmix)
    return _post(x, a, f, gates, wts["w_attn_out"], wts["w_fourier"], wts["w_out"], wts["norm_mlp_g"],
                 wts["w_mlp_up"], wts["w_mlp_down"], wts["norm_final_g"], tm=tm)


def kernel(x_prompt, x_sample, meta_tokens, norm_mix_g, w_in, b_gate, attn_sink, w_attn_out, w_fourier, w_out,
           norm_mlp_g, w_mlp_up, w_mlp_down, norm_final_g):
    assert w_in.shape[0] == 1, "single-layer trunk: meta-token outputs are never consumed"
    wts = dict(
        norm_mix_g=norm_mix_g[0][None, :], w_in=w_in[0].astype(BF16), b_gate=b_gate[0][None, :],
        attn_sink=attn_sink[0], w_attn_out=w_attn_out[0].astype(BF16), w_fourier=w_fourier[0].astype(BF16),
        w_out=w_out[0].astype(BF16), norm_mlp_g=norm_mlp_g[0][None, :], w_mlp_up=w_mlp_up[0].astype(BF16),
        w_mlp_down=w_mlp_down[0].astype(BF16), norm_final_g=norm_final_g[None, :])
    table = _rope_table(N_META, max(x_prompt.shape[1], x_sample.shape[1]))
    _, k4_m, v_m, u_m, _ = _proj(meta_tokens[None], _rope_table(0, N_META),
                                 wts["norm_mix_g"], wts["w_in"], wts["b_gate"], tm=N_META, transpose_v=False)
    vt_m = jnp.pad(v_m[0].T, ((0, 0), (0, LANES - N_META)))
    meta_parts = (k4_m[0], vt_m, u_m[0])
    tiles = dict(tm_proj=1024, tm=512, tq=2048)
    y_prompt = _trunk(x_prompt, meta_parts, table, wts, fmix=dict(groups=4, u1=48, u2=48), **tiles)
    y_sample = _trunk(x_sample, meta_parts, table, wts, fmix=dict(groups=1, u1=8, u2=36), **tiles)
    return (y_prompt, y_sample)
```

```python
import functools

import numpy as np
import jax
import jax.numpy as jnp
from jax import lax
from jax.experimental import pallas as pl
from jax.experimental.pallas import tpu as pltpu

D_MODEL = 1024
HEAD_DIM = 64
N_Q_HEADS = 8
N_KV_HEADS = 2
ATTN_WIDTH = N_Q_HEADS * HEAD_DIM
KV_WIDTH = N_KV_HEADS * HEAD_DIM
WINDOW = 128
ROPE_THETA = 500000.0
ROT_DIM = HEAD_DIM // 4
N_FOURIER_GROUPS = 4
FOURIER_GROUP_WIDTH = 128
FOURIER_WIDTH = N_FOURIER_GROUPS * FOURIER_GROUP_WIDTH
N_BRANCHES = 2
IN_WIDTH = ATTN_WIDTH + 2 * KV_WIDTH + FOURIER_WIDTH + N_BRANCHES * D_MODEL
D_FF = 4 * D_MODEL
N_META = 16
RMS_EPS = 1e-6
NEG_INF = -1e30
LOG2E = 1.4426950408889634

LANES = 128
SUBLANES = 8
VMEM_LIMIT_BYTES = 56 * 1024 * 1024

_Q0, _K0, _U0, _G0 = 0, ATTN_WIDTH, ATTN_WIDTH + 2 * KV_WIDTH, ATTN_WIDTH + 2 * KV_WIDTH + FOURIER_WIDTH

BF16 = jnp.bfloat16
F32 = jnp.float32


def _dot(a, b):
    return jnp.dot(a, b, preferred_element_type=F32)


def _dot_nt(a, b):
    return lax.dot_general(a, b, (((1,), (1,)), ((), ())), preferred_element_type=F32)


def _rms(x, g):
    return x * lax.rsqrt(jnp.mean(x * x, axis=-1, keepdims=True) + RMS_EPS) * g


def _proj_kernel(x_ref, tab_ref, g_ref, w_ref, b_ref,
                 q_ref, k4_ref, v_ref, u_ref, gate_ref, *, transpose_v, sub):
    tm = x_ref.shape[0]
    blocks = [slice(r, r + sub) for r in range(0, tm, sub)]
    normed = {rows.start: _rms(x_ref[rows, :], g_ref[...]).astype(BF16) for rows in blocks}
    project = lambda rows, lo, hi: _dot(normed[rows.start], w_ref[:, lo:hi])

    def rope(z, rows):
        cos_t, sin_a, sin_b = (tab_ref[rows, c * LANES:(c + 1) * LANES] for c in range(3))
        return (z * cos_t + pltpu.roll(z, ROT_DIM // 2, 1) * sin_a
                + pltpu.roll(z, LANES - ROT_DIM // 2, 1) * sin_b)

    for rows in blocks:
        zq = project(rows, _Q0, _K0)
        for c in range(ATTN_WIDTH // LANES):
            q_ref[rows, c * LANES:(c + 1) * LANES] = (
                rope(zq[:, c * LANES:(c + 1) * LANES], rows) * (LOG2E * HEAD_DIM ** -0.5)).astype(BF16)
    for rows in blocks:
        zkv = project(rows, _K0, _U0)
        zk = rope(zkv[:, :KV_WIDTH], rows)
        zv = zkv[:, KV_WIDTH:]
        zks = pltpu.roll(zk, HEAD_DIM, 1)
        low = lax.broadcasted_iota(jnp.int32, zk.shape, 1) < HEAD_DIM
        for s, (keep_low, src) in enumerate(((True, zk), (False, zks), (True, zks), (False, zk))):
            k4_ref[rows, s * LANES:(s + 1) * LANES] = jnp.where(low == keep_low, src, 0.0).astype(BF16)
        if transpose_v:
            v_ref[:, rows] = zv.T.astype(BF16)
        else:
            v_ref[rows, :] = zv.astype(BF16)
    for rows in blocks:
        u_ref[rows, :] = project(rows, _U0, _G0).astype(BF16)
    for c in range(N_BRANCHES):
        lo = _G0 + c * D_MODEL
        for rows in blocks:
            g = project(rows, lo, lo + D_MODEL) + b_ref[:, c * D_MODEL:(c + 1) * D_MODEL]
            gate_ref[rows, c * D_MODEL:(c + 1) * D_MODEL] = jax.nn.sigmoid(g).astype(BF16)


def _proj(x, table, g, w_in, b_gate, *, tm, transpose_v):
    B, S, _ = x.shape
    tok = lambda w: pl.BlockSpec((None, tm, w), lambda b, i: (b, i, 0))
    tab = pl.BlockSpec((tm, 3 * LANES), lambda b, i: (i, 0))
    const = lambda shape: pl.BlockSpec(shape, lambda b, i: (0,) * len(shape))
    out_w = (ATTN_WIDTH, 4 * LANES, KV_WIDTH, FOURIER_WIDTH, N_BRANCHES * D_MODEL)
    out_specs = [tok(w) for w in out_w]
    out_shape = [jax.ShapeDtypeStruct((B, S, w), BF16) for w in out_w]
    if transpose_v:
        out_specs[2] = pl.BlockSpec((None, KV_WIDTH, tm), lambda b, i: (b, 0, i))
        out_shape[2] = jax.ShapeDtypeStruct((B, KV_WIDTH, S), BF16)
    return pl.pallas_call(
        functools.partial(_proj_kernel, transpose_v=transpose_v, sub=min(tm, 256)),
        grid=(B, S // tm),
        in_specs=[tok(D_MODEL), tab, const((1, D_MODEL)),
                  const((D_MODEL, IN_WIDTH)), const((1, N_BRANCHES * D_MODEL))],
        out_specs=out_specs,
        out_shape=out_shape,
        compiler_params=pltpu.CompilerParams(
            dimension_semantics=("parallel", "parallel"), vmem_limit_bytes=VMEM_LIMIT_BYTES),
        name="proj",
    )(x, table, g, w_in, b_gate)


def _attn_slots(i, sink_ref, q_ref, kc_ref, kp_ref, kn_ref, vc_ref, vp_ref, vn_ref, km_ref, vm_ref,
                o_ref, *, tq, seq):
    nblk = tq // WINDOW
    nband = 3 * WINDOW
    kall =jnp.concatenate([kp_ref[...], kc_ref[...], kn_ref[...]], axis=0)
    vall = jnp.concatenate([vp_ref[...], vc_ref[...], vn_ref[...]], axis=1)
    kmeta, vmeta = km_ref[...], vm_ref[...]
    key = lax.broadcasted_iota(jnp.int32, (WINDOW, 2 * LANES), 0)
    qry = jnp.bitwise_and(lax.broadcasted_iota(jnp.int32, (WINDOW, 2 * LANES), 1), WINDOW - 1)
    band_prev = jnp.where(key >= qry, 0.0, NEG_INF).astype(F32)
    band_next = jnp.where(key <= qry, 0.0, NEG_INF).astype(F32)
    first_pair = lax.broadcasted_iota(jnp.int32, (1, 2 * LANES), 1) < LANES
    p_pad = jnp.zeros((LANES - N_META, 2 * LANES), BF16)
    ones_all = jnp.ones((2 * SUBLANES, nband + LANES), BF16)
    heads_per_kv = N_Q_HEADS // N_KV_HEADS

    units = [(h, e) for h in range(N_KV_HEADS) for e in range(2)]

    def score(j, h, e):
        rows = slice(j * WINDOW, (j + 1) * WINDOW)
        blk0 = i * tq + j * WINDOW
        bias_prev = band_prev + jnp.where(blk0 >= WINDOW, 0.0, NEG_INF).astype(F32)
        bias_next = band_next + jnp.where(blk0 + WINDOW < seq, 0.0, NEG_INF).astype(F32)
        q2 = jnp.concatenate([q_ref[rows, (2 * h) * LANES:(2 * h + 1) * LANES],
                              q_ref[rows, (2 * h + 1) * LANES:(2 * h + 2) * LANES]], axis=0)
        sl = slice((2 * h + e) * LANES, (2 * h + e + 1) * LANES)
        keys = jnp.concatenate([kall[j * WINDOW:j * WINDOW + nband, sl], kmeta[:, sl]], axis=0)
        s = _dot_nt(keys, q2)
        return jnp.concatenate([s[:WINDOW] + bias_prev, s[WINDOW:2 * WINDOW],
                                s[2 * WINDOW:nband] + bias_next, s[nband:]], axis=0)

    def softmax(h, e, s):
        sink = LOG2E * jnp.where(first_pair, sink_ref[heads_per_kv * h + e], sink_ref[heads_per_kv * h + 2 + e])
        m = jnp.maximum(jnp.max(s, axis=0, keepdims=True), sink)
        return jnp.exp2(s - m).astype(BF16), jnp.exp2(sink - m)

    def value(j, h, pb, p_sink):
        v_all = jnp.concatenate([vall[h * HEAD_DIM:(h + 1) * HEAD_DIM, j * WINDOW:j * WINDOW + nband],
                                 vmeta[h * HEAD_DIM:(h + 1) * HEAD_DIM, :]], axis=1)
        v_all = jnp.concatenate([v_all, ones_all], axis=0)
        o = _dot(v_all, jnp.concatenate([pb, p_pad], axis=0))
        denom = o[HEAD_DIM:HEAD_DIM + 1] + p_sink
        return o[:HEAD_DIM] * (1.0 / denom)

    def write(j, h, halves):
        rows = slice(j * WINDOW, (j + 1) * WINDOW)
        o_t = jnp.concatenate(halves, axis=0)
        for pp in range(2):
            o_ref[rows, (2 * h + pp) * LANES:(2 * h + pp + 1) * LANES] = (
                o_t[:, pp * LANES:(pp + 1) * LANES].T.astype(BF16))

    scores, probs, outs = {}, {}, {}

    def slot(t):
        for h, e in units:
            if 0 <= t - 2 < nblk:
                outs[h, e] = value(t - 2, h, *probs.pop((t - 2, h, e)))
                if e == 1:
                    write(t - 2, h, [outs.pop((h, 0)), outs.pop((h, 1))])
            if t < nblk:
                scores[t, h, e] = score(t, h, e)
            if 0 <= t - 1 < nblk:
                probs[t - 1, h, e] = softmax(h, e, scores.pop((t - 1, h, e)))

    return [functools.partial(slot, t) for t in range(nblk + 2)]


def _attn_kernel(*refs, tq, seq):
    for slot in _attn_slots(pl.program_id(1), *refs, tq=tq, seq=seq):
        slot()


def _attn(q, k4, vt, k4_meta, vt_meta, sink, *, tq):
    B, S, _ = q.shape
    r = tq // WINDOW
    last = S // WINDOW - 1
    prev_idx = lambda i: jnp.maximum(i * r - 1, 0)
    next_idx = lambda i: jnp.minimum((i + 1) * r, last)
    cur = lambda w: pl.BlockSpec((None, tq, w), lambda b, i: (b, i, 0))
    kw = 4 * LANES
    return pl.pallas_call(
        functools.partial(_attn_kernel, tq=tq, seq=S),
        grid=(B, S // tq),
        in_specs=[pl.BlockSpec(memory_space=pltpu.SMEM), cur(ATTN_WIDTH),
                  cur(kw),
                  pl.BlockSpec((None, WINDOW, kw), lambda b, i: (b, prev_idx(i), 0)),
                  pl.BlockSpec((None, WINDOW, kw), lambda b, i: (b, next_idx(i), 0)),
                  pl.BlockSpec((None, KV_WIDTH, tq), lambda b, i: (b, 0, i)),
                  pl.BlockSpec((None, KV_WIDTH, WINDOW), lambda b, i: (b, 0, prev_idx(i))),
                  pl.BlockSpec((None, KV_WIDTH, WINDOW), lambda b, i: (b, 0, next_idx(i))),
                  pl.BlockSpec((N_META, kw), lambda b, i: (0, 0)),
                  pl.BlockSpec((KV_WIDTH, LANES), lambda b, i: (0, 0))],
        out_specs=cur(ATTN_WIDTH),
        out_shape=jax.ShapeDtypeStruct((B, S, ATTN_WIDTH), BF16),
        compiler_params=pltpu.CompilerParams(
            dimension_semantics=("parallel", "parallel"), vmem_limit_bytes=VMEM_LIMIT_BYTES),
        name="attn",
    )(sink, q, k4, k4, k4, vt, vt, vt, k4_meta, vt_meta)


MXU_DEPTH = 256


def _factor(L):
    best = None
    for n1 in range(SUBLANES, L, SUBLANES):
        if L % n1 or (L // n1) % 8 == 0:
            continue
        n2p = _round_up(L // n1, 2 * SUBLANES)
        cost = (-(-2 * n1 // MXU_DEPTH) + -(-2 * n2p // MXU_DEPTH), abs(n1 - L ** 0.5))
        if best is None or cost < best[0]:
            best = (cost, n1)
    return best[1], L // best[1]


def _round_up(x, m):
    return (x + m - 1) // m * m


def _odd_tiles(rows):
    p = _round_up(rows, SUBLANES)
    return p if (p // SUBLANES) % 2 else p + SUBLANES


def _fmix_consts(S):
    L = S + N_META
    N1, N2 = _factor(L)
    N2p = _round_up(N2, 2 * SUBLANES)
    dims = dict(L=L, N1=N1, N2=N2, N2p=N2p, P=_odd_tiles(2 * N1), Q=_odd_tiles(N2p),
                Lp=_round_up(L + N2p - N2, SUBLANES))
    k1 = np.arange(N1, dtype=np.int64)
    ang1 = 2.0 * np.pi * ((k1[:, None] * k1[None, :]) % N1) / N1
    c1, s1 = np.cos(ang1) / np.sqrt(N1), np.sin(ang1) / np.sqrt(N1)
    f1 = np.block([[c1, s1], [-s1, c1]]).astype(np.float32)
    k = k1[:, None, None] + N1 * np.arange(N2p, dtype=np.int64)[None, :, None]
    n2 = np.arange(N2p, dtype=np.int64)[None, None, :]
    ang3 = 2.0 * np.pi * (((n2 + N_META) * k) % L) / L
    valid = ((np.arange(N2p) < N2)[None, :, None] & (np.arange(N2p) < N2)[None, None, :])
    c3 = np.where(valid, np.cos(ang3), 0.0) / np.sqrt(N2)
    s3 = np.where(valid, np.sin(ang3), 0.0) / np.sqrt(N2)
    m3 = np.concatenate([c3, s3], axis=2).astype(np.float32)
    c = np.arange(FOURIER_GROUP_WIDTH, dtype=np.int64)
    angc = 2.0 * np.pi * ((c[:, None] * c[None, :]) % FOURIER_GROUP_WIDTH) / FOURIER_GROUP_WIDTH
    cd = (np.concatenate([np.cos(angc), -np.sin(angc)], axis=1)
          / np.sqrt(FOURIER_GROUP_WIDTH)).astype(np.float32)
    return dims, f1, m3, cd


def _fmix_kernel(u_ref, um_ref, cd_ref, f1_ref, m3_ref, o_ref, *scratch,
                 S, L, Lp, N1, N2, N2p, P, Q, G, NB, chunk):
    W = FOURIER_GROUP_WIDTH
    xr, xi, a_s, y_s = (scratch[i * G:(i + 1) * G] for i in range(4))
    cd = cd_ref[...]
    lanes = lambda g: slice(g * W, (g + 1) * W)

    for r0 in range(0, S, chunk):
        for g in range(G):
            v = _dot(u_ref[r0:r0 + chunk, lanes(g)], cd)
            xr[g][r0:r0 + chunk, :] = v[:, :W]
            xi[g][r0:r0 + chunk, :] = v[:, W:]
    for g in range(G):
        vm = _dot(um_ref[:, lanes(g)], cd)
        xr[g][S:L, :] = vm[:, :W]
        xi[g][S:L, :] = vm[:, W:]
        xr[g][L:Lp, :] = jnp.zeros((Lp - L, W), F32)
        xi[g][L:Lp, :] = jnp.zeros((Lp - L, W), F32)

    for n2_0 in range(0, N2p, NB):
        n2s = range(n2_0, n2_0 + NB)
        cols = [jnp.concatenate([xr[g][pl.ds(n2, N1, stride=N2), :],
                                 xi[g][pl.ds(n2, N1, stride=N2), :]], axis=0)
                for n2 in n2s for g in range(G)]
        a = _dot(f1_ref[...], jnp.concatenate(cols, axis=1).astype(BF16))
        for j, n2 in enumerate(n2s):
            for g in range(G):
                a_s[g][n2 * P:n2 * P + 2 * N1, :] = a[:, lanes(j * G + g)]

    for k1 in range(N1):
        b = jnp.concatenate(
            [jnp.concatenate([a_s[g][pl.ds(k1, N2p, stride=P), :],
                              a_s[g][pl.ds(N1 + k1, N2p, stride=P), :]], axis=0) for g in range(G)], axis=1)
        z = _dot(m3_ref[k1], b.astype(BF16))
        for g in range(G):
            y_s[g][k1 * Q:k1 * Q + N2p, :] = z[:, lanes(g)]

    def pair(g, k2):
        return jnp.concatenate([y_s[g][pl.ds(k2, N1, stride=Q), :],
                                y_s[g][pl.ds(k2 + 1, N1, stride=Q), :]], axis=0)

    for g in range(G):
        o_ref[0:2 * N1 - N_META, lanes(g)] = pair(g, 0)[N_META:].astype(BF16)

    n_pairs = N2 // 2 - 1
    u3 = max(d for d in (4, 2, 1) if n_pairs % d == 0)

    def p3(t, carry):
        for uu in range(u3):
            k2 = 2 * (t * u3 + uu) + 2
            r0 = pl.multiple_of(k2 * N1 - N_META, 2 * SUBLANES)
            for g in range(G):
                o_ref[pl.ds(r0, 2 * N1), lanes(g)] = pair(g, k2).astype(BF16)
        return carry

    lax.fori_loop(0, n_pairs // u3, p3, 0)
    if N2 % 2:
        r0 = (N2 - 1) * N1 - N_META
        for g in range(G):
            o_ref[r0:r0 + N1, lanes(g)] = y_s[g][pl.ds(N2 - 1, N1, stride=Q), :].astype(BF16)


def _fmix_scratch_rows(dims):
    return (dims["Lp"], dims["Lp"], dims["N2p"] * dims["P"], dims["N1"] * dims["Q"])


def _fmix(u, u_meta, *, groups, chunk):
    B, S, _ = u.shape
    dims, f1, m3, cd = _fmix_consts(S)
    W, G = FOURIER_GROUP_WIDTH, groups
    nb = N_FOURIER_GROUPS // G
    assert dims["N2p"] % nb == 0 and S % chunk == 0
    const = lambda shape: pl.BlockSpec(shape, lambda b, g: (0,) * len(shape))
    scratch = [pltpu.VMEM((rows, W), F32) for rows in _fmix_scratch_rows(dims) for _ in range(G)]
    return pl.pallas_call(
        functools.partial(_fmix_kernel, S=S, chunk=chunk, G=G, NB=nb, **dims),
        grid=(B, N_FOURIER_GROUPS // G),
        in_specs=[pl.BlockSpec((None, S, G * W), lambda b, g: (b, 0, g)),
                  pl.BlockSpec((N_META, G * W), lambda b, g: (0, g)),
                  const(cd.shape), const(f1.shape), const(m3.shape)],
        out_specs=pl.BlockSpec((None, S, G * W), lambda b, g: (b, 0, g)),
        out_shape=jax.ShapeDtypeStruct((B, S, FOURIER_WIDTH), BF16),
        scratch_shapes=scratch,
        compiler_params=pltpu.CompilerParams(
            dimension_semantics=("parallel", "parallel"), vmem_limit_bytes=VMEM_LIMIT_BYTES),
        name="fmix",
    )(u, u_meta, jnp.asarray(cd).astype(BF16), jnp.asarray(f1).astype(BF16), jnp.asarray(m3).astype(BF16))


def _post_stages(x_ref, a_ref, f_ref, gate_ref, wao_ref, wf_ref, wout_ref, g2_ref, wup_ref, wdown_ref,
                 gfin_ref, y_ref, *, ff_chunk, sub):
    tm = x_ref.shape[0]
    blocks = [slice(r, r + sub) for r in range(0, tm, sub)]
    state = {}

    def mixer(rows):
        a = _dot(a_ref[rows, :], wao_ref[...])
        f = _dot(f_ref[rows, :], wf_ref[...])
        merged = (gate_ref[rows, :D_MODEL].astype(F32) * a + gate_ref[rows, D_MODEL:].astype(F32) * f).astype(BF16)
        h = x_ref[rows, :] + _dot(merged, wout_ref[...])
        state[rows.start] = (h, _rms(h, g2_ref[...]).astype(BF16))

    def mlp_chunk(c):
        for rows in blocks:
            h, n = state[rows.start]
            t = _dot(n, wup_ref[:, c * ff_chunk:(c + 1) * ff_chunk])
            r = jnp.square(jnp.maximum(t, 0.0)).astype(BF16)
            state[rows.start] = (h + _dot(r, wdown_ref[c * ff_chunk:(c + 1) * ff_chunk, :]), n)

    def final():
        for rows in blocks:
            y_ref[rows, :] = _rms(state[rows.start][0], gfin_ref[...])

    return ([functools.partial(mixer, rows) for rows in blocks]
            + [functools.partial(mlp_chunk, c) for c in range(D_FF // ff_chunk)] + [final])


def _post_kernel(*refs, ff_chunk, sub):
    for stage in _post_stages(*refs, ff_chunk=ff_chunk, sub=sub):
        stage()


def _post(x, a, f, gates, w_ao, w_f, w_out, g2, w_up, w_down, g_fin, *, tm, sub=512, ff_chunk=512):
    B, S, _ = x.shape
    tok = lambda w: pl.BlockSpec((None, tm, w), lambda b, i: (b, i, 0))
    const = lambda shape: pl.BlockSpec(shape, lambda b, i: (0,) * len(shape), pipeline_mode=pl.Buffered(1))
    return pl.pallas_call(
        functools.partial(_post_kernel, ff_chunk=ff_chunk, sub=sub),
        grid=(B, S // tm),
        in_specs=[tok(D_MODEL), tok(ATTN_WIDTH), tok(FOURIER_WIDTH), tok(N_BRANCHES * D_MODEL),
                  const(w_ao.shape), const(w_f.shape), const(w_out.shape), const((1, D_MODEL)),
                  const(w_up.shape), const(w_down.shape), const((1, D_MODEL))],
        out_specs=tok(D_MODEL),
        out_shape=jax.ShapeDtypeStruct((B, S, D_MODEL), F32),
        compiler_params=pltpu.CompilerParams(
            dimension_semantics=("parallel", "parallel"), vmem_limit_bytes=VMEM_LIMIT_BYTES),
        name="post",
    )(x, a, f, gates, w_ao, w_f, w_out, g2, w_up, w_down, g_fin)


def _rope_table(first_pos, n_pos):
    half = ROT_DIM // 2
    inv_freq = ROPE_THETA ** (-np.arange(half, dtype=np.float64) / half)
    ang = np.arange(first_pos, first_pos + n_pos, dtype=np.float64)[:, None] * inv_freq[None, :]
    cos, sin = np.cos(ang), np.sin(ang)
    rest = HEAD_DIM - ROT_DIM
    head = lambda a, b, fill: np.concatenate([a, b, np.full((n_pos, rest), fill)], axis=1)
    zero = np.zeros_like(sin)
    per_head = (head(cos, cos, 1.0), head(zero, sin, 0.0), head(-sin, zero, 0.0))
    table = np.concatenate([np.tile(t, (1, LANES // HEAD_DIM)) for t in per_head], axis=1)
    return jnp.asarray(table.astype(np.float32))


def _plan(S):
    dims = _fmix_consts(S)[0]
    group_bytes = sum(_fmix_scratch_rows(dims)) * FOURIER_GROUP_WIDTH * 4
    return dict(tm_proj=min(S, 1024), tm_post=min(S, 512), tq=min(S, 16 * WINDOW), fmix_chunk=min(S, 512),
                fmix_groups=N_FOURIER_GROUPS if N_FOURIER_GROUPS * group_bytes <= VMEM_LIMIT_BYTES // 2 else 1)


def _trunk(x, meta_parts, table, wts):
    k4_m, vt_m, u_m = meta_parts
    plan = _plan(x.shape[1])
    q, k4, vt, u, gates = _proj(x, table, wts["norm_mix_g"], wts["w_in"], wts["b_gate"], tm=plan["tm_proj"],
                                transpose_v=True)
    a = _attn(q, k4, vt, k4_m, vt_m, wts["attn_sink"], tq=plan["tq"])
    f = _fmix(u, u_m, groups=plan["fmix_groups"], chunk=plan["fmix_chunk"])
    return _post(x, a, f, gates, wts["w_attn_out"], wts["w_fourier"], wts["w_out"], wts["norm_mlp_g"],
                 wts["w_mlp_up"], wts["w_mlp_down"], wts["norm_final_g"], tm=plan["tm_post"])


def kernel(x_prompt, x_sample, meta_tokens, norm_mix_g, w_in, b_gate, attn_sink, w_attn_out, w_fourier, w_out,
           norm_mlp_g, w_mlp_up, w_mlp_down, norm_final_g):
    assert w_in.shape[0] == 1, "single-layer trunk: meta-token outputs are never consumed"
    wts = dict(
        norm_mix_g=norm_mix_g[0][None, :], w_in=w_in[0].astype(BF16), b_gate=b_gate[0][None, :],
        attn_sink=attn_sink[0], w_attn_out=w_attn_out[0].astype(BF16), w_fourier=w_fourier[0].astype(BF16),
        w_out=w_out[0].astype(BF16), norm_mlp_g=norm_mlp_g[0][None, :], w_mlp_up=w_mlp_up[0].astype(BF16),
        w_mlp_down=w_mlp_down[0].astype(BF16), norm_final_g=norm_final_g[None, :])
    table = _rope_table(N_META, max(x_prompt.shape[1], x_sample.shape[1]))
    _, k4_m, v_m, u_m, _ = _proj(meta_tokens[None], _rope_table(0, N_META),
                                 wts["norm_mix_g"], wts["w_in"], wts["b_gate"], tm=N_META, transpose_v=False)
    vt_m = jnp.pad(v_m[0].T, ((0, 0), (0, LANES - N_META)))
    meta_parts = (k4_m[0], vt_m, u_m[0])
    return (_trunk(x_prompt, meta_parts, table, wts), _trunk(x_sample, meta_parts, table, wts))
```

```python
import functools

import numpy as np
import jax
import jax.numpy as jnp
from jax import lax
from jax.experimental import pallas as pl
from jax.experimental.pallas import tpu as pltpu

D_MODEL = 1024
HEAD_DIM = 64
N_Q_HEADS = 8
N_KV_HEADS = 2
ATTN_WIDTH = N_Q_HEADS * HEAD_DIM
KV_WIDTH = N_KV_HEADS * HEAD_DIM
WINDOW = 128
ROPE_THETA = 500000.0
ROT_DIM = HEAD_DIM // 4
N_FOURIER_GROUPS = 4
FOURIER_GROUP_WIDTH = 128
FOURIER_WIDTH = N_FOURIER_GROUPS * FOURIER_GROUP_WIDTH
N_BRANCHES = 2
IN_WIDTH = ATTN_WIDTH + 2 * KV_WIDTH + FOURIER_WIDTH + N_BRANCHES * D_MODEL
D_FF = 4 * D_MODEL
N_META = 16
RMS_EPS = 1e-6
NEG_INF = -1e30
LOG2E = 1.4426950408889634

LANES = 128
SUBLANES = 8
V7X_VMEM_BYTES = 64 * 1024 * 1024
VMEM_LIMIT_BYTES = 56 * 1024 * 1024
POST_VMEM_LIMIT_BYTES = V7X_VMEM_BYTES - 2 * 1024 * 1024

_Q0, _K0, _U0, _G0 = 0, ATTN_WIDTH, ATTN_WIDTH + 2 * KV_WIDTH, ATTN_WIDTH + 2 * KV_WIDTH + FOURIER_WIDTH

BF16 = jnp.bfloat16
F32 = jnp.float32


def _dot(a, b):
    return jnp.dot(a, b, preferred_element_type=F32)


def _dot_nt(a, b):
    return lax.dot_general(a, b, (((1,), (1,)), ((), ())), preferred_element_type=F32)


def _rms(x, g):
    return x * lax.rsqrt(jnp.mean(x * x, axis=-1, keepdims=True) + RMS_EPS) * g


def _proj_kernel(x_ref, tab_ref, g_ref, w_ref, b_ref,
                 q_ref, k4_ref, v_ref, u_ref, gate_ref, *, transpose_v, sub):
    tm = x_ref.shape[0]
    blocks = [slice(r, r + sub) for r in range(0, tm, sub)]
    normed = {rows.start: _rms(x_ref[rows, :], g_ref[...]).astype(BF16) for rows in blocks}
    project = lambda rows, lo, hi: _dot(normed[rows.start], w_ref[:, lo:hi])

    def rope(z, rows):
        cos_t, sin_a, sin_b = (tab_ref[rows, c * LANES:(c + 1) * LANES] for c in range(3))
        return (z * cos_t + pltpu.roll(z, ROT_DIM // 2, 1) * sin_a
                + pltpu.roll(z, LANES - ROT_DIM // 2, 1) * sin_b)

    for rows in blocks:
        zq = project(rows, _Q0, _K0)
        for c in range(ATTN_WIDTH // LANES):
            q_ref[rows, c * LANES:(c + 1) * LANES] = (
                rope(zq[:, c * LANES:(c + 1) * LANES], rows) * (LOG2E * HEAD_DIM ** -0.5)).astype(BF16)
    for rows in blocks:
        zkv = project(rows, _K0, _U0)
        zk = rope(zkv[:, :KV_WIDTH], rows)
        zv = zkv[:, KV_WIDTH:]
        zks = pltpu.roll(zk, HEAD_DIM, 1)
        low = lax.broadcasted_iota(jnp.int32, zk.shape, 1) < HEAD_DIM
        for s, (keep_low, src) in enumerate(((True, zk), (False, zks), (True, zks), (False, zk))):
            k4_ref[rows, s * LANES:(s + 1) * LANES] = jnp.where(low == keep_low, src, 0.0).astype(BF16)
        if transpose_v:
            v_ref[:, rows] = zv.T.astype(BF16)
        else:
            v_ref[rows, :] = zv.astype(BF16)
    for rows in blocks:
        u_ref[rows, :] = project(rows, _U0, _G0).astype(BF16)
    for c in range(N_BRANCHES):
        lo = _G0 + c * D_MODEL
        for rows in blocks:
            g = project(rows, lo, lo + D_MODEL) + b_ref[:, c * D_MODEL:(c + 1) * D_MODEL]
            gate_ref[rows, c * D_MODEL:(c + 1) * D_MODEL] = jax.nn.sigmoid(g).astype(BF16)


def _proj(x, table, g, w_in, b_gate, *, tm, transpose_v):
    B, S, _ = x.shape
    tok = lambda w: pl.BlockSpec((None, tm, w), lambda b, i: (b, i, 0))
    tab = pl.BlockSpec((tm, 3 * LANES), lambda b, i: (i, 0))
    const = lambda shape: pl.BlockSpec(shape, lambda b, i: (0,) * len(shape))
    out_w = (ATTN_WIDTH, 4 * LANES, KV_WIDTH, FOURIER_WIDTH, N_BRANCHES * D_MODEL)
    out_specs = [tok(w) for w in out_w]
    out_shape = [jax.ShapeDtypeStruct((B, S, w), BF16) for w in out_w]
    if transpose_v:
        out_specs[2] = pl.BlockSpec((None, KV_WIDTH, tm), lambda b, i: (b, 0, i))
        out_shape[2] = jax.ShapeDtypeStruct((B, KV_WIDTH, S), BF16)
    return pl.pallas_call(
        functools.partial(_proj_kernel, transpose_v=transpose_v, sub=min(tm, 256)),
        grid=(B, S // tm),
        in_specs=[tok(D_MODEL), tab, const((1, D_MODEL)),
                  const((D_MODEL, IN_WIDTH)), const((1, N_BRANCHES * D_MODEL))],
        out_specs=out_specs,
        out_shape=out_shape,
        compiler_params=pltpu.CompilerParams(
            dimension_semantics=("parallel", "parallel"), vmem_limit_bytes=VMEM_LIMIT_BYTES),
        name="proj",
    )(x, table, g, w_in, b_gate)


def _attn_slots(i, sink_ref, q_ref, kc_ref, kp_ref, kn_ref, vc_ref, vp_ref, vn_ref, km_ref, vm_ref,
                o_ref, *, tq, seq):
    nblk = tq // WINDOW
    nband = 3 * WINDOW
    kall =jnp.concatenate([kp_ref[...], kc_ref[...], kn_ref[...]], axis=0)
    vall = jnp.concatenate([vp_ref[...], vc_ref[...], vn_ref[...]], axis=1)
    kmeta, vmeta = km_ref[...], vm_ref[...]
    key = lax.broadcasted_iota(jnp.int32, (WINDOW, 2 * LANES), 0)
    qry = jnp.bitwise_and(lax.broadcasted_iota(jnp.int32, (WINDOW, 2 * LANES), 1), WINDOW - 1)
    band_prev = jnp.where(key >= qry, 0.0, NEG_INF).astype(F32)
    band_next = jnp.where(key <= qry, 0.0, NEG_INF).astype(F32)
    first_pair = lax.broadcasted_iota(jnp.int32, (1, 2 * LANES), 1) < LANES
    p_pad = jnp.zeros((LANES - N_META, 2 * LANES), BF16)
    ones_all = jnp.ones((2 * SUBLANES, nband + LANES), BF16)
    heads_per_kv = N_Q_HEADS // N_KV_HEADS

    units = [(h, e) for h in range(N_KV_HEADS) for e in range(2)]

    def score(j, h, e):
        rows = slice(j * WINDOW, (j + 1) * WINDOW)
        blk0 = i * tq + j * WINDOW
        bias_prev = band_prev + jnp.where(blk0 >= WINDOW, 0.0, NEG_INF).astype(F32)
        bias_next = band_next + jnp.where(blk0 + WINDOW < seq, 0.0, NEG_INF).astype(F32)
        q2 = jnp.concatenate([q_ref[rows, (2 * h) * LANES:(2 * h + 1) * LANES],
                              q_ref[rows, (2 * h + 1) * LANES:(2 * h + 2) * LANES]], axis=0)
        sl = slice((2 * h + e) * LANES, (2 * h + e + 1) * LANES)
        keys = jnp.concatenate([kall[j * WINDOW:j * WINDOW + nband, sl], kmeta[:, sl]], axis=0)
        s = _dot_nt(keys, q2)
        return jnp.concatenate([s[:WINDOW] + bias_prev, s[WINDOW:2 * WINDOW],
                                s[2 * WINDOW:nband] + bias_next, s[nband:]], axis=0)

    def softmax(h, e, s):
        sink = LOG2E * jnp.where(first_pair, sink_ref[heads_per_kv * h + e], sink_ref[heads_per_kv * h + 2 + e])
        m = jnp.maximum(jnp.max(s, axis=0, keepdims=True), sink)
        return jnp.exp2(s - m).astype(BF16), jnp.exp2(sink - m)

    def value(j, h, pb, p_sink):
        v_all = jnp.concatenate([vall[h * HEAD_DIM:(h + 1) * HEAD_DIM, j * WINDOW:j * WINDOW + nband],
                                 vmeta[h * HEAD_DIM:(h + 1) * HEAD_DIM, :]], axis=1)
        v_all = jnp.concatenate([v_all, ones_all], axis=0)
        o = _dot(v_all, jnp.concatenate([pb, p_pad], axis=0))
        denom = o[HEAD_DIM:HEAD_DIM + 1] + p_sink
        return o[:HEAD_DIM] * (1.0 / denom)

    def write(j, h, halves):
        rows = slice(j * WINDOW, (j + 1) * WINDOW)
        o_t = jnp.concatenate(halves, axis=0)
        for pp in range(2):
            o_ref[rows, (2 * h + pp) * LANES:(2 * h + pp + 1) * LANES] = (
                o_t[:, pp * LANES:(pp + 1) * LANES].T.astype(BF16))

    scores, probs, outs = {}, {}, {}

    def slot(t):
        for h, e in units:
            if 0 <= t - 2 < nblk:
                outs[h, e] = value(t - 2, h, *probs.pop((t - 2, h, e)))
                if e == 1:
                    write(t - 2, h, [outs.pop((h, 0)), outs.pop((h, 1))])
            if t < nblk:
                scores[t, h, e] = score(t, h, e)
            if 0 <= t - 1 < nblk:
                probs[t - 1, h, e] = softmax(h, e, scores.pop((t - 1, h, e)))

    return [functools.partial(slot, t) for t in range(nblk + 2)]


def _attn_kernel(*refs, tq, seq):
    for slot in _attn_slots(pl.program_id(1), *refs, tq=tq, seq=seq):
        slot()


def _attn(q, k4, vt, k4_meta, vt_meta, sink, *, tq):
    B, S, _ = q.shape
    r = tq // WINDOW
    last = S // WINDOW - 1
    prev_idx = lambda i: jnp.maximum(i * r - 1, 0)
    next_idx = lambda i: jnp.minimum((i + 1) * r, last)
    cur = lambda w: pl.BlockSpec((None, tq, w), lambda b, i: (b, i, 0))
    kw = 4 * LANES
    return pl.pallas_call(
        functools.partial(_attn_kernel, tq=tq, seq=S),
        grid=(B, S // tq),
        in_specs=[pl.BlockSpec(memory_space=pltpu.SMEM), cur(ATTN_WIDTH),
                  cur(kw),
                  pl.BlockSpec((None, WINDOW, kw), lambda b, i: (b, prev_idx(i), 0)),
                  pl.BlockSpec((None, WINDOW, kw), lambda b, i: (b, next_idx(i), 0)),
                  pl.BlockSpec((None, KV_WIDTH, tq), lambda b, i: (b, 0, i)),
                  pl.BlockSpec((None, KV_WIDTH, WINDOW), lambda b, i: (b, 0, prev_idx(i))),
                  pl.BlockSpec((None, KV_WIDTH, WINDOW), lambda b, i: (b, 0, next_idx(i))),
                  pl.BlockSpec((N_META, kw), lambda b, i: (0, 0)),
                  pl.BlockSpec((KV_WIDTH, LANES), lambda b, i: (0, 0))],
        out_specs=cur(ATTN_WIDTH),
        out_shape=jax.ShapeDtypeStruct((B, S, ATTN_WIDTH), BF16),
        compiler_params=pltpu.CompilerParams(
            dimension_semantics=("parallel", "parallel"), vmem_limit_bytes=VMEM_LIMIT_BYTES),
        name="attn",
    )(sink, q, k4, k4, k4, vt, vt, vt, k4_meta, vt_meta)


MXU_DEPTH = 256


def _factor(L):
    best = None
    for n1 in range(SUBLANES, L, SUBLANES):
        if L % n1 or (L // n1) % 8 == 0:
            continue
        n2p = _round_up(L // n1, 2 * SUBLANES)
        cost = (-(-2 * n1 // MXU_DEPTH) + -(-2 * n2p // MXU_DEPTH), abs(n1 - L ** 0.5))
        if best is None or cost < best[0]:
            best = (cost, n1)
    return best[1], L // best[1]


def _round_up(x, m):
    return (x + m - 1) // m * m


def _odd_tiles(rows):
    p = _round_up(rows, SUBLANES)
    return p if (p // SUBLANES) % 2 else p + SUBLANES


def _fmix_consts(S):
    L = S + N_META
    N1, N2 = _factor(L)
    N2p = _round_up(N2, 2 * SUBLANES)
    dims = dict(L=L, N1=N1, N2=N2, N2p=N2p, P=_odd_tiles(2 * N1), Q=_odd_tiles(N2p),
                Lp=_round_up(L + N2p - N2, SUBLANES))
    k1 = np.arange(N1, dtype=np.int64)
    ang1 = 2.0 * np.pi * ((k1[:, None] * k1[None, :]) % N1) / N1
    c1, s1 = np.cos(ang1) / np.sqrt(N1), np.sin(ang1) / np.sqrt(N1)
    f1 = np.block([[c1, s1], [-s1, c1]]).astype(np.float32)
    k = k1[:, None, None] + N1 * np.arange(N2p, dtype=np.int64)[None, :, None]
    n2 = np.arange(N2p, dtype=np.int64)[None, None, :]
    ang3 = 2.0 * np.pi * (((n2 + N_META) * k) % L) / L
    valid = ((np.arange(N2p) < N2)[None, :, None] & (np.arange(N2p) < N2)[None, None, :])
    c3 = np.where(valid, np.cos(ang3), 0.0) / np.sqrt(N2)
    s3 = np.where(valid, np.sin(ang3), 0.0) / np.sqrt(N2)
    m3 = np.concatenate([c3, s3], axis=2).astype(np.float32)
    c = np.arange(FOURIER_GROUP_WIDTH, dtype=np.int64)
    angc = 2.0 * np.pi * ((c[:, None] * c[None, :]) % FOURIER_GROUP_WIDTH) / FOURIER_GROUP_WIDTH
    cd = (np.concatenate([np.cos(angc), -np.sin(angc)], axis=1)
          / np.sqrt(FOURIER_GROUP_WIDTH)).astype(np.float32)
    return dims, f1, m3, cd


def _fmix_kernel(u_ref, um_ref, cd_ref, f1_ref, m3_ref, o_ref, *scratch,
                 S, L, Lp, N1, N2, N2p, P, Q, G, NB, chunk):
    W = FOURIER_GROUP_WIDTH
    xr, xi, a_s, y_s = (scratch[i * G:(i + 1) * G] for i in range(4))
    cd = cd_ref[...]
    lanes = lambda g: slice(g * W, (g + 1) * W)

    for r0 in range(0, S, chunk):
        for g in range(G):
            v = _dot(u_ref[r0:r0 + chunk, lanes(g)], cd)
            xr[g][r0:r0 + chunk, :] = v[:, :W]
            xi[g][r0:r0 + chunk, :] = v[:, W:]
    for g in range(G):
        vm = _dot(um_ref[:, lanes(g)], cd)
        xr[g][S:L, :] = vm[:, :W]
        xi[g][S:L, :] = vm[:, W:]
        xr[g][L:Lp, :] = jnp.zeros((Lp - L, W), F32)
        xi[g][L:Lp, :] = jnp.zeros((Lp - L, W), F32)

    for n2_0 in range(0, N2p, NB):
        n2s = range(n2_0, n2_0 + NB)
        cols = [jnp.concatenate([xr[g][pl.ds(n2, N1, stride=N2), :],
                                 xi[g][pl.ds(n2, N1, stride=N2), :]], axis=0)
                for n2 in n2s for g in range(G)]
        a = _dot(f1_ref[...], jnp.concatenate(cols, axis=1).astype(BF16))
        for j, n2 in enumerate(n2s):
            for g in range(G):
                a_s[g][n2 * P:n2 * P + 2 * N1, :] = a[:, lanes(j * G + g)]

    for k1 in range(N1):
        b = jnp.concatenate(
            [jnp.concatenate([a_s[g][pl.ds(k1, N2p, stride=P), :],
                              a_s[g][pl.ds(N1 + k1, N2p, stride=P), :]], axis=0) for g in range(G)], axis=1)
        z = _dot(m3_ref[k1], b.astype(BF16))
        for g in range(G):
            y_s[g][k1 * Q:k1 * Q + N2p, :] = z[:, lanes(g)]

    def pair(g, k2):
        return jnp.concatenate([y_s[g][pl.ds(k2, N1, stride=Q), :],
                                y_s[g][pl.ds(k2 + 1, N1, stride=Q), :]], axis=0)

    for g in range(G):
        o_ref[0:2 * N1 - N_META, lanes(g)] = pair(g, 0)[N_META:].astype(BF16)

    n_pairs = N2 // 2 - 1
    u3 = max(d for d in (4, 2, 1) if n_pairs % d == 0)

    def p3(t, carry):
        for uu in range(u3):
            k2 = 2 * (t * u3 + uu) + 2
            r0 = pl.multiple_of(k2 * N1 - N_META, 2 * SUBLANES)
            for g in range(G):
                o_ref[pl.ds(r0, 2 * N1), lanes(g)] = pair(g, k2).astype(BF16)
        return carry

    lax.fori_loop(0, n_pairs // u3, p3, 0)
    if N2 % 2:
        r0 = (N2 - 1) * N1 - N_META
        for g in range(G):
            o_ref[r0:r0 + N1, lanes(g)] = y_s[g][pl.ds(N2 - 1, N1, stride=Q), :].astype(BF16)


def _fmix_scratch_rows(dims):
    return (dims["Lp"], dims["Lp"], dims["N2p"] * dims["P"], dims["N1"] * dims["Q"])


def _fmix(u, u_meta, *, groups, chunk):
    B, S, _ = u.shape
    dims, f1, m3, cd = _fmix_consts(S)
    W, G = FOURIER_GROUP_WIDTH, groups
    nb = N_FOURIER_GROUPS // G
    assert dims["N2p"] % nb == 0 and S % chunk == 0
    const = lambda shape: pl.BlockSpec(shape, lambda b, g: (0,) * len(shape))
    scratch = [pltpu.VMEM((rows, W), F32) for rows in _fmix_scratch_rows(dims) for _ in range(G)]
    return pl.pallas_call(
        functools.partial(_fmix_kernel, S=S, chunk=chunk, G=G, NB=nb, **dims),
        grid=(B, N_FOURIER_GROUPS // G),
        in_specs=[pl.BlockSpec((None, S, G * W), lambda b, g: (b, 0, g)),
                  pl.BlockSpec((N_META, G * W), lambda b, g: (0, g)),
                  const(cd.shape), const(f1.shape), const(m3.shape)],
        out_specs=pl.BlockSpec((None, S, G * W), lambda b, g: (b, 0, g)),
        out_shape=jax.ShapeDtypeStruct((B, S, FOURIER_WIDTH), BF16),
        scratch_shapes=scratch,
        compiler_params=pltpu.CompilerParams(
            dimension_semantics=("parallel", "parallel"), vmem_limit_bytes=VMEM_LIMIT_BYTES),
        name="fmix",
    )(u, u_meta, jnp.asarray(cd).astype(BF16), jnp.asarray(f1).astype(BF16), jnp.asarray(m3).astype(BF16))


def _post_stages(x_ref, a_ref, f_ref, gate_ref, wao_ref, wf_ref, wout_ref, g2_ref, wup_ref, wdown_ref,
                 gfin_ref, y_ref, *, ff_chunk, sub):
    tm = x_ref.shape[0]
    blocks = [slice(r, r + sub) for r in range(0, tm, sub)]
    state = {}

    def mixer(rows):
        a = _dot(a_ref[rows, :], wao_ref[...])
        f = _dot(f_ref[rows, :], wf_ref[...])
        merged = (gate_ref[rows, :D_MODEL].astype(F32) * a + gate_ref[rows, D_MODEL:].astype(F32) * f).astype(BF16)
        h = x_ref[rows, :] + _dot(merged, wout_ref[...])
        state[rows.start] = (h, _rms(h, g2_ref[...]).astype(BF16))

    def mlp_chunk(c):
        for rows in blocks:
            h, n = state[rows.start]
            t = _dot(n, wup_ref[:, c * ff_chunk:(c + 1) * ff_chunk])
            r = jnp.square(jnp.maximum(t, 0.0)).astype(BF16)
            state[rows.start] = (h + _dot(r, wdown_ref[c * ff_chunk:(c + 1) * ff_chunk, :]), n)

    def final():
        for rows in blocks:
            y_ref[rows, :] = _rms(state[rows.start][0], gfin_ref[...])

    return ([functools.partial(mixer, rows) for rows in blocks]
            + [functools.partial(mlp_chunk, c) for c in range(D_FF // ff_chunk)] + [final])


def _post_kernel(*refs, ff_chunk, sub):
    for stage in _post_stages(*refs, ff_chunk=ff_chunk, sub=sub):
        stage()


def _post(x, a, f, gates, w_ao, w_f, w_out, g2, w_up, w_down, g_fin, *, tm, sub=512, ff_chunk=512):
    B, S, _ = x.shape
    tok = lambda w: pl.BlockSpec((None, tm, w), lambda b, i: (b, i, 0))
    const = lambda shape: pl.BlockSpec(shape, lambda b, i: (0,) * len(shape), pipeline_mode=pl.Buffered(1))
    return pl.pallas_call(
        functools.partial(_post_kernel, ff_chunk=ff_chunk, sub=sub),
        grid=(B, S // tm),
        in_specs=[tok(D_MODEL), tok(ATTN_WIDTH), tok(FOURIER_WIDTH), tok(N_BRANCHES * D_MODEL),
                  const(w_ao.shape), const(w_f.shape), const(w_out.shape), const((1, D_MODEL)),
                  const(w_up.shape), const(w_down.shape), const((1, D_MODEL))],
        out_specs=tok(D_MODEL),
        out_shape=jax.ShapeDtypeStruct((B, S, D_MODEL), F32),
        compiler_params=pltpu.CompilerParams(
            dimension_semantics=("parallel", "parallel"), vmem_limit_bytes=POST_VMEM_LIMIT_BYTES),
        name="post",
    )(x, a, f, gates, w_ao, w_f, w_out, g2, w_up, w_down, g_fin)


def _rope_table(first_pos, n_pos):
    half = ROT_DIM // 2
    inv_freq = ROPE_THETA ** (-np.arange(half, dtype=np.float64) / half)
    ang = np.arange(first_pos, first_pos + n_pos, dtype=np.float64)[:, None] * inv_freq[None, :]
    cos, sin = np.cos(ang), np.sin(ang)
    rest = HEAD_DIM - ROT_DIM
    head = lambda a, b, fill: np.concatenate([a, b, np.full((n_pos, rest), fill)], axis=1)
    zero = np.zeros_like(sin)
    per_head = (head(cos, cos, 1.0), head(zero, sin, 0.0), head(-sin, zero, 0.0))
    table = np.concatenate([np.tile(t, (1, LANES // HEAD_DIM)) for t in per_head], axis=1)
    return jnp.asarray(table.astype(np.float32))


def _plan(S):
    dims = _fmix_consts(S)[0]
    group_bytes = sum(_fmix_scratch_rows(dims)) * FOURIER_GROUP_WIDTH * 4
    return dict(tm_proj=min(S, 1024), tm_post=min(S, 1024), tq=min(S, 16 * WINDOW), fmix_chunk=min(S, 512),
                fmix_groups=N_FOURIER_GROUPS if N_FOURIER_GROUPS * group_bytes <= VMEM_LIMIT_BYTES // 2 else 1)


def _trunk(x, meta_parts, table, wts):
    k4_m, vt_m, u_m = meta_parts
    plan = _plan(x.shape[1])
    q, k4, vt, u, gates = _proj(x, table, wts["norm_mix_g"], wts["w_in"], wts["b_gate"], tm=plan["tm_proj"],
                                transpose_v=True)
    a = _attn(q, k4, vt, k4_m, vt_m, wts["attn_sink"], tq=plan["tq"])
    f = _fmix(u, u_m, groups=plan["fmix_groups"], chunk=plan["fmix_chunk"])
    return _post(x, a, f, gates, wts["w_attn_out"], wts["w_fourier"], wts["w_out"], wts["norm_mlp_g"],
                 wts["w_mlp_up"], wts["w_mlp_down"], wts["norm_final_g"], tm=plan["tm_post"])


def kernel(x_prompt, x_sample, meta_tokens, norm_mix_g, w_in, b_gate, attn_sink, w_attn_out, w_fourier, w_out,
           norm_mlp_g, w_mlp_up, w_mlp_down, norm_final_g):
    assert w_in.shape[0] == 1, "single-layer trunk: meta-token outputs are never consumed"
    wts = dict(
        norm_mix_g=norm_mix_g[0][None, :], w_in=w_in[0].astype(BF16), b_gate=b_gate[0][None, :],
        attn_sink=attn_sink[0], w_attn_out=w_attn_out[0].astype(BF16), w_fourier=w_fourier[0].astype(BF16),
        w_out=w_out[0].astype(BF16), norm_mlp_g=norm_mlp_g[0][None, :], w_mlp_up=w_mlp_up[0].astype(BF16),
        w_mlp_down=w_mlp_down[0].astype(BF16), norm_final_g=norm_final_g[None, :])
    table = _rope_table(N_META, max(x_prompt.shape[1], x_sample.shape[1]))
    _, k4_m, v_m, u_m, _ = _proj(meta_tokens[None], _rope_table(0, N_META),
                                 wts["norm_mix_g"], wts["w_in"], wts["b_gate"], tm=N_META, transpose_v=False)
    vt_m = jnp.pad(v_m[0].T, ((0, 0), (0, LANES - N_META)))
    meta_parts = (k4_m[0], vt_m, u_m[0])
    return (_trunk(x_prompt, meta_parts, table, wts), _trunk(x_sample, meta_parts, table, wts))
```

```python
import functools

import numpy as np
import jax
import jax.numpy as jnp
from jax import lax
from jax.experimental import pallas as pl
from jax.experimental.pallas import tpu as pltpu

D_MODEL = 1024
HEAD_DIM = 64
N_Q_HEADS = 8
N_KV_HEADS = 2
ATTN_WIDTH = N_Q_HEADS * HEAD_DIM
KV_WIDTH = N_KV_HEADS * HEAD_DIM
WINDOW = 128
ROPE_THETA = 500000.0
ROT_DIM = HEAD_DIM // 4
N_FOURIER_GROUPS = 4
FOURIER_GROUP_WIDTH = 128
FOURIER_WIDTH = N_FOURIER_GROUPS * FOURIER_GROUP_WIDTH
N_BRANCHES = 2
IN_WIDTH = ATTN_WIDTH + 2 * KV_WIDTH + FOURIER_WIDTH + N_BRANCHES * D_MODEL
D_FF = 4 * D_MODEL
N_META = 16
RMS_EPS = 1e-6
NEG_INF = -1e30
LOG2E = 1.4426950408889634

LANES = 128
SUBLANES = 8
V7X_VMEM_BYTES = 64 * 1024 * 1024
VMEM_LIMIT_BYTES = 56 * 1024 * 1024
POST_VMEM_LIMIT_BYTES = V7X_VMEM_BYTES - 2 * 1024 * 1024

_Q0, _K0, _U0, _G0 = 0, ATTN_WIDTH, ATTN_WIDTH + 2 * KV_WIDTH, ATTN_WIDTH + 2 * KV_WIDTH + FOURIER_WIDTH

BF16 = jnp.bfloat16
F32 = jnp.float32


def _dot(a, b):
    return jnp.dot(a, b, preferred_element_type=F32)


def _dot_nt(a, b):
    return lax.dot_general(a, b, (((1,), (1,)), ((), ())), preferred_element_type=F32)


def _rms(x, g):
    return x * lax.rsqrt(jnp.mean(x * x, axis=-1, keepdims=True) + RMS_EPS) * g


def _proj_kernel(x_ref, tab_ref, g_ref, w_ref, b_ref, *rest, transpose_v, sub, n_cast):
    cast_in, (q_ref, k4_ref, v_ref, u_ref, gate_ref), cast_out = rest[:n_cast], rest[n_cast:n_cast + 5], \
        rest[n_cast + 5:]
    for src, dst in zip(cast_in, cast_out):
        dst[...] = src[...].astype(BF16)
    tm = x_ref.shape[0]
    blocks = [slice(r, r + sub) for r in range(0, tm, sub)]
    normed = {rows.start: _rms(x_ref[rows, :], g_ref[...]).astype(BF16) for rows in blocks}
    project = lambda rows, lo, hi: _dot(normed[rows.start], w_ref[:, lo:hi])

    def rope(z, rows):
        cos_t, sin_a, sin_b = (tab_ref[rows, c * LANES:(c + 1) * LANES] for c in range(3))
        return (z * cos_t + pltpu.roll(z, ROT_DIM // 2, 1) * sin_a
                + pltpu.roll(z, LANES - ROT_DIM // 2, 1) * sin_b)

    for rows in blocks:
        zq = project(rows, _Q0, _K0)
        for c in range(ATTN_WIDTH // LANES):
            q_ref[rows, c * LANES:(c + 1) * LANES] = (
                rope(zq[:, c * LANES:(c + 1) * LANES], rows) * (LOG2E * HEAD_DIM ** -0.5)).astype(BF16)
    for rows in blocks:
        zkv = project(rows, _K0, _U0)
        zk = rope(zkv[:, :KV_WIDTH], rows)
        zv = zkv[:, KV_WIDTH:]
        zks = pltpu.roll(zk, HEAD_DIM, 1)
        low = lax.broadcasted_iota(jnp.int32, zk.shape, 1) < HEAD_DIM
        for s, (keep_low, src) in enumerate(((True, zk), (False, zks), (True, zks), (False, zk))):
            k4_ref[rows, s * LANES:(s + 1) * LANES] = jnp.where(low == keep_low, src, 0.0).astype(BF16)
        if transpose_v:
            v_ref[:, rows] = zv.T.astype(BF16)
        else:
            v_ref[rows, :] = zv.astype(BF16)
    for rows in blocks:
        u_ref[rows, :] = project(rows, _U0, _G0).astype(BF16)
    for c in range(N_BRANCHES):
        lo = _G0 + c * D_MODEL
        for rows in blocks:
            g = project(rows, lo, lo + D_MODEL) + b_ref[:, c * D_MODEL:(c + 1) * D_MODEL]
            gate_ref[rows, c * D_MODEL:(c + 1) * D_MODEL] = jax.nn.sigmoid(g).astype(BF16)


def _proj(x, table, g, w_in, b_gate, *, tm, transpose_v, cast=()):
    B, S, _ = x.shape
    nt = S // tm
    tok = lambda w: pl.BlockSpec((None, tm, w), lambda b, i: (b, i, 0))
    tab = pl.BlockSpec((tm, 3 * LANES), lambda b, i: (i, 0))
    const = lambda shape: pl.BlockSpec(shape, lambda b, i: (0,) * len(shape))
    out_w = (ATTN_WIDTH, 4 * LANES, KV_WIDTH, FOURIER_WIDTH, N_BRANCHES * D_MODEL)
    out_specs = [tok(w) for w in out_w]
    out_shape = [jax.ShapeDtypeStruct((B, S, w), BF16) for w in out_w]
    if transpose_v:
        out_specs[2] = pl.BlockSpec((None, KV_WIDTH, tm), lambda b, i: (b, 0, i))
        out_shape[2] = jax.ShapeDtypeStruct((B, KV_WIDTH, S), BF16)
    assert all(w.shape[0] % (B * nt * 2 * SUBLANES) == 0 for w in cast)
    cast_specs = [pl.BlockSpec((w.shape[0] // (B * nt), w.shape[1]), lambda b, i: (b * nt + i, 0)) for w in cast]
    return pl.pallas_call(
        functools.partial(_proj_kernel, transpose_v=transpose_v, sub=min(tm, 256), n_cast=len(cast)),
        grid=(B, nt),
        in_specs=[tok(D_MODEL), tab, const((1, D_MODEL)),
                  const((D_MODEL, IN_WIDTH)), const((1, N_BRANCHES * D_MODEL))] + cast_specs,
        out_specs=out_specs + cast_specs,
        out_shape=out_shape + [jax.ShapeDtypeStruct(w.shape, BF16) for w in cast],
        compiler_params=pltpu.CompilerParams(
            dimension_semantics=("parallel", "parallel"), vmem_limit_bytes=VMEM_LIMIT_BYTES),
        name="proj",
    )(x, table, g, w_in, b_gate, *cast)


def _attn_slots(i, sink_ref, q_ref, kc_ref, kp_ref, kn_ref, vc_ref, vp_ref, vn_ref, km_ref, vm_ref,
                o_ref, *, tq, seq):
    nblk = tq // WINDOW
    nband = 3 * WINDOW
    kall =jnp.concatenate([kp_ref[...], kc_ref[...], kn_ref[...]], axis=0)
    vall = jnp.concatenate([vp_ref[...], vc_ref[...], vn_ref[...]], axis=1)
    kmeta, vmeta = km_ref[...], vm_ref[...]
    key = lax.broadcasted_iota(jnp.int32, (WINDOW, 2 * LANES), 0)
    qry = jnp.bitwise_and(lax.broadcasted_iota(jnp.int32, (WINDOW, 2 * LANES), 1), WINDOW - 1)
    band_prev = jnp.where(key >= qry, 0.0, NEG_INF).astype(F32)
    band_next = jnp.where(key <= qry, 0.0, NEG_INF).astype(F32)
    first_pair = lax.broadcasted_iota(jnp.int32, (1, 2 * LANES), 1) < LANES
    p_pad = jnp.zeros((LANES - N_META, 2 * LANES), BF16)
    ones_all = jnp.ones((2 * SUBLANES, nband + LANES), BF16)
    heads_per_kv = N_Q_HEADS // N_KV_HEADS

    units = [(h, e) for h in range(N_KV_HEADS) for e in range(2)]

    def score(j, h, e):
        rows = slice(j * WINDOW, (j + 1) * WINDOW)
        blk0 = i * tq + j * WINDOW
        bias_prev = band_prev + jnp.where(blk0 >= WINDOW, 0.0, NEG_INF).astype(F32)
        bias_next = band_next + jnp.where(blk0 + WINDOW < seq, 0.0, NEG_INF).astype(F32)
        q2 = jnp.concatenate([q_ref[rows, (2 * h) * LANES:(2 * h + 1) * LANES],
                              q_ref[rows, (2 * h + 1) * LANES:(2 * h + 2) * LANES]], axis=0)
        sl = slice((2 * h + e) * LANES, (2 * h + e + 1) * LANES)
        keys = jnp.concatenate([kall[j * WINDOW:j * WINDOW + nband, sl], kmeta[:, sl]], axis=0)
        s = _dot_nt(keys, q2)
        return jnp.concatenate([s[:WINDOW] + bias_prev, s[WINDOW:2 * WINDOW],
                                s[2 * WINDOW:nband] + bias_next, s[nband:]], axis=0)

    def softmax(h, e, s):
        sink = LOG2E * jnp.where(first_pair, sink_ref[heads_per_kv * h + e], sink_ref[heads_per_kv * h + 2 + e])
        m = jnp.maximum(jnp.max(s, axis=0, keepdims=True), sink)
        return jnp.exp2(s - m).astype(BF16), jnp.exp2(sink - m)

    def value(j, h, pb, p_sink):
        v_all = jnp.concatenate([vall[h * HEAD_DIM:(h + 1) * HEAD_DIM, j * WINDOW:j * WINDOW + nband],
                                 vmeta[h * HEAD_DIM:(h + 1) * HEAD_DIM, :]], axis=1)
        v_all = jnp.concatenate([v_all, ones_all], axis=0)
        o = _dot(v_all, jnp.concatenate([pb, p_pad], axis=0))
        denom = o[HEAD_DIM:HEAD_DIM + 1] + p_sink
        return o[:HEAD_DIM] * (1.0 / denom)

    def write(j, h, halves):
        rows = slice(j * WINDOW, (j + 1) * WINDOW)
        o_t = jnp.concatenate(halves, axis=0)
        for pp in range(2):
            o_ref[rows, (2 * h + pp) * LANES:(2 * h + pp + 1) * LANES] = (
                o_t[:, pp * LANES:(pp + 1) * LANES].T.astype(BF16))

    scores, probs, outs = {}, {}, {}

    def slot(t):
        for h, e in units:
            if 0 <= t - 2 < nblk:
                outs[h, e] = value(t - 2, h, *probs.pop((t - 2, h, e)))
                if e == 1:
                    write(t - 2, h, [outs.pop((h, 0)), outs.pop((h, 1))])
            if t < nblk:
                scores[t, h, e] = score(t, h, e)
            if 0 <= t - 1 < nblk:
                probs[t - 1, h, e] = softmax(h, e, scores.pop((t - 1, h, e)))

    return [functools.partial(slot, t) for t in range(nblk + 2)]


def _attn_kernel(*refs, tq, seq):
    for slot in _attn_slots(pl.program_id(1), *refs, tq=tq, seq=seq):
        slot()


def _attn(q, k4, vt, k4_meta, vt_meta, sink, *, tq):
    B, S, _ = q.shape
    r = tq // WINDOW
    last = S // WINDOW - 1
    prev_idx = lambda i: jnp.maximum(i * r - 1, 0)
    next_idx = lambda i: jnp.minimum((i + 1) * r, last)
    cur = lambda w: pl.BlockSpec((None, tq, w), lambda b, i: (b, i, 0))
    kw = 4 * LANES
    return pl.pallas_call(
        functools.partial(_attn_kernel, tq=tq, seq=S),
        grid=(B, S // tq),
        in_specs=[pl.BlockSpec(memory_space=pltpu.SMEM), cur(ATTN_WIDTH),
                  cur(kw),
                  pl.BlockSpec((None, WINDOW, kw), lambda b, i: (b, prev_idx(i), 0)),
                  pl.BlockSpec((None, WINDOW, kw), lambda b, i: (b, next_idx(i), 0)),
                  pl.BlockSpec((None, KV_WIDTH, tq), lambda b, i: (b, 0, i)),
                  pl.BlockSpec((None, KV_WIDTH, WINDOW), lambda b, i: (b, 0, prev_idx(i))),
                  pl.BlockSpec((None, KV_WIDTH, WINDOW), lambda b, i: (b, 0, next_idx(i))),
                  pl.BlockSpec((N_META, kw), lambda b, i: (0, 0)),
                  pl.BlockSpec((KV_WIDTH, LANES), lambda b, i: (0, 0))],
        out_specs=cur(ATTN_WIDTH),
        out_shape=jax.ShapeDtypeStruct((B, S, ATTN_WIDTH), BF16),
        compiler_params=pltpu.CompilerParams(
            dimension_semantics=("parallel", "parallel"), vmem_limit_bytes=VMEM_LIMIT_BYTES),
        name="attn",
    )(sink, q, k4, k4, k4, vt, vt, vt, k4_meta, vt_meta)


MXU_DEPTH = 256


def _factor(L):
    best = None
    for n1 in range(SUBLANES, L, SUBLANES):
        if L % n1 or (L // n1) % 8 == 0:
            continue
        n2p = _round_up(L // n1, 2 * SUBLANES)
        cost = (-(-2 * n1 // MXU_DEPTH) + -(-2 * n2p // MXU_DEPTH), abs(n1 - L ** 0.5))
        if best is None or cost < best[0]:
            best = (cost, n1)
    return best[1], L // best[1]


def _round_up(x, m):
    return (x + m - 1) // m * m


def _odd_tiles(rows):
    p = _round_up(rows, SUBLANES)
    return p if (p // SUBLANES) % 2 else p + SUBLANES


def _fmix_consts(S):
    L = S + N_META
    N1, N2 = _factor(L)
    N2p = _round_up(N2, 2 * SUBLANES)
    dims = dict(L=L, N1=N1, N2=N2, N2p=N2p, P=_odd_tiles(2 * N1), Q=_odd_tiles(N2p),
                Lp=_round_up(L + N2p - N2, SUBLANES))
    k1 = np.arange(N1, dtype=np.int64)
    ang1 = 2.0 * np.pi * ((k1[:, None] * k1[None, :]) % N1) / N1
    c1, s1 = np.cos(ang1) / np.sqrt(N1), np.sin(ang1) / np.sqrt(N1)
    f1 = np.block([[c1, s1], [-s1, c1]]).astype(np.float32)
    k = k1[:, None, None] + N1 * np.arange(N2p, dtype=np.int64)[None, :, None]
    n2 = np.arange(N2p, dtype=np.int64)[None, None, :]
    ang3 = 2.0 * np.pi * (((n2 + N_META) * k) % L) / L
    valid = ((np.arange(N2p) < N2)[None, :, None] & (np.arange(N2p) < N2)[None, None, :])
    c3 = np.where(valid, np.cos(ang3), 0.0) / np.sqrt(N2)
    s3 = np.where(valid, np.sin(ang3), 0.0) / np.sqrt(N2)
    m3 = np.concatenate([c3, s3], axis=2).astype(np.float32)
    c = np.arange(FOURIER_GROUP_WIDTH, dtype=np.int64)
    angc = 2.0 * np.pi * ((c[:, None] * c[None, :]) % FOURIER_GROUP_WIDTH) / FOURIER_GROUP_WIDTH
    cd = (np.concatenate([np.cos(angc), -np.sin(angc)], axis=1)
          / np.sqrt(FOURIER_GROUP_WIDTH)).astype(np.float32)
    return dims, f1, m3, cd


def _fmix_kernel(u_ref, um_ref, cd_ref, f1_ref, m3_ref, o_ref, *scratch,
                 S, L, Lp, N1, N2, N2p, P, Q, G, NB, chunk):
    W = FOURIER_GROUP_WIDTH
    xr, xi, a_s, y_s = (scratch[i * G:(i + 1) * G] for i in range(4))
    cd = cd_ref[...]
    lanes = lambda g: slice(g * W, (g + 1) * W)

    for r0 in range(0, S, chunk):
        for g in range(G):
            v = _dot(u_ref[r0:r0 + chunk, lanes(g)], cd)
            xr[g][r0:r0 + chunk, :] = v[:, :W]
            xi[g][r0:r0 + chunk, :] = v[:, W:]
    for g in range(G):
        vm = _dot(um_ref[:, lanes(g)], cd)
        xr[g][S:L, :] = vm[:, :W]
        xi[g][S:L, :] = vm[:, W:]
        xr[g][L:Lp, :] = jnp.zeros((Lp - L, W), F32)
        xi[g][L:Lp, :] = jnp.zeros((Lp - L, W), F32)

    for n2_0 in range(0, N2p, NB):
        n2s = range(n2_0, n2_0 + NB)
        cols = [jnp.concatenate([xr[g][pl.ds(n2, N1, stride=N2), :],
                                 xi[g][pl.ds(n2, N1, stride=N2), :]], axis=0)
                for n2 in n2s for g in range(G)]
        a = _dot(f1_ref[...], jnp.concatenate(cols, axis=1).astype(BF16))
        for j, n2 in enumerate(n2s):
            for g in range(G):
                a_s[g][n2 * P:n2 * P + 2 * N1, :] = a[:, lanes(j * G + g)]

    for k1 in range(N1):
        b = jnp.concatenate(
            [jnp.concatenate([a_s[g][pl.ds(k1, N2p, stride=P), :],
                              a_s[g][pl.ds(N1 + k1, N2p, stride=P), :]], axis=0) for g in range(G)], axis=1)
        z = _dot(m3_ref[k1], b.astype(BF16))
        for g in range(G):
            y_s[g][k1 * Q:k1 * Q + N2p, :] = z[:, lanes(g)]

    def pair(g, k2):
        return jnp.concatenate([y_s[g][pl.ds(k2, N1, stride=Q), :],
                                y_s[g][pl.ds(k2 + 1, N1, stride=Q), :]], axis=0)

    for g in range(G):
        o_ref[0:2 * N1 - N_META, lanes(g)] = pair(g, 0)[N_META:].astype(BF16)

    n_pairs = N2 // 2 - 1
    u3 = max(d for d in (4, 2, 1) if n_pairs % d == 0)

    def p3(t, carry):
        for uu in range(u3):
            k2 = 2 * (t * u3 + uu) + 2
            r0 = pl.multiple_of(k2 * N1 - N_META, 2 * SUBLANES)
            for g in range(G):
                o_ref[pl.ds(r0, 2 * N1), lanes(g)] = pair(g, k2).astype(BF16)
        return carry

    lax.fori_loop(0, n_pairs // u3, p3, 0)
    if N2 % 2:
        r0 = (N2 - 1) * N1 - N_META
        for g in range(G):
            o_ref[r0:r0 + N1, lanes(g)] = y_s[g][pl.ds(N2 - 1, N1, stride=Q), :].astype(BF16)


def _fmix_scratch_rows(dims):
    return (dims["Lp"], dims["Lp"], dims["N2p"] * dims["P"], dims["N1"] * dims["Q"])


def _fmix(u, u_meta, *, groups, chunk):
    B, S, _ = u.shape
    dims, f1, m3, cd = _fmix_consts(S)
    W, G = FOURIER_GROUP_WIDTH, groups
    nb = N_FOURIER_GROUPS // G
    assert dims["N2p"] % nb == 0 and S % chunk == 0
    const = lambda shape: pl.BlockSpec(shape, lambda b, g: (0,) * len(shape))
    scratch = [pltpu.VMEM((rows, W), F32) for rows in _fmix_scratch_rows(dims) for _ in range(G)]
    return pl.pallas_call(
        functools.partial(_fmix_kernel, S=S, chunk=chunk, G=G, NB=nb, **dims),
        grid=(B, N_FOURIER_GROUPS // G),
        in_specs=[pl.BlockSpec((None, S, G * W), lambda b, g: (b, 0, g)),
                  pl.BlockSpec((N_META, G * W), lambda b, g: (0, g)),
                  const(cd.shape), const(f1.shape), const(m3.shape)],
        out_specs=pl.BlockSpec((None, S, G * W), lambda b, g: (b, 0, g)),
        out_shape=jax.ShapeDtypeStruct((B, S, FOURIER_WIDTH), BF16),
        scratch_shapes=scratch,
        compiler_params=pltpu.CompilerParams(
            dimension_semantics=("parallel", "parallel"), vmem_limit_bytes=VMEM_LIMIT_BYTES),
        name="fmix",
    )(u, u_meta, jnp.asarray(cd).astype(BF16), jnp.asarray(f1).astype(BF16), jnp.asarray(m3).astype(BF16))


def _post_stages(x_ref, a_ref, f_ref, gate_ref, wao_ref, wf_ref, wout_ref, g2_ref, wup_ref, wdown_ref,
                 gfin_ref, y_ref, *, ff_chunk, sub):
    tm = x_ref.shape[0]
    blocks = [slice(r, r + sub) for r in range(0, tm, sub)]
    state = {}

    def mixer(rows):
        a = _dot(a_ref[rows, :], wao_ref[...])
        f = _dot(f_ref[rows, :], wf_ref[...])
        merged = (gate_ref[rows, :D_MODEL].astype(F32) * a + gate_ref[rows, D_MODEL:].astype(F32) * f).astype(BF16)
        h = x_ref[rows, :] + _dot(merged, wout_ref[...])
        state[rows.start] = (h, _rms(h, g2_ref[...]).astype(BF16))

    def mlp_chunk(c):
        for rows in blocks:
            h, n = state[rows.start]
            t = _dot(n, wup_ref[:, c * ff_chunk:(c + 1) * ff_chunk])
            r = jnp.square(jnp.maximum(t, 0.0)).astype(BF16)
            state[rows.start] = (h + _dot(r, wdown_ref[c * ff_chunk:(c + 1) * ff_chunk, :]), n)

    def final():
        for rows in blocks:
            y_ref[rows, :] = _rms(state[rows.start][0], gfin_ref[...])

    return ([functools.partial(mixer, rows) for rows in blocks]
            + [functools.partial(mlp_chunk, c) for c in range(D_FF // ff_chunk)] + [final])


def _post_kernel(*refs, ff_chunk, sub):
    for stage in _post_stages(*refs, ff_chunk=ff_chunk, sub=sub):
        stage()


def _post(x, a, f, gates, w_ao, w_f, w_out, g2, w_up, w_down, g_fin, *, tm, sub=512, ff_chunk=512):
    B, S, _ = x.shape
    tok = lambda w: pl.BlockSpec((None, tm, w), lambda b, i: (b, i, 0))
    const = lambda shape: pl.BlockSpec(shape, lambda b, i: (0,) * len(shape), pipeline_mode=pl.Buffered(1))
    return pl.pallas_call(
        functools.partial(_post_kernel, ff_chunk=ff_chunk, sub=sub),
        grid=(B, S // tm),
        in_specs=[tok(D_MODEL), tok(ATTN_WIDTH), tok(FOURIER_WIDTH), tok(N_BRANCHES * D_MODEL),
                  const(w_ao.shape), const(w_f.shape), const(w_out.shape), const((1, D_MODEL)),
                  const(w_up.shape), const(w_down.shape), const((1, D_MODEL))],
        out_specs=tok(D_MODEL),
        out_shape=jax.ShapeDtypeStruct((B, S, D_MODEL), F32),
        compiler_params=pltpu.CompilerParams(
            dimension_semantics=("parallel", "parallel"), vmem_limit_bytes=POST_VMEM_LIMIT_BYTES),
        name="post",
    )(x, a, f, gates, w_ao, w_f, w_out, g2, w_up, w_down, g_fin)


def _rope_table(first_pos, n_pos):
    half = ROT_DIM // 2
    inv_freq = ROPE_THETA ** (-np.arange(half, dtype=np.float64) / half)
    ang = np.arange(first_pos, first_pos + n_pos, dtype=np.float64)[:, None] * inv_freq[None, :]
    cos, sin = np.cos(ang), np.sin(ang)
    rest = HEAD_DIM - ROT_DIM
    head = lambda a, b, fill: np.concatenate([a, b, np.full((n_pos, rest), fill)], axis=1)
    zero = np.zeros_like(sin)
    per_head = (head(cos, cos, 1.0), head(zero, sin, 0.0), head(-sin, zero, 0.0))
    table = np.concatenate([np.tile(t, (1, LANES // HEAD_DIM)) for t in per_head], axis=1)
    return jnp.asarray(table.astype(np.float32))


def _plan(S):
    dims = _fmix_consts(S)[0]
    group_bytes = sum(_fmix_scratch_rows(dims)) * FOURIER_GROUP_WIDTH * 4
    return dict(tm_proj=min(S, 1024), tm_post=min(S, 1024), tq=min(S, 16 * WINDOW), fmix_chunk=min(S, 512),
                fmix_groups=N_FOURIER_GROUPS if N_FOURIER_GROUPS * group_bytes <= VMEM_LIMIT_BYTES // 2 else 1)


_POST_MATRICES = ("w_attn_out", "w_fourier", "w_out", "w_mlp_up", "w_mlp_down")


def _trunk(x, meta_parts, table, wts, post_w):
    k4_m, vt_m, u_m = meta_parts
    plan = _plan(x.shape[1])
    cast = () if post_w is not None else tuple(wts[k] for k in _POST_MATRICES)
    q, k4, vt, u, gates, *made = _proj(x, table, wts["norm_mix_g"], wts["w_in"], wts["b_gate"],
                                       tm=plan["tm_proj"], transpose_v=True, cast=cast)
    w_ao, w_f, w_out, w_up, w_down = post_w if post_w is not None else made
    a = _attn(q, k4, vt, k4_m, vt_m, wts["attn_sink"], tq=plan["tq"])
    f = _fmix(u, u_m, groups=plan["fmix_groups"], chunk=plan["fmix_chunk"])
    y = _post(x, a, f, gates, w_ao, w_f, w_out, wts["norm_mlp_g"], w_up, w_down, wts["norm_final_g"],
              tm=plan["tm_post"])
    return y, (w_ao, w_f, w_out, w_up, w_down)


def kernel(x_prompt, x_sample, meta_tokens, norm_mix_g, w_in, b_gate, attn_sink, w_attn_out, w_fourier, w_out,
           norm_mlp_g, w_mlp_up, w_mlp_down, norm_final_g):
    assert w_in.shape[0] == 1, "single-layer trunk: meta-token outputs are never consumed"
    wts = dict(
        norm_mix_g=norm_mix_g[0][None, :], w_in=w_in[0].astype(BF16), b_gate=b_gate[0][None, :],
        attn_sink=attn_sink[0], w_attn_out=w_attn_out[0], w_fourier=w_fourier[0], w_out=w_out[0],
        norm_mlp_g=norm_mlp_g[0][None, :], w_mlp_up=w_mlp_up[0], w_mlp_down=w_mlp_down[0],
        norm_final_g=norm_final_g[None, :])
    table = _rope_table(N_META, max(x_prompt.shape[1], x_sample.shape[1]))
    _, k4_m, v_m, u_m, _ = _proj(meta_tokens[None], _rope_table(0, N_META),
                                 wts["norm_mix_g"], wts["w_in"], wts["b_gate"], tm=N_META, transpose_v=False)
    vt_m = jnp.pad(v_m[0].T, ((0, 0), (0, LANES - N_META)))
    meta_parts = (k4_m[0], vt_m, u_m[0])
    y_prompt, post_w = _trunk(x_prompt, meta_parts, table, wts, None)
    y_sample, _ = _trunk(x_sample, meta_parts, table, wts, post_w)
    return (y_prompt, y_sample)
```

```python
import functools

import numpy as np
import jax
import jax.numpy as jnp
from jax import lax
from jax.experimental import pallas as pl
from jax.experimental.pallas import tpu as pltpu

D_MODEL = 1024
HEAD_DIM = 64
N_Q_HEADS = 8
N_KV_HEADS = 2
ATTN_WIDTH = N_Q_HEADS * HEAD_DIM
KV_WIDTH = N_KV_HEADS * HEAD_DIM
WINDOW = 128
ROPE_THETA = 500000.0
ROT_DIM = HEAD_DIM // 4
N_FOURIER_GROUPS = 4
FOURIER_GROUP_WIDTH = 128
FOURIER_WIDTH = N_FOURIER_GROUPS * FOURIER_GROUP_WIDTH
N_BRANCHES = 2
IN_WIDTH = ATTN_WIDTH + 2 * KV_WIDTH + FOURIER_WIDTH + N_BRANCHES * D_MODEL
D_FF = 4 * D_MODEL
N_META = 16
RMS_EPS = 1e-6
NEG_INF = -1e30
LOG2E = 1.4426950408889634

LANES = 128
SUBLANES = 8
V7X_VMEM_BYTES = 64 * 1024 * 1024
VMEM_LIMIT_BYTES = 56 * 1024 * 1024
POST_VMEM_LIMIT_BYTES = V7X_VMEM_BYTES - 2 * 1024 * 1024

_Q0, _K0, _U0, _G0 = 0, ATTN_WIDTH, ATTN_WIDTH + 2 * KV_WIDTH, ATTN_WIDTH + 2 * KV_WIDTH + FOURIER_WIDTH

BF16 = jnp.bfloat16
F32 = jnp.float32


def _dot(a, b):
    return jnp.dot(a, b, preferred_element_type=F32)


def _dot_nt(a, b):
    return lax.dot_general(a, b, (((1,), (1,)), ((), ())), preferred_element_type=F32)


def _rms(x, g):
    return x * lax.rsqrt(jnp.mean(x * x, axis=-1, keepdims=True) + RMS_EPS) * g


def _proj_kernel(x_ref, tab_ref, g_ref, w_ref, b_ref, *rest, transpose_v, sub, n_cast):
    cast_in, (q_ref, k4_ref, v_ref, u_ref, gate_ref), cast_out = rest[:n_cast], rest[n_cast:n_cast + 5], \
        rest[n_cast + 5:]
    for src, dst in zip(cast_in, cast_out):
        dst[...] = src[...].astype(BF16)
    tm = x_ref.shape[0]
    blocks = [slice(r, r + sub) for r in range(0, tm, sub)]
    normed = {rows.start: _rms(x_ref[rows, :], g_ref[...]).astype(BF16) for rows in blocks}
    project = lambda rows, lo, hi: _dot(normed[rows.start], w_ref[:, lo:hi])

    def rope(z, rows):
        cos_t, sin_a, sin_b = (tab_ref[rows, c * LANES:(c + 1) * LANES] for c in range(3))
        return (z * cos_t + pltpu.roll(z, ROT_DIM // 2, 1) * sin_a
                + pltpu.roll(z, LANES - ROT_DIM // 2, 1) * sin_b)

    for rows in blocks:
        zq = project(rows, _Q0, _K0)
        for c in range(ATTN_WIDTH // LANES):
            q_ref[rows, c * LANES:(c + 1) * LANES] = (
                rope(zq[:, c * LANES:(c + 1) * LANES], rows) * (LOG2E * HEAD_DIM ** -0.5)).astype(BF16)
    for rows in blocks:
        zkv = project(rows, _K0, _U0)
        zk = rope(zkv[:, :KV_WIDTH], rows)
        zv = zkv[:, KV_WIDTH:]
        zks = pltpu.roll(zk, HEAD_DIM, 1)
        low = lax.broadcasted_iota(jnp.int32, zk.shape, 1) < HEAD_DIM
        for s, (keep_low, src) in enumerate(((True, zk), (False, zks), (True, zks), (False, zk))):
            k4_ref[rows, s * LANES:(s + 1) * LANES] = jnp.where(low == keep_low, src, 0.0).astype(BF16)
        if transpose_v:
            v_ref[:, rows] = zv.T.astype(BF16)
        else:
            v_ref[rows, :] = zv.astype(BF16)
    for rows in blocks:
        u_ref[rows, :] = project(rows, _U0, _G0).astype(BF16)
    for c in range(N_BRANCHES):
        lo = _G0 + c * D_MODEL
        for rows in blocks:
            g = project(rows, lo, lo + D_MODEL) + b_ref[:, c * D_MODEL:(c + 1) * D_MODEL]
            gate_ref[rows, c * D_MODEL:(c + 1) * D_MODEL] = jax.nn.sigmoid(g).astype(BF16)


def _proj(x, table, g, w_in, b_gate, *, tm, sub, transpose_v, cast=()):
    B, S, _ = x.shape
    nt = S // tm
    tok = lambda w: pl.BlockSpec((None, tm, w), lambda b, i: (b, i, 0))
    tab = pl.BlockSpec((tm, 3 * LANES), lambda b, i: (i, 0))
    const = lambda shape: pl.BlockSpec(shape, lambda b, i: (0,) * len(shape))
    out_w = (ATTN_WIDTH, 4 * LANES, KV_WIDTH, FOURIER_WIDTH, N_BRANCHES * D_MODEL)
    out_specs = [tok(w) for w in out_w]
    out_shape = [jax.ShapeDtypeStruct((B, S, w), BF16) for w in out_w]
    if transpose_v:
        out_specs[2] = pl.BlockSpec((None, KV_WIDTH, tm), lambda b, i: (b, 0, i))
        out_shape[2] = jax.ShapeDtypeStruct((B, KV_WIDTH, S), BF16)
    assert all(w.shape[0] % (B * nt * 2 * SUBLANES) == 0 for w in cast)
    cast_specs = [pl.BlockSpec((w.shape[0] // (B * nt), w.shape[1]), lambda b, i: (b * nt + i, 0)) for w in cast]
    return pl.pallas_call(
        functools.partial(_proj_kernel, transpose_v=transpose_v, sub=sub, n_cast=len(cast)),
        grid=(B, nt),
        in_specs=[tok(D_MODEL), tab, const((1, D_MODEL)),
                  const((D_MODEL, IN_WIDTH)), const((1, N_BRANCHES * D_MODEL))] + cast_specs,
        out_specs=out_specs + cast_specs,
        out_shape=out_shape + [jax.ShapeDtypeStruct(w.shape, BF16) for w in cast],
        compiler_params=pltpu.CompilerParams(
            dimension_semantics=("parallel", "parallel"), vmem_limit_bytes=VMEM_LIMIT_BYTES),
        name="proj",
    )(x, table, g, w_in, b_gate, *cast)


def _attn_slots(i, sink_ref, q_ref, kc_ref, kp_ref, kn_ref, vc_ref, vp_ref, vn_ref, km_ref, vm_ref,
                o_ref, *, tq, seq):
    nblk = tq // WINDOW
    nband = 3 * WINDOW
    kall =jnp.concatenate([kp_ref[...], kc_ref[...], kn_ref[...]], axis=0)
    vall = jnp.concatenate([vp_ref[...], vc_ref[...], vn_ref[...]], axis=1)
    kmeta, vmeta = km_ref[...], vm_ref[...]
    key = lax.broadcasted_iota(jnp.int32, (WINDOW, 2 * LANES), 0)
    qry = jnp.bitwise_and(lax.broadcasted_iota(jnp.int32, (WINDOW, 2 * LANES), 1), WINDOW - 1)
    band_prev = jnp.where(key >= qry, 0.0, NEG_INF).astype(F32)
    band_next = jnp.where(key <= qry, 0.0, NEG_INF).astype(F32)
    first_pair = lax.broadcasted_iota(jnp.int32, (1, 2 * LANES), 1) < LANES
    p_pad = jnp.zeros((LANES - N_META, 2 * LANES), BF16)
    ones_all = jnp.ones((2 * SUBLANES, nband + LANES), BF16)
    heads_per_kv = N_Q_HEADS // N_KV_HEADS

    units = [(h, e) for h in range(N_KV_HEADS) for e in range(2)]

    def score(j, h, e):
        rows = slice(j * WINDOW, (j + 1) * WINDOW)
        blk0 = i * tq + j * WINDOW
        bias_prev = band_prev + jnp.where(blk0 >= WINDOW, 0.0, NEG_INF).astype(F32)
        bias_next = band_next + jnp.where(blk0 + WINDOW < seq, 0.0, NEG_INF).astype(F32)
        q2 = jnp.concatenate([q_ref[rows, (2 * h) * LANES:(2 * h + 1) * LANES],
                              q_ref[rows, (2 * h + 1) * LANES:(2 * h + 2) * LANES]], axis=0)
        sl = slice((2 * h + e) * LANES, (2 * h + e + 1) * LANES)
        keys = jnp.concatenate([kall[j * WINDOW:j * WINDOW + nband, sl], kmeta[:, sl]], axis=0)
        s = _dot_nt(keys, q2)
        return jnp.concatenate([s[:WINDOW] + bias_prev, s[WINDOW:2 * WINDOW],
                                s[2 * WINDOW:nband] + bias_next, s[nband:]], axis=0)

    def softmax(h, e, s):
        sink = LOG2E * jnp.where(first_pair, sink_ref[heads_per_kv * h + e], sink_ref[heads_per_kv * h + 2 + e])
        m = jnp.maximum(jnp.max(s, axis=0, keepdims=True), sink)
        return jnp.exp2(s - m).astype(BF16), jnp.exp2(sink - m)

    def value(j, h, pb, p_sink):
        v_all = jnp.concatenate([vall[h * HEAD_DIM:(h + 1) * HEAD_DIM, j * WINDOW:j * WINDOW + nband],
                                 vmeta[h * HEAD_DIM:(h + 1) * HEAD_DIM, :]], axis=1)
        v_all = jnp.concatenate([v_all, ones_all], axis=0)
        o = _dot(v_all, jnp.concatenate([pb, p_pad], axis=0))
        denom = o[HEAD_DIM:HEAD_DIM + 1] + p_sink
        return o[:HEAD_DIM] * (1.0 / denom)

    def write(j, h, halves):
        rows = slice(j * WINDOW, (j + 1) * WINDOW)
        o_t = jnp.concatenate(halves, axis=0)
        for pp in range(2):
            o_ref[rows, (2 * h + pp) * LANES:(2 * h + pp + 1) * LANES] = (
                o_t[:, pp * LANES:(pp + 1) * LANES].T.astype(BF16))

    scores, probs, outs = {}, {}, {}

    def slot(t):
        for h, e in units:
            if 0 <= t - 2 < nblk:
                outs[h, e] = value(t - 2, h, *probs.pop((t - 2, h, e)))
                if e == 1:
                    write(t - 2, h, [outs.pop((h, 0)), outs.pop((h, 1))])
            if t < nblk:
                scores[t, h, e] = score(t, h, e)
            if 0 <= t - 1 < nblk:
                probs[t - 1, h, e] = softmax(h, e, scores.pop((t - 1, h, e)))

    return [functools.partial(slot, t) for t in range(nblk + 2)]


def _attn_kernel(*refs, tq, seq):
    for slot in _attn_slots(pl.program_id(1), *refs, tq=tq, seq=seq):
        slot()


def _attn(q, k4, vt, k4_meta, vt_meta, sink, *, tq):
    B, S, _ = q.shape
    r = tq // WINDOW
    last = S // WINDOW - 1
    prev_idx = lambda i: jnp.maximum(i * r - 1, 0)
    next_idx = lambda i: jnp.minimum((i + 1) * r, last)
    cur = lambda w: pl.BlockSpec((None, tq, w), lambda b, i: (b, i, 0))
    kw = 4 * LANES
    return pl.pallas_call(
        functools.partial(_attn_kernel, tq=tq, seq=S),
        grid=(B, S // tq),
        in_specs=[pl.BlockSpec(memory_space=pltpu.SMEM), cur(ATTN_WIDTH),
                  cur(kw),
                  pl.BlockSpec((None, WINDOW, kw), lambda b, i: (b, prev_idx(i), 0)),
                  pl.BlockSpec((None, WINDOW, kw), lambda b, i: (b, next_idx(i), 0)),
                  pl.BlockSpec((None, KV_WIDTH, tq), lambda b, i: (b, 0, i)),
                  pl.BlockSpec((None, KV_WIDTH, WINDOW), lambda b, i: (b, 0, prev_idx(i))),
                  pl.BlockSpec((None, KV_WIDTH, WINDOW), lambda b, i: (b, 0, next_idx(i))),
                  pl.BlockSpec((N_META, kw), lambda b, i: (0, 0)),
                  pl.BlockSpec((KV_WIDTH, LANES), lambda b, i: (0, 0))],
        out_specs=cur(ATTN_WIDTH),
        out_shape=jax.ShapeDtypeStruct((B, S, ATTN_WIDTH), BF16),
        compiler_params=pltpu.CompilerParams(
            dimension_semantics=("parallel", "parallel"), vmem_limit_bytes=VMEM_LIMIT_BYTES),
        name="attn",
    )(sink, q, k4, k4, k4, vt, vt, vt, k4_meta, vt_meta)


MXU_DEPTH = 256


def _factor(L):
    best = None
    for n1 in range(SUBLANES, L, SUBLANES):
        if L % n1 or (L // n1) % 8 == 0:
            continue
        n2p = _round_up(L // n1, 2 * SUBLANES)
        cost = (-(-2 * n1 // MXU_DEPTH) + -(-2 * n2p // MXU_DEPTH), abs(n1 - L ** 0.5))
        if best is None or cost < best[0]:
            best = (cost, n1)
    return best[1], L // best[1]


def _round_up(x, m):
    return (x + m - 1) // m * m


def _odd_tiles(rows):
    p = _round_up(rows, SUBLANES)
    return p if (p // SUBLANES) % 2 else p + SUBLANES


def _fmix_consts(S):
    L = S + N_META
    N1, N2 = _factor(L)
    N2p = _round_up(N2, 2 * SUBLANES)
    dims = dict(L=L, N1=N1, N2=N2, N2p=N2p, P=_odd_tiles(2 * N1), Q=_odd_tiles(N2p),
                Lp=_round_up(L + N2p - N2, SUBLANES))
    k1 = np.arange(N1, dtype=np.int64)
    ang1 = 2.0 * np.pi * ((k1[:, None] * k1[None, :]) % N1) / N1
    c1, s1 = np.cos(ang1) / np.sqrt(N1), np.sin(ang1) / np.sqrt(N1)
    f1 = np.block([[c1, s1], [-s1, c1]]).astype(np.float32)
    k = k1[:, None, None] + N1 * np.arange(N2p, dtype=np.int64)[None, :, None]
    n2 = np.arange(N2p, dtype=np.int64)[None, None, :]
    ang3 = 2.0 * np.pi * (((n2 + N_META) * k) % L) / L
    valid = ((np.arange(N2p) < N2)[None, :, None] & (np.arange(N2p) < N2)[None, None, :])
    c3 = np.where(valid, np.cos(ang3), 0.0) / np.sqrt(N2)
    s3 = np.where(valid, np.sin(ang3), 0.0) / np.sqrt(N2)
    m3 = np.concatenate([c3, s3], axis=2).astype(np.float32)
    c = np.arange(FOURIER_GROUP_WIDTH, dtype=np.int64)
    angc = 2.0 * np.pi * ((c[:, None] * c[None, :]) % FOURIER_GROUP_WIDTH) / FOURIER_GROUP_WIDTH
    cd = (np.concatenate([np.cos(angc), -np.sin(angc)], axis=1)
          / np.sqrt(FOURIER_GROUP_WIDTH)).astype(np.float32)
    return dims, f1, m3, cd


def _fmix_kernel(u_ref, um_ref, cd_ref, f1_ref, m3_ref, o_ref, *scratch,
                 S, L, Lp, N1, N2, N2p, P, Q, G, NB, chunk):
    W = FOURIER_GROUP_WIDTH
    xr, xi, a_s, y_s = (scratch[i * G:(i + 1) * G] for i in range(4))
    cd = cd_ref[...]
    lanes = lambda g: slice(g * W, (g + 1) * W)

    for r0 in range(0, S, chunk):
        for g in range(G):
            v = _dot(u_ref[r0:r0 + chunk, lanes(g)], cd)
            xr[g][r0:r0 + chunk, :] = v[:, :W]
            xi[g][r0:r0 + chunk, :] = v[:, W:]
    for g in range(G):
        vm = _dot(um_ref[:, lanes(g)], cd)
        xr[g][S:L, :] = vm[:, :W]
        xi[g][S:L, :] = vm[:, W:]
        xr[g][L:Lp, :] = jnp.zeros((Lp - L, W), F32)
        xi[g][L:Lp, :] = jnp.zeros((Lp - L, W), F32)

    for n2_0 in range(0, N2p, NB):
        n2s = range(n2_0, n2_0 + NB)
        cols = [jnp.concatenate([xr[g][pl.ds(n2, N1, stride=N2), :],
                                 xi[g][pl.ds(n2, N1, stride=N2), :]], axis=0)
                for n2 in n2s for g in range(G)]
        a = _dot(f1_ref[...], jnp.concatenate(cols, axis=1).astype(BF16))
        for j, n2 in enumerate(n2s):
            for g in range(G):
                a_s[g][n2 * P:n2 * P + 2 * N1, :] = a[:, lanes(j * G + g)]

    for k1 in range(N1):
        b = jnp.concatenate(
            [jnp.concatenate([a_s[g][pl.ds(k1, N2p, stride=P), :],
                              a_s[g][pl.ds(N1 + k1, N2p, stride=P), :]], axis=0) for g in range(G)], axis=1)
        z = _dot(m3_ref[k1], b.astype(BF16))
        for g in range(G):
            y_s[g][k1 * Q:k1 * Q + N2p, :] = z[:, lanes(g)]

    def pair(g, k2):
        return jnp.concatenate([y_s[g][pl.ds(k2, N1, stride=Q), :],
                                y_s[g][pl.ds(k2 + 1, N1, stride=Q), :]], axis=0)

    for g in range(G):
        o_ref[0:2 * N1 - N_META, lanes(g)] = pair(g, 0)[N_META:].astype(BF16)

    n_pairs = N2 // 2 - 1
    u3 = max(d for d in (4, 2, 1) if n_pairs % d == 0)

    def p3(t, carry):
        for uu in range(u3):
            k2 = 2 * (t * u3 + uu) + 2
            r0 = pl.multiple_of(k2 * N1 - N_META, 2 * SUBLANES)
            for g in range(G):
                o_ref[pl.ds(r0, 2 * N1), lanes(g)] = pair(g, k2).astype(BF16)
        return carry

    lax.fori_loop(0, n_pairs // u3, p3, 0)
    if N2 % 2:
        r0 = (N2 - 1) * N1 - N_META
        for g in range(G):
            o_ref[r0:r0 + N1, lanes(g)] = y_s[g][pl.ds(N2 - 1, N1, stride=Q), :].astype(BF16)


def _fmix_scratch_rows(dims):
    return (dims["Lp"], dims["Lp"], dims["N2p"] * dims["P"], dims["N1"] * dims["Q"])


def _fmix(u, u_meta, *, groups, chunk):
    B, S, _ = u.shape
    dims, f1, m3, cd = _fmix_consts(S)
    W, G = FOURIER_GROUP_WIDTH, groups
    nb = N_FOURIER_GROUPS // G
    assert dims["N2p"] % nb == 0 and S % chunk == 0
    const = lambda shape: pl.BlockSpec(shape, lambda b, g: (0,) * len(shape))
    scratch = [pltpu.VMEM((rows, W), F32) for rows in _fmix_scratch_rows(dims) for _ in range(G)]
    return pl.pallas_call(
        functools.partial(_fmix_kernel, S=S, chunk=chunk, G=G, NB=nb, **dims),
        grid=(B, N_FOURIER_GROUPS // G),
        in_specs=[pl.BlockSpec((None, S, G * W), lambda b, g: (b, 0, g)),
                  pl.BlockSpec((N_META, G * W), lambda b, g: (0, g)),
                  const(cd.shape), const(f1.shape), const(m3.shape)],
        out_specs=pl.BlockSpec((None, S, G * W), lambda b, g: (b, 0, g)),
        out_shape=jax.ShapeDtypeStruct((B, S, FOURIER_WIDTH), BF16),
        scratch_shapes=scratch,
        compiler_params=pltpu.CompilerParams(
            dimension_semantics=("parallel", "parallel"), vmem_limit_bytes=VMEM_LIMIT_BYTES),
        name="fmix",
    )(u, u_meta, jnp.asarray(cd).astype(BF16), jnp.asarray(f1).astype(BF16), jnp.asarray(m3).astype(BF16))


def _post_stages(x_ref, a_ref, f_ref, gate_ref, wao_ref, wf_ref, wout_ref, g2_ref, wup_ref, wdown_ref,
                 gfin_ref, y_ref, *, ff_chunk, sub):
    tm = x_ref.shape[0]
    blocks = [slice(r, r + sub) for r in range(0, tm, sub)]
    state = {}

    def mixer(rows):
        a = _dot(a_ref[rows, :], wao_ref[...])
        f = _dot(f_ref[rows, :], wf_ref[...])
        merged = (gate_ref[rows, :D_MODEL].astype(F32) * a + gate_ref[rows, D_MODEL:].astype(F32) * f).astype(BF16)
        h = x_ref[rows, :] + _dot(merged, wout_ref[...])
        state[rows.start] = (h, _rms(h, g2_ref[...]).astype(BF16))

    def mlp_chunk(c):
        for rows in blocks:
            h, n = state[rows.start]
            t = _dot(n, wup_ref[:, c * ff_chunk:(c + 1) * ff_chunk])
            r = jnp.square(jnp.maximum(t, 0.0)).astype(BF16)
            state[rows.start] = (h + _dot(r, wdown_ref[c * ff_chunk:(c + 1) * ff_chunk, :]), n)

    def final():
        for rows in blocks:
            y_ref[rows, :] = _rms(state[rows.start][0], gfin_ref[...])

    return ([functools.partial(mixer, rows) for rows in blocks]
            + [functools.partial(mlp_chunk, c) for c in range(D_FF // ff_chunk)] + [final])


def _post_kernel(*refs, ff_chunk, sub):
    for stage in _post_stages(*refs, ff_chunk=ff_chunk, sub=sub):
        stage()


def _post(x, a, f, gates, w_ao, w_f, w_out, g2, w_up, w_down, g_fin, *, tm, sub, ff_chunk):
    B, S, _ = x.shape
    tok = lambda w: pl.BlockSpec((None, tm, w), lambda b, i: (b, i, 0))
    const = lambda shape: pl.BlockSpec(shape, lambda b, i: (0,) * len(shape), pipeline_mode=pl.Buffered(1))
    return pl.pallas_call(
        functools.partial(_post_kernel, ff_chunk=ff_chunk, sub=sub),
        grid=(B, S // tm),
        in_specs=[tok(D_MODEL), tok(ATTN_WIDTH), tok(FOURIER_WIDTH), tok(N_BRANCHES * D_MODEL),
                  const(w_ao.shape), const(w_f.shape), const(w_out.shape), const((1, D_MODEL)),
                  const(w_up.shape), const(w_down.shape), const((1, D_MODEL))],
        out_specs=tok(D_MODEL),
        out_shape=jax.ShapeDtypeStruct((B, S, D_MODEL), F32),
        compiler_params=pltpu.CompilerParams(
            dimension_semantics=("parallel", "parallel"), vmem_limit_bytes=POST_VMEM_LIMIT_BYTES),
        name="post",
    )(x, a, f, gates, w_ao, w_f, w_out, g2, w_up, w_down, g_fin)


def _rope_table(first_pos, n_pos):
    half = ROT_DIM // 2
    inv_freq = ROPE_THETA ** (-np.arange(half, dtype=np.float64) / half)
    ang = np.arange(first_pos, first_pos + n_pos, dtype=np.float64)[:, None] * inv_freq[None, :]
    cos, sin = np.cos(ang), np.sin(ang)
    rest = HEAD_DIM - ROT_DIM
    head = lambda a, b, fill: np.concatenate([a, b, np.full((n_pos, rest), fill)], axis=1)
    zero = np.zeros_like(sin)
    per_head = (head(cos, cos, 1.0), head(zero, sin, 0.0), head(-sin, zero, 0.0))
    table = np.concatenate([np.tile(t, (1, LANES // HEAD_DIM)) for t in per_head], axis=1)
    return jnp.asarray(table.astype(np.float32))


def _plan(S):
    dims = _fmix_consts(S)[0]
    group_bytes = sum(_fmix_scratch_rows(dims)) * FOURIER_GROUP_WIDTH * 4
    return dict(tm_proj=min(S, 1024), sub_proj=min(S, 256), tm_post=min(S, 1024), sub_post=min(S, 512),
                ff_chunk=512, tq=min(S, 16 * WINDOW), fmix_chunk=min(S, 512),
                fmix_groups=N_FOURIER_GROUPS if N_FOURIER_GROUPS * group_bytes <= VMEM_LIMIT_BYTES // 2 else 1)


_POST_MATRICES = ("w_attn_out", "w_fourier", "w_out", "w_mlp_up", "w_mlp_down")


def _trunk(x, meta_parts, table, wts, post_w):
    k4_m, vt_m, u_m = meta_parts
    plan = _plan(x.shape[1])
    cast = () if post_w is not None else tuple(wts[k] for k in _POST_MATRICES)
    q, k4, vt, u, gates, *made = _proj(x, table, wts["norm_mix_g"], wts["w_in"], wts["b_gate"],
                                       tm=plan["tm_proj"], sub=plan["sub_proj"], transpose_v=True, cast=cast)
    w_ao, w_f, w_out, w_up, w_down = post_w if post_w is not None else made
    a = _attn(q, k4, vt, k4_m, vt_m, wts["attn_sink"], tq=plan["tq"])
    f = _fmix(u, u_m, groups=plan["fmix_groups"], chunk=plan["fmix_chunk"])
    y = _post(x, a, f, gates, w_ao, w_f, w_out, wts["norm_mlp_g"], w_up, w_down, wts["norm_final_g"],
              tm=plan["tm_post"], sub=plan["sub_post"], ff_chunk=plan["ff_chunk"])
    return y, (w_ao, w_f, w_out, w_up, w_down)


def kernel(x_prompt, x_sample, meta_tokens, norm_mix_g, w_in, b_gate, attn_sink, w_attn_out, w_fourier, w_out,
           norm_mlp_g, w_mlp_up, w_mlp_down, norm_final_g):
    assert w_in.shape[0] == 1, "single-layer trunk: meta-token outputs are never consumed"
    wts = dict(
        norm_mix_g=norm_mix_g[0][None, :], w_in=w_in[0].astype(BF16), b_gate=b_gate[0][None, :],
        attn_sink=attn_sink[0], w_attn_out=w_attn_out[0], w_fourier=w_fourier[0], w_out=w_out[0],
        norm_mlp_g=norm_mlp_g[0][None, :], w_mlp_up=w_mlp_up[0], w_mlp_down=w_mlp_down[0],
        norm_final_g=norm_final_g[None, :])
    table = _rope_table(N_META, max(x_prompt.shape[1], x_sample.shape[1]))
    _, k4_m, v_m, u_m, _ = _proj(meta_tokens[None], _rope_table(0, N_META),
                                 wts["norm_mix_g"], wts["w_in"], wts["b_gate"], tm=N_META, sub=N_META,
                                 transpose_v=False)
    vt_m = jnp.pad(v_m[0].T, ((0, 0), (0, LANES - N_META)))
    meta_parts = (k4_m[0], vt_m, u_m[0])
    y_prompt, post_w = _trunk(x_prompt, meta_parts, table, wts, None)
    y_sample, _ = _trunk(x_sample, meta_parts, table, wts, post_w)
    return (y_prompt, y_sample)
```

```python
import functools

import numpy as np
import jax
import jax.numpy as jnp
from jax import lax
from jax.experimental import pallas as pl
from jax.experimental.pallas import tpu as pltpu

D_MODEL = 1024
HEAD_DIM = 64
N_Q_HEADS = 8
N_KV_HEADS = 2
ATTN_WIDTH = N_Q_HEADS * HEAD_DIM
KV_WIDTH = N_KV_HEADS * HEAD_DIM
WINDOW = 128
ROPE_THETA = 500000.0
ROT_DIM = HEAD_DIM // 4
N_FOURIER_GROUPS = 4
FOURIER_GROUP_WIDTH = 128
FOURIER_WIDTH = N_FOURIER_GROUPS * FOURIER_GROUP_WIDTH
N_BRANCHES = 2
IN_WIDTH = ATTN_WIDTH + 2 * KV_WIDTH + FOURIER_WIDTH + N_BRANCHES * D_MODEL
D_FF = 4 * D_MODEL
N_META = 16
RMS_EPS = 1e-6
NEG_INF = -1e30
LOG2E = 1.4426950408889634

LANES = 128
SUBLANES = 8
V7X_VMEM_BYTES = 64 * 1024 * 1024
VMEM_LIMIT_BYTES = 56 * 1024 * 1024
POST_VMEM_LIMIT_BYTES = V7X_VMEM_BYTES - 2 * 1024 * 1024

_Q0, _K0, _U0, _G0 = 0, ATTN_WIDTH, ATTN_WIDTH + 2 * KV_WIDTH, ATTN_WIDTH + 2 * KV_WIDTH + FOURIER_WIDTH

BF16 = jnp.bfloat16
F32 = jnp.float32


def _dot(a, b):
    return jnp.dot(a, b, preferred_element_type=F32)


def _dot_nt(a, b):
    return lax.dot_general(a, b, (((1,), (1,)), ((), ())), preferred_element_type=F32)


def _rms(x, g):
    return x * lax.rsqrt(jnp.mean(x * x, axis=-1, keepdims=True) + RMS_EPS) * g


def _proj_kernel(x_ref, tab_ref, g_ref, w_ref, b_ref, *rest, transpose_v, sub, n_cast):
    cast_in, (q_ref, k4_ref, v_ref, u_ref, gate_ref), cast_out = rest[:n_cast], rest[n_cast:n_cast + 5], \
        rest[n_cast + 5:]
    for src, dst in zip(cast_in, cast_out):
        dst[...] = src[...].astype(BF16)
    tm = x_ref.shape[0]
    blocks = [slice(r, r + sub) for r in range(0, tm, sub)]
    normed = {rows.start: _rms(x_ref[rows, :], g_ref[...]).astype(BF16) for rows in blocks}
    project = lambda rows, lo, hi: _dot(normed[rows.start], w_ref[:, lo:hi])

    def rope(z, rows):
        cos_t, sin_a, sin_b = (tab_ref[rows, c * LANES:(c + 1) * LANES] for c in range(3))
        return (z * cos_t + pltpu.roll(z, ROT_DIM // 2, 1) * sin_a
                + pltpu.roll(z, LANES - ROT_DIM // 2, 1) * sin_b)

    for rows in blocks:
        zq = project(rows, _Q0, _K0)
        for c in range(ATTN_WIDTH // LANES):
            q_ref[rows, c * LANES:(c + 1) * LANES] = (
                rope(zq[:, c * LANES:(c + 1) * LANES], rows) * (LOG2E * HEAD_DIM ** -0.5)).astype(BF16)
    for rows in blocks:
        zkv = project(rows, _K0, _U0)
        zk = rope(zkv[:, :KV_WIDTH], rows)
        zv = zkv[:, KV_WIDTH:]
        zks = pltpu.roll(zk, HEAD_DIM, 1)
        low = lax.broadcasted_iota(jnp.int32, zk.shape, 1) < HEAD_DIM
        for s, (keep_low, src) in enumerate(((True, zk), (False, zks), (True, zks), (False, zk))):
            k4_ref[rows, s * LANES:(s + 1) * LANES] = jnp.where(low == keep_low, src, 0.0).astype(BF16)
        if transpose_v:
            v_ref[:, rows] = zv.T.astype(BF16)
        else:
            v_ref[rows, :] = zv.astype(BF16)
    for rows in blocks:
        u_ref[rows, :] = project(rows, _U0, _G0).astype(BF16)
    for c in range(N_BRANCHES):
        lo = _G0 + c * D_MODEL
        for rows in blocks:
            g = project(rows, lo, lo + D_MODEL) + b_ref[:, c * D_MODEL:(c + 1) * D_MODEL]
            gate_ref[rows, c * D_MODEL:(c + 1) * D_MODEL] = jax.nn.sigmoid(g).astype(BF16)


def _proj(x, table, g, w_in, b_gate, *, tm, sub, transpose_v, cast=()):
    B, S, _ = x.shape
    nt = S // tm
    tok = lambda w: pl.BlockSpec((None, tm, w), lambda b, i: (b, i, 0))
    tab = pl.BlockSpec((tm, 3 * LANES), lambda b, i: (i, 0))
    const = lambda shape: pl.BlockSpec(shape, lambda b, i: (0,) * len(shape))
    out_w = (ATTN_WIDTH, 4 * LANES, KV_WIDTH, FOURIER_WIDTH, N_BRANCHES * D_MODEL)
    out_specs = [tok(w) for w in out_w]
    out_shape = [jax.ShapeDtypeStruct((B, S, w), BF16) for w in out_w]
    if transpose_v:
        out_specs[2] = pl.BlockSpec((None, KV_WIDTH, tm), lambda b, i: (b, 0, i))
        out_shape[2] = jax.ShapeDtypeStruct((B, KV_WIDTH, S), BF16)
    assert all(w.shape[0] % (B * nt * 2 * SUBLANES) == 0 for w in cast)
    cast_specs = [pl.BlockSpec((w.shape[0] // (B * nt), w.shape[1]), lambda b, i: (b * nt + i, 0)) for w in cast]
    return pl.pallas_call(
        functools.partial(_proj_kernel, transpose_v=transpose_v, sub=sub, n_cast=len(cast)),
        grid=(B, nt),
        in_specs=[tok(D_MODEL), tab, const((1, D_MODEL)),
                  const((D_MODEL, IN_WIDTH)), const((1, N_BRANCHES * D_MODEL))] + cast_specs,
        out_specs=out_specs + cast_specs,
        out_shape=out_shape + [jax.ShapeDtypeStruct(w.shape, BF16) for w in cast],
        compiler_params=pltpu.CompilerParams(
            dimension_semantics=("parallel", "parallel"), vmem_limit_bytes=VMEM_LIMIT_BYTES),
        name="proj",
    )(x, table, g, w_in, b_gate, *cast)


def _attn_slots(i, sink_ref, q_ref, kc_ref, kp_ref, kn_ref, vc_ref, vp_ref, vn_ref, km_ref, vm_ref,
                o_ref, *, tq, seq):
    nblk = tq // WINDOW
    nband = 3 * WINDOW
    kall =jnp.concatenate([kp_ref[...], kc_ref[...], kn_ref[...]], axis=0)
    vall = jnp.concatenate([vp_ref[...], vc_ref[...], vn_ref[...]], axis=1)
    kmeta, vmeta = km_ref[...], vm_ref[...]
    key = lax.broadcasted_iota(jnp.int32, (WINDOW, 2 * LANES), 0)
    qry = jnp.bitwise_and(lax.broadcasted_iota(jnp.int32, (WINDOW, 2 * LANES), 1), WINDOW - 1)
    band_prev = jnp.where(key >= qry, 0.0, NEG_INF).astype(F32)
    band_next = jnp.where(key <= qry, 0.0, NEG_INF).astype(F32)
    first_pair = lax.broadcasted_iota(jnp.int32, (1, 2 * LANES), 1) < LANES
    p_pad = jnp.zeros((LANES - N_META, 2 * LANES), BF16)
    ones_all = jnp.ones((2 * SUBLANES, nband + LANES), BF16)
    heads_per_kv = N_Q_HEADS // N_KV_HEADS

    units = [(h, e) for h in range(N_KV_HEADS) for e in range(2)]

    def score(j, h, e):
        rows = slice(j * WINDOW, (j + 1) * WINDOW)
        blk0 = i * tq + j * WINDOW
        bias_prev = band_prev + jnp.where(blk0 >= WINDOW, 0.0, NEG_INF).astype(F32)
        bias_next = band_next + jnp.where(blk0 + WINDOW < seq, 0.0, NEG_INF).astype(F32)
        q2 = jnp.concatenate([q_ref[rows, (2 * h) * LANES:(2 * h + 1) * LANES],
                              q_ref[rows, (2 * h + 1) * LANES:(2 * h + 2) * LANES]], axis=0)
        sl = slice((2 * h + e) * LANES, (2 * h + e + 1) * LANES)
        keys = jnp.concatenate([kall[j * WINDOW:j * WINDOW + nband, sl], kmeta[:, sl]], axis=0)
        s = _dot_nt(keys, q2)
        return jnp.concatenate([s[:WINDOW] + bias_prev, s[WINDOW:2 * WINDOW],
                                s[2 * WINDOW:nband] + bias_next, s[nband:]], axis=0)

    def softmax(h, e, s):
        sink = LOG2E * jnp.where(first_pair, sink_ref[heads_per_kv * h + e], sink_ref[heads_per_kv * h + 2 + e])
        m = jnp.maximum(jnp.max(s, axis=0, keepdims=True), sink)
        return jnp.exp2(s - m).astype(BF16), jnp.exp2(sink - m)

    def value(j, h, pb, p_sink):
        v_all = jnp.concatenate([vall[h * HEAD_DIM:(h + 1) * HEAD_DIM, j * WINDOW:j * WINDOW + nband],
                                 vmeta[h * HEAD_DIM:(h + 1) * HEAD_DIM, :]], axis=1)
        v_all = jnp.concatenate([v_all, ones_all], axis=0)
        o = _dot(v_all, jnp.concatenate([pb, p_pad], axis=0))
        denom = o[HEAD_DIM:HEAD_DIM + 1] + p_sink
        return o[:HEAD_DIM] * (1.0 / denom)

    def write(j, h, halves):
        rows = slice(j * WINDOW, (j + 1) * WINDOW)
        o_t = jnp.concatenate(halves, axis=0)
        for pp in range(2):
            o_ref[rows, (2 * h + pp) * LANES:(2 * h + pp + 1) * LANES] = (
                o_t[:, pp * LANES:(pp + 1) * LANES].T.astype(BF16))

    scores, probs, outs = {}, {}, {}

    def slot(t):
        for h, e in units:
            if 0 <= t - 2 < nblk:
                outs[h, e] = value(t - 2, h, *probs.pop((t - 2, h, e)))
                if e == 1:
                    write(t - 2, h, [outs.pop((h, 0)), outs.pop((h, 1))])
            if t < nblk:
                scores[t, h, e] = score(t, h, e)
            if 0 <= t - 1 < nblk:
                probs[t - 1, h, e] = softmax(h, e, scores.pop((t - 1, h, e)))

    return [functools.partial(slot, t) for t in range(nblk + 2)]


def _attn_kernel(*refs, tq, seq):
    for slot in _attn_slots(pl.program_id(1), *refs, tq=tq, seq=seq):
        slot()


def _attn(q, k4, vt, k4_meta, vt_meta, sink, *, tq):
    B, S, _ = q.shape
    r = tq // WINDOW
    last = S // WINDOW - 1
    prev_idx = lambda i: jnp.maximum(i * r - 1, 0)
    next_idx = lambda i: jnp.minimum((i + 1) * r, last)
    cur = lambda w: pl.BlockSpec((None, tq, w), lambda b, i: (b, i, 0))
    kw = 4 * LANES
    return pl.pallas_call(
        functools.partial(_attn_kernel, tq=tq, seq=S),
        grid=(B, S // tq),
        in_specs=[pl.BlockSpec(memory_space=pltpu.SMEM), cur(ATTN_WIDTH),
                  cur(kw),
                  pl.BlockSpec((None, WINDOW, kw), lambda b, i: (b, prev_idx(i), 0)),
                  pl.BlockSpec((None, WINDOW, kw), lambda b, i: (b, next_idx(i), 0)),
                  pl.BlockSpec((None, KV_WIDTH, tq), lambda b, i: (b, 0, i)),
                  pl.BlockSpec((None, KV_WIDTH, WINDOW), lambda b, i: (b, 0, prev_idx(i))),
                  pl.BlockSpec((None, KV_WIDTH, WINDOW), lambda b, i: (b, 0, next_idx(i))),
                  pl.BlockSpec((N_META, kw), lambda b, i: (0, 0)),
                  pl.BlockSpec((KV_WIDTH, LANES), lambda b, i: (0, 0))],
        out_specs=cur(ATTN_WIDTH),
        out_shape=jax.ShapeDtypeStruct((B, S, ATTN_WIDTH), BF16),
        compiler_params=pltpu.CompilerParams(
            dimension_semantics=("parallel", "parallel"), vmem_limit_bytes=VMEM_LIMIT_BYTES),
        name="attn",
    )(sink, q, k4, k4, k4, vt, vt, vt, k4_meta, vt_meta)


MXU_DEPTH = 256


def _factor(L):
    best = None
    for n1 in range(SUBLANES, L, SUBLANES):
        if L % n1 or (L // n1) % 8 == 0:
            continue
        n2p = _round_up(L // n1, 2 * SUBLANES)
        cost = (-(-2 * n1 // MXU_DEPTH) + -(-2 * n2p // MXU_DEPTH), abs(n1 - L ** 0.5))
        if best is None or cost < best[0]:
            best = (cost, n1)
    return best[1], L // best[1]


def _round_up(x, m):
    return (x + m - 1) // m * m


def _odd_tiles(rows):
    p = _round_up(rows, SUBLANES)
    return p if (p // SUBLANES) % 2 else p + SUBLANES


def _fmix_consts(S):
    L = S + N_META
    N1, N2 = _factor(L)
    N2p = _round_up(N2, 2 * SUBLANES)
    dims = dict(L=L, N1=N1, N2=N2, N2p=N2p, P=_odd_tiles(2 * N1), Q=_odd_tiles(N2p),
                Lp=_round_up(L + N2p - N2, SUBLANES))
    k1 = np.arange(N1, dtype=np.int64)
    ang1 = 2.0 * np.pi * ((k1[:, None] * k1[None, :]) % N1) / N1
    c1, s1 = np.cos(ang1) / np.sqrt(N1), np.sin(ang1) / np.sqrt(N1)
    f1 = np.block([[c1, s1], [-s1, c1]]).astype(np.float32)
    k = k1[:, None, None] + N1 * np.arange(N2p, dtype=np.int64)[None, :, None]
    n2 = np.arange(N2p, dtype=np.int64)[None, None, :]
    ang3 = 2.0 * np.pi * (((n2 + N_META) * k) % L) / L
    valid = ((np.arange(N2p) < N2)[None, :, None] & (np.arange(N2p) < N2)[None, None, :])
    c3 = np.where(valid, np.cos(ang3), 0.0) / np.sqrt(N2)
    s3 = np.where(valid, np.sin(ang3), 0.0) / np.sqrt(N2)
    m3 = np.concatenate([c3, s3], axis=2).astype(np.float32)
    c = np.arange(FOURIER_GROUP_WIDTH, dtype=np.int64)
    angc = 2.0 * np.pi * ((c[:, None] * c[None, :]) % FOURIER_GROUP_WIDTH) / FOURIER_GROUP_WIDTH
    cd = (np.concatenate([np.cos(angc), -np.sin(angc)], axis=1)
          / np.sqrt(FOURIER_GROUP_WIDTH)).astype(np.float32)
    return dims, f1, m3, cd


def _fmix_kernel(u_ref, um_ref, cd_ref, f1_ref, m3_ref, o_ref, *scratch,
                 S, L, Lp, N1, N2, N2p, P, Q, G, NB, chunk):
    W = FOURIER_GROUP_WIDTH
    xr, xi, a_s, y_s = (scratch[i * G:(i + 1) * G] for i in range(4))
    cd = cd_ref[...]
    lanes = lambda g: slice(g * W, (g + 1) * W)

    for r0 in range(0, S, chunk):
        for g in range(G):
            v = _dot(u_ref[r0:r0 + chunk, lanes(g)], cd)
            xr[g][r0:r0 + chunk, :] = v[:, :W]
            xi[g][r0:r0 + chunk, :] = v[:, W:]
    for g in range(G):
        vm = _dot(um_ref[:, lanes(g)], cd)
        xr[g][S:L, :] = vm[:, :W]
        xi[g][S:L, :] = vm[:, W:]
        xr[g][L:Lp, :] = jnp.zeros((Lp - L, W), F32)
        xi[g][L:Lp, :] = jnp.zeros((Lp - L, W), F32)

    for n2_0 in range(0, N2p, NB):
        n2s = range(n2_0, n2_0 + NB)
        cols = [jnp.concatenate([xr[g][pl.ds(n2, N1, stride=N2), :],
                                 xi[g][pl.ds(n2, N1, stride=N2), :]], axis=0)
                for n2 in n2s for g in range(G)]
        a = _dot(f1_ref[...], jnp.concatenate(cols, axis=1).astype(BF16))
        for j, n2 in enumerate(n2s):
            for g in range(G):
                a_s[g][n2 * P:n2 * P + 2 * N1, :] = a[:, lanes(j * G + g)]

    for k1 in range(N1):
        b = jnp.concatenate(
            [jnp.concatenate([a_s[g][pl.ds(k1, N2p, stride=P), :],
                              a_s[g][pl.ds(N1 + k1, N2p, stride=P), :]], axis=0) for g in range(G)], axis=1)
        z = _dot(m3_ref[k1], b.astype(BF16))
        for g in range(G):
            y_s[g][k1 * Q:k1 * Q + N2p, :] = z[:, lanes(g)]

    def pair(g, k2):
        return jnp.concatenate([y_s[g][pl.ds(k2, N1, stride=Q), :],
                                y_s[g][pl.ds(k2 + 1, N1, stride=Q), :]], axis=0)

    for g in range(G):
        o_ref[0:2 * N1 - N_META, lanes(g)] = pair(g, 0)[N_META:].astype(BF16)

    n_pairs = N2 // 2 - 1
    u3 = max(d for d in (4, 2, 1) if n_pairs % d == 0)

    def p3(t, carry):
        for uu in range(u3):
            k2 = 2 * (t * u3 + uu) + 2
            r0 = pl.multiple_of(k2 * N1 - N_META, 2 * SUBLANES)
            for g in range(G):
                o_ref[pl.ds(r0, 2 * N1), lanes(g)] = pair(g, k2).astype(BF16)
        return carry

    lax.fori_loop(0, n_pairs // u3, p3, 0)
    if N2 % 2:
        r0 = (N2 - 1) * N1 - N_META
        for g in range(G):
            o_ref[r0:r0 + N1, lanes(g)] = y_s[g][pl.ds(N2 - 1, N1, stride=Q), :].astype(BF16)


def _fmix_scratch_rows(dims):
    return (dims["Lp"], dims["Lp"], dims["N2p"] * dims["P"], dims["N1"] * dims["Q"])


def _fmix(u, u_meta, *, groups, chunk):
    B, S, _ = u.shape
    dims, f1, m3, cd = _fmix_consts(S)
    W, G = FOURIER_GROUP_WIDTH, groups
    nb = N_FOURIER_GROUPS // G
    assert dims["N2p"] % nb == 0 and S % chunk == 0
    const = lambda shape: pl.BlockSpec(shape, lambda b, g: (0,) * len(shape))
    scratch = [pltpu.VMEM((rows, W), F32) for rows in _fmix_scratch_rows(dims) for _ in range(G)]
    return pl.pallas_call(
        functools.partial(_fmix_kernel, S=S, chunk=chunk, G=G, NB=nb, **dims),
        grid=(B, N_FOURIER_GROUPS // G),
        in_specs=[pl.BlockSpec((None, S, G * W), lambda b, g: (b, 0, g)),
                  pl.BlockSpec((N_META, G * W), lambda b, g: (0, g)),
                  const(cd.shape), const(f1.shape), const(m3.shape)],
        out_specs=pl.BlockSpec((None, S, G * W), lambda b, g: (b, 0, g)),
        out_shape=jax.ShapeDtypeStruct((B, S, FOURIER_WIDTH), BF16),
        scratch_shapes=scratch,
        compiler_params=pltpu.CompilerParams(
            dimension_semantics=("parallel", "parallel"), vmem_limit_bytes=VMEM_LIMIT_BYTES),
        name="fmix",
    )(u, u_meta, jnp.asarray(cd).astype(BF16), jnp.asarray(f1).astype(BF16), jnp.asarray(m3).astype(BF16))


def _post_stages(x_ref, a_ref, f_ref, gate_ref, wao_ref, wf_ref, wout_ref, g2_ref, wup_ref, wdown_ref,
                 gfin_ref, y_ref, *, ff_chunk, sub):
    tm = x_ref.shape[0]
    blocks = [slice(r, r + sub) for r in range(0, tm, sub)]
    state = {}

    def mixer(rows):
        a = _dot(a_ref[rows, :], wao_ref[...])
        f = _dot(f_ref[rows, :], wf_ref[...])
        merged = (gate_ref[rows, :D_MODEL].astype(F32) * a + gate_ref[rows, D_MODEL:].astype(F32) * f).astype(BF16)
        h = x_ref[rows, :] + _dot(merged, wout_ref[...])
        state[rows.start] = (h, _rms(h, g2_ref[...]).astype(BF16))

    def mlp_chunk(c):
        for rows in blocks:
            h, n = state[rows.start]
            t = _dot(n, wup_ref[:, c * ff_chunk:(c + 1) * ff_chunk])
            r = jnp.square(jnp.maximum(t, 0.0)).astype(BF16)
            state[rows.start] = (h + _dot(r, wdown_ref[c * ff_chunk:(c + 1) * ff_chunk, :]), n)

    def final():
        for rows in blocks:
            y_ref[rows, :] = _rms(state[rows.start][0], gfin_ref[...])

    return ([functools.partial(mixer, rows) for rows in blocks]
            + [functools.partial(mlp_chunk, c) for c in range(D_FF // ff_chunk)] + [final])


def _post_kernel(*refs, ff_chunk, sub):
    for stage in _post_stages(*refs, ff_chunk=ff_chunk, sub=sub):
        stage()


def _post(x, a, f, gates, w_ao, w_f, w_out, g2, w_up, w_down, g_fin, *, tm, sub, ff_chunk):
    B, S, _ = x.shape
    tok = lambda w: pl.BlockSpec((None, tm, w), lambda b, i: (b, i, 0))
    const = lambda shape: pl.BlockSpec(shape, lambda b, i: (0,) * len(shape), pipeline_mode=pl.Buffered(1))
    return pl.pallas_call(
        functools.partial(_post_kernel, ff_chunk=ff_chunk, sub=sub),
        grid=(B, S // tm),
        in_specs=[tok(D_MODEL), tok(ATTN_WIDTH), tok(FOURIER_WIDTH), tok(N_BRANCHES * D_MODEL),
                  const(w_ao.shape), const(w_f.shape), const(w_out.shape), const((1, D_MODEL)),
                  const(w_up.shape), const(w_down.shape), const((1, D_MODEL))],
        out_specs=tok(D_MODEL),
        out_shape=jax.ShapeDtypeStruct((B, S, D_MODEL), F32),
        compiler_params=pltpu.CompilerParams(
            dimension_semantics=("parallel", "parallel"), vmem_limit_bytes=POST_VMEM_LIMIT_BYTES),
        name="post",
    )(x, a, f, gates, w_ao, w_f, w_out, g2, w_up, w_down, g_fin)


def _rope_table(first_pos, n_pos):
    half = ROT_DIM // 2
    inv_freq = ROPE_THETA ** (-np.arange(half, dtype=np.float64) / half)
    ang = np.arange(first_pos, first_pos + n_pos, dtype=np.float64)[:, None] * inv_freq[None, :]
    cos, sin = np.cos(ang), np.sin(ang)
    rest = HEAD_DIM - ROT_DIM
    head = lambda a, b, fill: np.concatenate([a, b, np.full((n_pos, rest), fill)], axis=1)
    zero = np.zeros_like(sin)
    per_head = (head(cos, cos, 1.0), head(zero, sin, 0.0), head(-sin, zero, 0.0))
    table = np.concatenate([np.tile(t, (1, LANES // HEAD_DIM)) for t in per_head], axis=1)
    return jnp.asarray(table.astype(np.float32))


def _plan(S):
    dims = _fmix_consts(S)[0]
    group_bytes = sum(_fmix_scratch_rows(dims)) * FOURIER_GROUP_WIDTH * 4
    return dict(tm_proj=min(S, 1024), sub_proj=min(S, 256), tm_post=min(S, 1024), sub_post=min(S, 512),
                ff_chunk=512, tq=min(S, 32 * WINDOW), fmix_chunk=min(S, 512),
                fmix_groups=N_FOURIER_GROUPS if N_FOURIER_GROUPS * group_bytes <= VMEM_LIMIT_BYTES // 2 else 1)


_POST_MATRICES = ("w_attn_out", "w_fourier", "w_out", "w_mlp_up", "w_mlp_down")


def _trunk(x, meta_parts, table, wts, post_w):
    k4_m, vt_m, u_m = meta_parts
    plan = _plan(x.shape[1])
    cast = () if post_w is not None else tuple(wts[k] for k in _POST_MATRICES)
    q, k4, vt, u, gates, *made = _proj(x, table, wts["norm_mix_g"], wts["w_in"], wts["b_gate"],
                                       tm=plan["tm_proj"], sub=plan["sub_proj"], transpose_v=True, cast=cast)
    w_ao, w_f, w_out, w_up, w_down = post_w if post_w is not None else made
    a = _attn(q, k4, vt, k4_m, vt_m, wts["attn_sink"], tq=plan["tq"])
    f = _fmix(u, u_m, groups=plan["fmix_groups"], chunk=plan["fmix_chunk"])
    y = _post(x, a, f, gates, w_ao, w_f, w_out, wts["norm_mlp_g"], w_up, w_down, wts["norm_final_g"],
              tm=plan["tm_post"], sub=plan["sub_post"], ff_chunk=plan["ff_chunk"])
    return y, (w_ao, w_f, w_out, w_up, w_down)


def kernel(x_prompt, x_sample, meta_tokens, norm_mix_g, w_in, b_gate, attn_sink, w_attn_out, w_fourier, w_out,
           norm_mlp_g, w_mlp_up, w_mlp_down, norm_final_g):
    assert w_in.shape[0] == 1, "single-layer trunk: meta-token outputs are never consumed"
    wts = dict(
        norm_mix_g=norm_mix_g[0][None, :], w_in=w_in[0].astype(BF16), b_gate=b_gate[0][None, :],
        attn_sink=attn_sink[0], w_attn_out=w_attn_out[0], w_fourier=w_fourier[0], w_out=w_out[0],
        norm_mlp_g=norm_mlp_g[0][None, :], w_mlp_up=w_mlp_up[0], w_mlp_down=w_mlp_down[0],
        norm_final_g=norm_final_g[None, :])
    table = _rope_table(N_META, max(x_prompt.shape[1], x_sample.shape[1]))
    _, k4_m, v_m, u_m, _ = _proj(meta_tokens[None], _rope_table(0, N_META),
                                 wts["norm_mix_g"], wts["w_in"], wts["b_gate"], tm=N_META, sub=N_META,
                                 transpose_v=False)
    vt_m = jnp.pad(v_m[0].T, ((0, 0), (0, LANES - N_META)))
    meta_parts = (k4_m[0], vt_m, u_m[0])
    y_prompt, post_w = _trunk(x_prompt, meta_parts, table, wts, None)
    y_sample, _ = _trunk(x_sample, meta_parts, table, wts, post_w)
    return (y_prompt, y_sample)
```
